```python
import math
import jax, jax.numpy as jnp
from jax import lax
import numpy as np

D_MODEL = 1024
BATCH = 16
SEQ = 256
DEPTH = 1
DEC_BATCH = 4
DEC_SEQ = 4096
PAST_LEN = 512

GRID_W = 64
N_ATT_HEADS = 8
DIFF_HEAD_DIM = 64
V_HEAD_DIM = 2 * DIFF_HEAD_DIM
QK_WIDTH = N_ATT_HEADS * 2 * DIFF_HEAD_DIM
D_ATT = N_ATT_HEADS * V_HEAD_DIM
D_LRU = 1024
N_LRU_BLOCKS = 8
LRU_BLOCK = D_LRU // N_LRU_BLOCKS
CONV_WIDTH = 4
LRU_C = 8.0
D_MIX = D_ATT + D_LRU
SPLITS = (QK_WIDTH, 2 * QK_WIDTH, 2 * QK_WIDTH + D_ATT, 2 * QK_WIDTH + 2 * D_ATT, 2 * QK_WIDTH + 2 * D_ATT + D_LRU)
D_IN_TOTAL = 2 * QK_WIDTH + 2 * D_ATT + 2 * D_LRU
ROPE_BASE = 10000.0
Q_BLOCK = 128
EPS = 1e-6

kernel_name = "hybrid_diffattn_rglru_prefix_step"


def rms_norm(x, g):
    xf = x.astype(jnp.float32)
    y = xf * lax.rsqrt(jnp.mean(xf * xf, axis=-1, keepdims=True) + EPS)
    return (y * g.astype(jnp.float32)).astype(x.dtype)


def modulation(cond, w_ada, b_ada):
    m = jax.nn.silu(cond) @ w_ada + b_ada
    shift, scale, gate = jnp.split(m, 3, axis=-1)
    return shift, scale, gate


def axial_rope(t):
    T = t.shape[1]
    rows = T // GRID_W
    row = jnp.repeat(jnp.arange(rows, dtype=jnp.float32), GRID_W)
    col = jnp.tile(jnp.arange(GRID_W, dtype=jnp.float32), rows)
    half = DIFF_HEAD_DIM // 2
    nf = half // 2
    inv = ROPE_BASE ** (-jnp.arange(nf, dtype=jnp.float32) * 2.0 / half)

    def rot(x, pos):
        ang = pos[:, None] * inv[None, :]
        cos = jnp.cos(ang)[None, :, None, None, :]
        sin = jnp.sin(ang)[None, :, None, None, :]
        xf = x.astype(jnp.float32)
        x1, x2 = xf[..., :nf], xf[..., nf:]
        return jnp.concatenate([x1 * cos - x2 * sin, x1 * sin + x2 * cos], axis=-1).astype(x.dtype)

    return jnp.concatenate([rot(t[..., :half], row), rot(t[..., half:], col)], axis=-1)


def diff_attention(q, k, v, lam):
    B, S, H, _, d = q.shape
    nb = S // Q_BLOCK
    scale = 1.0 / math.sqrt(d)
    qb = q.reshape(B, nb, Q_BLOCK, H, 2, d).transpose(1, 0, 2, 3, 4, 5)

    def one_block(qblk):
        s = jnp.einsum('bqhmd,bkhmd->bhmqk', qblk, k).astype(jnp.float32) * scale
        p = jax.nn.softmax(s, axis=-1)
        a = p[:, :, 0] - lam * p[:, :, 1]
        return jnp.einsum('bhqk,bkhe->bqhe', a.astype(v.dtype), v)

    o = lax.map(one_block, qb)
    return o.transpose(1, 0, 2, 3, 4).reshape(B, S, H, v.shape[-1])


def centred_conv(x, w, b):
    T = x.shape[1]
    left = (CONV_WIDTH - 1) // 2
    right = CONV_WIDTH - 1 - left
    xp = jnp.pad(x, ((0, 0), (left, right), (0, 0)))
    y = b
    for j in range(CONV_WIDTH):
        y = y + xp[:, j:j + T] * w[j]
    return y


def lru_coeffs(x, w_r, b_r, w_i, b_i, lam):
    B, T, D = x.shape
    xr = x.reshape(B, T, N_LRU_BLOCKS, LRU_BLOCK)
    r = jax.nn.sigmoid(jnp.einsum('btnc,ncd->btnd', xr, w_r).reshape(B, T, D) + b_r)
    i = jax.nn.sigmoid(jnp.einsum('btnc,ncd->btnd', xr, w_i).reshape(B, T, D) + b_i)
    log_a = -LRU_C * r.astype(jnp.float32) * jax.nn.softplus(-lam.astype(jnp.float32))
    a = jnp.exp(log_a)
    mult = jnp.sqrt(-jnp.expm1(2.0 * log_a))
    return a, mult * (i * x).astype(jnp.float32)


def linear_scan(a, b, h0, reverse):
    def comb(l, r):
        al, bl = l
        ar, br = r
        return al * ar, ar * bl + br
    A, Bc = lax.associative_scan(comb, (a, b), axis=1, reverse=reverse)
    return Bc + A * h0.astype(jnp.float32)[:, None, :]


def sublayer(x, shift, scale, gate, ctx_k, ctx_v, h0_f, h0_b, use_rope, layer,
             g_pre, w_in, lq1, lk1, lq2, lk2, g_subln, conv_w, conv_b,
             w_r, b_r, w_i, b_i, lru_lam, w_out, g_post):
    B, T, _ = x.shape
    h = rms_norm(x, g_pre) * (1.0 + scale) + shift
    p = h @ w_in
    q, k, v, g_att, x_lru, g_lru = jnp.split(p, SPLITS, axis=-1)
    q = q.reshape(B, T, N_ATT_HEADS, 2, DIFF_HEAD_DIM)
    k = k.reshape(B, T, N_ATT_HEADS, 2, DIFF_HEAD_DIM)
    v = v.reshape(B, T, N_ATT_HEADS, V_HEAD_DIM)
    if use_rope:
        q = axial_rope(q)
        k = axial_rope(k)
    if ctx_k is None:
        k_all, v_all = k, v
    else:
        k_all = jnp.concatenate([ctx_k, k], axis=1)
        v_all = jnp.concatenate([ctx_v, v], axis=1)
    lam_init = 0.8 - 0.6 * math.exp(-0.3 * layer)
    lam = (jnp.exp(jnp.sum(lq1.astype(jnp.float32) * lk1.astype(jnp.float32)))
           - jnp.exp(jnp.sum(lq2.astype(jnp.float32) * lk2.astype(jnp.float32))) + lam_init)
    o = diff_attention(q, k_all, v_all, lam)
    att = (rms_norm(o, g_subln) * (1.0 - lam_init)).reshape(B, T, D_ATT) * jax.nn.silu(g_att)
    u = centred_conv(x_lru, conv_w, conv_b)
    a_f, b_f = lru_coeffs(u, w_r[0], b_r[0], w_i[0], b_i[0], lru_lam[0])
    a_b, b_b = lru_coeffs(u, w_r[1], b_r[1], w_i[1], b_i[1], lru_lam[1])
    hf = linear_scan(a_f, b_f, h0_f, False)
    hb = linear_scan(a_b, b_b, h0_b, True)
    lru = (hf + hb).astype(x.dtype) * jax.nn.silu(g_lru)
    out = jnp.concatenate([att, lru], axis=-1) @ w_out
    x_new = x + gate * rms_norm(out, g_post)
    k_flat = k.reshape(B, T, N_ATT_HEADS, 2 * DIFF_HEAD_DIM)
    state = jnp.stack([hf[:, -1], hb[:, 0]], axis=1).astype(x.dtype)
    return x_new, k_flat, v, state


def setup_inputs(seed: int = 0) -> dict:
    key = jax.random.key(seed)
    ks = jax.random.split(key, 32)
    f32 = jnp.float32
    nrm = lambda k, s, sc: jax.random.normal(k, s, f32) * sc
    a0 = jax.random.uniform(ks[20], (DEPTH, 2, D_LRU), f32, 0.9, 0.999)
    s0 = a0 ** (1.0 / LRU_C)
    return {
        "x_prompt": nrm(ks[0], (BATCH, SEQ, D_MODEL), 1.0),
        "x_sample": nrm(ks[1], (DEC_BATCH, DEC_SEQ, D_MODEL), 1.0),
        "cache_k": nrm(ks[2], (DEC_BATCH, DEPTH, PAST_LEN, N_ATT_HEADS, 2 * DIFF_HEAD_DIM), 1.0),
        "cache_v": nrm(ks[3], (DEC_BATCH, DEPTH, PAST_LEN, N_ATT_HEADS, V_HEAD_DIM), 1.0),
        "state_lru": nrm(ks[4], (DEC_BATCH, DEPTH, 2, D_LRU), 0.5),
        "c": nrm(ks[5], (DEC_BATCH, D_MODEL), 1.0),
        "c_ctx": nrm(ks[6], (D_MODEL,), 1.0),
        "w_ada": nrm(ks[7], (DEPTH, D_MODEL, 3 * D_MODEL), 0.5 * D_MODEL ** -0.5),
        "b_ada": nrm(ks[8], (DEPTH, 3 * D_MODEL), 0.02),
        "g_pre": 1.0 + nrm(ks[9], (DEPTH, D_MODEL), 0.02),
        "w_in": nrm(ks[10], (DEPTH, D_MODEL, D_IN_TOTAL), D_MODEL ** -0.5),
        "lambda_q1": nrm(ks[11], (DEPTH, DIFF_HEAD_DIM), 0.1),
        "lambda_k1": nrm(ks[12], (DEPTH, DIFF_HEAD_DIM), 0.1),
        "lambda_q2": nrm(ks[13], (DEPTH, DIFF_HEAD_DIM), 0.1),
        "lambda_k2": nrm(ks[14], (DEPTH, DIFF_HEAD_DIM), 0.1),
        "g_subln": 1.0 + nrm(ks[15], (DEPTH, V_HEAD_DIM), 0.02),
        "conv_w": nrm(ks[16], (DEPTH, CONV_WIDTH, D_LRU), CONV_WIDTH ** -0.5),
        "conv_b": nrm(ks[17], (DEPTH, D_LRU), 0.02),
        "w_rgate": nrm(ks[18], (DEPTH, 2, N_LRU_BLOCKS, LRU_BLOCK, LRU_BLOCK), LRU_BLOCK ** -0.5),
        "b_rgate": nrm(ks[19], (DEPTH, 2, D_LRU), 0.02),
        "w_igate": nrm(ks[21], (DEPTH, 2, N_LRU_BLOCKS, LRU_BLOCK, LRU_BLOCK), LRU_BLOCK ** -0.5),
        "b_igate": nrm(ks[22], (DEPTH, 2, D_LRU), 0.02),
        "lru_lambda": jnp.log(s0) - jnp.log1p(-s0),
        "w_out": nrm(ks[23], (DEPTH, D_MIX, D_MODEL), D_MIX ** -0.5),
        "g_post": 1.0 + nrm(ks[24], (DEPTH, D_MODEL), 0.02),
    }


def reference(x_prompt, x_sample, cache_k, cache_v, state_lru, c, c_ctx, w_ada, b_ada, g_pre, w_in,
              lambda_q1, lambda_k1, lambda_q2, lambda_k2, g_subln, conv_w, conv_b,
              w_rgate, b_rgate, w_igate, b_igate, lru_lambda, w_out, g_post):
    y_p = x_prompt
    y_s = x_sample
    Bp = x_prompt.shape[0]
    Bd, Kc = cache_k.shape[0], cache_k.shape[2]
    zeros = jnp.zeros((Bp, D_LRU), jnp.float32)
    new_ks, new_vs, new_sts = [], [], []
    for l in range(DEPTH):
        w = (g_pre[l], w_in[l], lambda_q1[l], lambda_k1[l], lambda_q2[l], lambda_k2[l], g_subln[l],
             conv_w[l], conv_b[l], w_rgate[l], b_rgate[l], w_igate[l], b_igate[l], lru_lambda[l],
             w_out[l], g_post[l])
        sh, sc, gt = modulation(c_ctx, w_ada[l], b_ada[l])
        y_p, k_l, v_l, st_l = sublayer(y_p, sh, sc, gt, None, None, zeros, zeros, False, l, *w)
        new_ks.append(k_l)
        new_vs.append(v_l)
        new_sts.append(st_l)
        sh, sc, gt = modulation(c[:, None, :], w_ada[l], b_ada[l])
        ctx_k = cache_k[:, l].reshape(Bd, Kc, N_ATT_HEADS, 2, DIFF_HEAD_DIM)
        y_s, _, _, _ = sublayer(y_s, sh, sc, gt, ctx_k, cache_v[:, l], state_lru[:, l, 0], state_lru[:, l, 1],
                                True, l, *w)
    new_k = jnp.stack(new_ks, axis=1)
    new_v = jnp.stack(new_vs, axis=1)
    new_state_lru = jnp.stack(new_sts, axis=1)
    return (y_p, y_s, new_k, new_v, new_state_lru)
```

```python
import functools
import math

import jax
import jax.numpy as jnp
import numpy as np
from jax import lax
from jax.experimental import pallas as pl
from jax.experimental.pallas import tpu as pltpu

F32 = jnp.float32
BF16 = jnp.bfloat16

D_MODEL = 1024
GRID_W = 64
N_HEADS = 8
DIFF_HEAD_DIM = 64
HEAD_W = 2 * DIFF_HEAD_DIM
D_ATT = N_HEADS * HEAD_W
D_LRU = 1024
N_LRU_BLOCKS = 8
LRU_BLOCK = D_LRU // N_LRU_BLOCKS
LRU_C = 8.0
N_GROUPS = 6
ROPE_BASE = 10000.0
EPS = 1e-6
LAM_INIT = 0.8 - 0.6 * math.exp(-0.3 * 0)

V7X_VMEM_LIMIT_BYTES = 56 * 1024 * 1024
SUBLANES = 8
ROPE_SWAP = DIFF_HEAD_DIM // 4


def _silu(x):
    return x * jax.nn.sigmoid(x)


def _dot(a, b):
    return jnp.dot(a, b, preferred_element_type=F32)


def _dot_nt(a, b):
    return lax.dot_general(a, b, (((1,), (1,)), ((), ())), preferred_element_type=F32)


def _mod_kernel(cond_ref, w_ref, b_ref, o_ref):
    s = _silu(cond_ref[...])
    o_ref[...] = _dot(s.astype(BF16), w_ref[...].astype(BF16)) + b_ref[...]


def _modulation(cond, w_ada, b_ada):
    n = cond.shape[0]
    return pl.pallas_call(
        _mod_kernel,
        grid=(3,),
        in_specs=[
            pl.BlockSpec((n, D_MODEL), lambda j: (0, 0)),
            pl.BlockSpec((D_MODEL, D_MODEL), lambda j: (0, j)),
            pl.BlockSpec((1, D_MODEL), lambda j: (0, j)),
        ],
        out_specs=pl.BlockSpec((n, D_MODEL), lambda j: (0, j)),
        out_shape=jax.ShapeDtypeStruct((n, 3 * D_MODEL), F32),
        name="modulation",
    )(cond, w_ada, b_ada)


def _rope_tables(t_len):
    half = DIFF_HEAD_DIM // 2
    nf = half // 2
    t = np.arange(t_len)
    row = (t // GRID_W).astype(np.float32)
    col = (t % GRID_W).astype(np.float32)
    inv = (ROPE_BASE ** (-np.arange(nf, dtype=np.float32) * 2.0 / half)).astype(np.float32)
    lane = np.arange(HEAD_W) % DIFF_HEAD_DIM
    use_row = lane < half
    freq = (lane % half) % nf
    first = (lane % half) < nf
    pos = np.where(use_row[None, :], row[:, None], col[:, None]).astype(np.float32)
    ang = (pos * inv[freq][None, :]).astype(np.float32).astype(np.float64)
    cos = np.cos(ang).astype(np.float32)
    sin = (np.sin(ang) * np.where(first, -1.0, 1.0)[None, :]).astype(np.float32)
    return jnp.asarray(cos), jnp.asarray(sin)


def _inproj_kernel(*refs, use_rope, emit_kv_f32):
    x_ref, scale_ref, shift_ref, gpre_ref, w_ref = refs[:5]
    refs = refs[5:]
    if use_rope:
        cos_ref, sin_ref = refs[:2]
        refs = refs[2:]
    q_ref, k_ref, v_ref, gatt_ref, xlru_ref, glru_ref = refs[:6]
    refs = refs[6:]
    if emit_kv_f32:
        kf_ref, vf_ref = refs

    x = x_ref[...]
    ms = jnp.mean(x * x, axis=-1, keepdims=True)
    y = x * lax.rsqrt(ms + EPS) * gpre_ref[...]
    h = (y * (1.0 + scale_ref[...]) + shift_ref[...]).astype(BF16)

    def proj(g):
        return _dot(h, w_ref[:, g * D_MODEL:(g + 1) * D_MODEL])

    if use_rope:
        cos = cos_ref[...]
        sin = sin_ref[...]
        lane = lax.broadcasted_iota(jnp.int32, cos.shape, 1)
        take_next = (lane % (2 * ROPE_SWAP)) < ROPE_SWAP

        def rope(p):
            outs = []
            for hd in range(N_HEADS):
                xh = p[:, hd * HEAD_W:(hd + 1) * HEAD_W]
                partner = jnp.where(take_next,
                                    pltpu.roll(xh, HEAD_W - ROPE_SWAP, 1),
                                    pltpu.roll(xh, ROPE_SWAP, 1))
                outs.append(xh * cos + partner * sin)
            return jnp.concatenate(outs, axis=-1)
    else:
        def rope(p):
            return p

    q = rope(proj(0)) * (1.0 / math.sqrt(DIFF_HEAD_DIM))
    q_ref[...] = q.astype(BF16)
    k = rope(proj(1))
    k_ref[...] = k.astype(BF16)
    v = proj(2)
    v_ref[...] = v.astype(BF16)
    if emit_kv_f32:
        kf_ref[...] = k
        vf_ref[...] = v
    gatt_ref[...] = proj(3)
    xlru_ref[...] = proj(4)
    glru_ref[...] = proj(5)


def _inproj(x, scale, shift, g_pre, w_in_bf16, *, t_len, use_rope, emit_kv_f32, tm=256):
    n_tok = x.shape[0]
    tiles_per_batch = t_len // tm
    nb = scale.shape[0]
    if nb == 1:
        mod_map = lambda i: (0, 0, 0)
    else:
        mod_map = lambda i: (i // tiles_per_batch, 0, 0)
    tok_spec = pl.BlockSpec((tm, D_MODEL), lambda i: (i, 0))
    in_specs = [
        tok_spec,
        pl.BlockSpec((None, 1, D_MODEL), mod_map),
        pl.BlockSpec((None, 1, D_MODEL), mod_map),
        pl.BlockSpec((1, D_MODEL), lambda i: (0, 0)),
        pl.BlockSpec((D_MODEL, N_GROUPS * D_MODEL), lambda i: (0, 0)),
    ]
    args = [x, scale, shift, g_pre, w_in_bf16]
    if use_rope:
        cos, sin = _rope_tables(t_len)
        rope_spec = pl.BlockSpec((tm, HEAD_W), lambda i: (i % tiles_per_batch, 0))
        in_specs += [rope_spec, rope_spec]
        args += [cos, sin]
    out_specs = [tok_spec] * 6
    out_shape = [jax.ShapeDtypeStruct((n_tok, D_MODEL), BF16)] * 3 + \
                [jax.ShapeDtypeStruct((n_tok, D_MODEL), F32)] * 3
    if emit_kv_f32:
        out_specs += [tok_spec] * 2
        out_shape += [jax.ShapeDtypeStruct((n_tok, D_MODEL), F32)] * 2
    return pl.pallas_call(
        functools.partial(_inproj_kernel, use_rope=use_rope, emit_kv_f32=emit_kv_f32),
        grid=(n_tok // tm,),
        in_specs=in_specs,
        out_specs=out_specs,
        out_shape=out_shape,
        compiler_params=pltpu.CompilerParams(
            dimension_semantics=("arbitrary",), vmem_limit_bytes=V7X_VMEM_LIMIT_BYTES),
        name="inproj_rope" if use_rope else "inproj",
    )(*args)


def _attn_kernel(*refs, has_cache):
    lam_ref, gsub_ref, q_ref, k_ref, v_ref, gatt_ref = refs[:6]
    refs = refs[6:]
    if has_cache:
        kc_ref, vc_ref = refs[:2]
        refs = refs[2:]
    o_ref, = refs

    lp = lam_ref[...]
    lam = (jnp.exp(jnp.sum(lp[0:1] * lp[1:2], axis=-1, keepdims=True))
           - jnp.exp(jnp.sum(lp[2:3] * lp[3:4], axis=-1, keepdims=True)) + LAM_INIT)

    q = q_ref[...]
    lane = lax.broadcasted_iota(jnp.int32, q.shape, 1)
    zero = jnp.zeros_like(q)
    q_maps = (jnp.where(lane < DIFF_HEAD_DIM, q, zero), jnp.where(lane >= DIFF_HEAD_DIM, q, zero))

    k_parts = [k_ref[...]]
    v_parts = [v_ref[...]]
    if has_cache:
        k_parts.insert(0, kc_ref[...].astype(BF16))
        v_parts.insert(0, vc_ref[...].astype(BF16))

    probs = []
    for qm in q_maps:
        s_parts = [_dot_nt(qm, kp) for kp in k_parts]
        m = functools.reduce(jnp.maximum, [jnp.max(s, axis=-1, keepdims=True) for s in s_parts])
        e_parts = [jnp.exp(s - m) for s in s_parts]
        denom = functools.reduce(jnp.add, [jnp.sum(e, axis=-1, keepdims=True) for e in e_parts])
        inv = 1.0 / denom
        probs.append([e * inv for e in e_parts])

    o = None
    for p1, p2, vp in zip(probs[0], probs[1], v_parts):
        a = (p1 - lam * p2).astype(BF16)
        part = _dot(a, vp)
        o = part if o is None else o + part

    ms = jnp.mean(o * o, axis=-1, keepdims=True)
    on = o * lax.rsqrt(ms + EPS) * gsub_ref[...] * (1.0 - LAM_INIT)
    o_ref[...] = (on * _silu(gatt_ref[...])).astype(BF16)


def _attention(lam_params, g_subln, q, k, v, g_att, cache_k=None, cache_v=None, *, tq=128):
    b, t_len, _ = q.shape
    has_cache = cache_k is not None
    q_spec = pl.BlockSpec((None, tq, HEAD_W), lambda bi, h, qi: (bi, qi, h))
    kv_spec = pl.BlockSpec((None, t_len, HEAD_W), lambda bi, h, qi: (bi, 0, h))
    in_specs = [
        pl.BlockSpec(lam_params.shape, lambda bi, h, qi: (0, 0)),
        pl.BlockSpec((1, HEAD_W), lambda bi, h, qi: (0, 0)),
        q_spec, kv_spec, kv_spec, q_spec,
    ]
    args = [lam_params, g_subln, q, k, v, g_att]
    if has_cache:
        kc = cache_k.shape[1]
        c_spec = pl.BlockSpec((None, kc, HEAD_W), lambda bi, h, qi: (bi, 0, h))
        in_specs += [c_spec, c_spec]
        args += [cache_k, cache_v]
    return pl.pallas_call(
        functools.partial(_attn_kernel, has_cache=has_cache),
        grid=(b, N_HEADS, t_len // tq),
        in_specs=in_specs,
        out_specs=q_spec,
        out_shape=jax.ShapeDtypeStruct((b, t_len, D_ATT), BF16),
        compiler_params=pltpu.CompilerParams(
            dimension_semantics=("arbitrary",) * 3, vmem_limit_bytes=V7X_VMEM_LIMIT_BYTES),
        name="diff_attn_cache" if has_cache else "diff_attn",
    )(*args)


def _lru_kernel(x_ref, g_ref, cw_ref, cb_ref, w_ref, b_ref, lam_ref, h0_ref,
                o_ref, st_ref,
                xp, a_f, b_f, a_b, b_b, hin_f, hin_b,
                *, t_len, chunk, pitch, rows):
    n_chunks = t_len // chunk
    chunks_per_tile = rows // chunk

    pad = jnp.zeros((SUBLANES, LRU_BLOCK), F32)
    xp[pl.ds(0, SUBLANES), :] = pad
    xp[pl.ds(SUBLANES, t_len), :] = x_ref[...]
    xp[pl.ds(SUBLANES + t_len, SUBLANES), :] = pad

    lam = lam_ref[...]
    sp = jnp.maximum(-lam, 0.0) + jnp.log1p(jnp.exp(-jnp.abs(lam)))
    cw = cw_ref[...]
    cb = cb_ref[...]
    wcat = w_ref[...]
    bcat = b_ref[...]

    def gate_tile(r, carry):
        t0 = pl.multiple_of(r * rows, SUBLANES)
        u = cb
        for j in range(4):
            u = u + xp[pl.ds(t0 + SUBLANES - 1 + j, rows), :] * cw[j:j + 1]
        z = _dot(u.astype(BF16), wcat) + bcat
        for d, (a_s, b_s) in enumerate(((a_f, b_f), (a_b, b_b))):
            rg = jax.nn.sigmoid(z[:, (2 * d) * LRU_BLOCK:(2 * d + 1) * LRU_BLOCK])
            ig = jax.nn.sigmoid(z[:, (2 * d + 1) * LRU_BLOCK:(2 * d + 2) * LRU_BLOCK])
            log_a = -LRU_C * rg * sp[d:d + 1]
            a = jnp.exp(log_a)
            bb = jnp.sqrt(jnp.tanh(-log_a) * (1.0 + a * a)) * (ig * u)
            for cc in range(chunks_per_tile):
                dst = pl.multiple_of((r * chunks_per_tile + cc) * pitch, SUBLANES)
                a_s[pl.ds(dst, chunk), :] = a[cc * chunk:(cc + 1) * chunk]
                b_s[pl.ds(dst, chunk), :] = bb[cc * chunk:(cc + 1) * chunk]
        return carry

    lax.fori_loop(0, t_len // rows, gate_tile, 0)

    def local_step(l, carry):
        hf, pf, hb, pb = carry
        rf = pl.ds(l, n_chunks, stride=pitch)
        a = a_f[rf, :]
        hf = a * hf + b_f[rf, :]
        pf = a * pf
        b_f[rf, :] = hf
        a_f[rf, :] = pf
        rb = pl.ds(chunk - 1 - l, n_chunks, stride=pitch)
        a = a_b[rb, :]
        hb = a * hb + b_b[rb, :]
        pb = a * pb
        b_b[rb, :] = hb
        a_b[rb, :] = pb
        return hf, pf, hb, pb

    z0 = jnp.zeros((n_chunks, LRU_BLOCK), F32)
    o0 = jnp.ones((n_chunks, LRU_BLOCK), F32)
    lax.fori_loop(0, chunk, local_step, (z0, o0, z0, o0))

    h0 = h0_ref[...]

    def carry_step(c, carry):
        hf, hb = carry
        hin_f[pl.ds(c, 1), :] = hf
        last = c * pitch + chunk - 1
        hf = b_f[pl.ds(last, 1), :] + a_f[pl.ds(last, 1), :] * hf
        cb_ = n_chunks - 1 - c
        hin_b[pl.ds(cb_, 1), :] = hb
        first = cb_ * pitch
        hb = b_b[pl.ds(first, 1), :] + a_b[pl.ds(first, 1), :] * hb
        return hf, hb

    hf_end, hb_end = lax.fori_loop(0, n_chunks, carry_step, (h0[0:1], h0[1:2]))
    st_ref[0:1, :] = hf_end
    st_ref[1:2, :] = hb_end

    def out_chunk(c, carry):
        src = pl.ds(pl.multiple_of(c * pitch, SUBLANES), chunk)
        hf = b_f[src, :] + a_f[src, :] * hin_f[pl.ds(c, 1), :]
        hb = b_b[src, :] + a_b[src, :] * hin_b[pl.ds(c, 1), :]
        dst = pl.ds(pl.multiple_of(c * chunk, SUBLANES), chunk)
        o_ref[dst, :] = ((hf + hb) * _silu(g_ref[dst, :])).astype(BF16)
        return carry

    lax.fori_loop(0, n_chunks, out_chunk, 0)


def _lru(x_lru, g_lru, conv_w, conv_b, w_cat, b_cat, lru_lam, h0):
    b, t_len, _ = x_lru.shape
    chunk = 64 if t_len >= 2048 else 32
    pitch = chunk + SUBLANES
    n_chunks = t_len // chunk
    rows = min(512, t_len)
    seq_spec = pl.BlockSpec((None, t_len, LRU_BLOCK), lambda bi, n: (bi, 0, n))
    st_spec = pl.BlockSpec((None, 2, LRU_BLOCK), lambda bi, n: (bi, 0, n))
    scan_buf = pltpu.VMEM((n_chunks * pitch, LRU_BLOCK), F32)
    return pl.pallas_call(
        functools.partial(_lru_kernel, t_len=t_len, chunk=chunk, pitch=pitch, rows=rows),
        grid=(b, N_LRU_BLOCKS),
        in_specs=[
            seq_spec, seq_spec,
            pl.BlockSpec((4, LRU_BLOCK), lambda bi, n: (0, n)),
            pl.BlockSpec((1, LRU_BLOCK), lambda bi, n: (0, n)),
            pl.BlockSpec((None, LRU_BLOCK, 4 * LRU_BLOCK), lambda bi, n: (n, 0, 0)),
            pl.BlockSpec((None, 1, 4 * LRU_BLOCK), lambda bi, n: (n, 0, 0)),
            pl.BlockSpec((2, LRU_BLOCK), lambda bi, n: (0, n)),
            st_spec,
        ],
        out_specs=[seq_spec, st_spec],
        out_shape=[jax.ShapeDtypeStruct((b, t_len, D_LRU), BF16),
                   jax.ShapeDtypeStruct((b, 2, D_LRU), F32)],
        scratch_shapes=[
            pltpu.VMEM((t_len + 2 * SUBLANES, LRU_BLOCK), F32),
            scan_buf, scan_buf, scan_buf, scan_buf,
            pltpu.VMEM((n_chunks, LRU_BLOCK), F32),
            pltpu.VMEM((n_chunks, LRU_BLOCK), F32),
        ],
        compiler_params=pltpu.CompilerParams(
            dimension_semantics=("arbitrary",) * 2, vmem_limit_bytes=V7X_VMEM_LIMIT_BYTES),
        name="rglru",
    )(x_lru, g_lru, conv_w, conv_b, w_cat, b_cat, lru_lam, h0)


def _outproj_kernel(att_ref, lru_ref, x_ref, gate_ref, wa_ref, wl_ref, gpost_ref, y_ref):
    o = _dot(att_ref[...], wa_ref[...]) + _dot(lru_ref[...], wl_ref[...])
    ms = jnp.mean(o * o, axis=-1, keepdims=True)
    n = o * lax.rsqrt(ms + EPS) * gpost_ref[...]
    y_ref[...] = x_ref[...] + gate_ref[...] * n


def _outproj(att, lru, x, gate, w_att, w_lru, g_post, *, t_len, tm=512):
    n_tok = x.shape[0]
    tm = min(tm, t_len)
    tiles_per_batch = t_len // tm
    nb = gate.shape[0]
    if nb == 1:
        mod_map = lambda i: (0, 0, 0)
    else:
        mod_map = lambda i: (i // tiles_per_batch, 0, 0)
    tok_spec = pl.BlockSpec((tm, D_MODEL), lambda i: (i, 0))
    w_spec = pl.BlockSpec((D_MODEL, D_MODEL), lambda i: (0, 0))
    return pl.pallas_call(
        _outproj_kernel,
        grid=(n_tok // tm,),
        in_specs=[tok_spec, tok_spec, tok_spec,
                  pl.BlockSpec((None, 1, D_MODEL), mod_map),
                  w_spec, w_spec,
                  pl.BlockSpec((1, D_MODEL), lambda i: (0, 0))],
        out_specs=tok_spec,
        out_shape=jax.ShapeDtypeStruct((n_tok, D_MODEL), F32),
        compiler_params=pltpu.CompilerParams(
            dimension_semantics=("arbitrary",), vmem_limit_bytes=V7X_VMEM_LIMIT_BYTES),
        name="outproj",
    )(att, lru, x, gate, w_att, w_lru, g_post)


def _sublayer(x, scale, shift, gate, cache_k, cache_v, h0, use_rope, w):
    b, t_len, _ = x.shape
    xf = x.reshape(b * t_len, D_MODEL)
    emit_kv = cache_k is None
    outs = _inproj(xf, scale, shift, w["g_pre"], w["w_in"], t_len=t_len,
                   use_rope=use_rope, emit_kv_f32=emit_kv)
    q, k, v, g_att, x_lru, g_lru = [o.reshape(b, t_len, D_MODEL) for o in outs[:6]]
    att = _attention(w["lam"], w["g_subln"], q, k, v, g_att, cache_k, cache_v)
    lru, state = _lru(x_lru, g_lru, w["conv_w"], w["conv_b"], w["w_cat"], w["b_cat"], w["lru_lam"], h0)
    y = _outproj(att.reshape(b * t_len, D_ATT), lru.reshape(b * t_len, D_LRU), xf, gate,
                 w["w_out_att"], w["w_out_lru"], w["g_post"], t_len=t_len)
    y = y.reshape(b, t_len, D_MODEL)
    if emit_kv:
        return y, outs[6], outs[7], state
    return y, None, None, state


def kernel(x_prompt, x_sample, cache_k, cache_v, state_lru, c, c_ctx, w_ada, b_ada, g_pre, w_in, lambda_q1, lambda_k1, lambda_q2, lambda_k2, g_subln, conv_w, conv_b, w_rgate, b_rgate, w_igate, b_igate, lru_lambda, w_out, g_post):
    bp, seq, _ = x_prompt.shape
    bd, kc = cache_k.shape[0], cache_k.shape[2]
    l = 0

    cond = jnp.concatenate([c, c_ctx[None, :], jnp.zeros((SUBLANES - bd - 1, D_MODEL), F32)], axis=0)
    mod = _modulation(cond, w_ada[l], b_ada[l][None, :])
    shift, scale, gate = [mod[:, i * D_MODEL:(i + 1) * D_MODEL] for i in range(3)]

    def rows(a, lo, hi):
        return a[lo:hi][:, None, :]

    w = {
        "g_pre": g_pre[l][None, :],
        "w_in": w_in[l].astype(BF16),
        "lam": jnp.stack([lambda_q1[l], lambda_k1[l], lambda_q2[l], lambda_k2[l]], axis=0),
        "g_subln": g_subln[l][None, :],
        "conv_w": conv_w[l],
        "conv_b": conv_b[l][None, :],
        "w_cat": jnp.concatenate([w_rgate[l, 0], w_igate[l, 0], w_rgate[l, 1], w_igate[l, 1]],
                                 axis=-1).astype(BF16),
        "b_cat": jnp.concatenate(
            [bb.reshape(N_LRU_BLOCKS, 1, LRU_BLOCK)
             for bb in (b_rgate[l, 0], b_igate[l, 0], b_rgate[l, 1], b_igate[l, 1])], axis=-1),
        "lru_lam": lru_lambda[l],
        "w_out_att": w_out[l, :D_ATT].astype(BF16),
        "w_out_lru": w_out[l, D_ATT:].astype(BF16),
        "g_post": g_post[l][None, :],
    }

    y_p, new_k, new_v, st_p = _sublayer(
        x_prompt, rows(scale, bd, bd + 1), rows(shift, bd, bd + 1), rows(gate, bd, bd + 1),
        None, None, jnp.zeros((bp, 2, D_LRU), F32), False, w)

    y_s, _, _, _ = _sublayer(
        x_sample, rows(scale, 0, bd), rows(shift, 0, bd), rows(gate, 0, bd),
        cache_k[:, l].reshape(bd, kc, D_ATT), cache_v[:, l].reshape(bd, kc, D_ATT),
        state_lru[:, l], True, w)

    new_k = new_k.reshape(bp, 1, seq, N_HEADS, HEAD_W)
    new_v = new_v.reshape(bp, 1, seq, N_HEADS, HEAD_W)
    return (y_p, y_s, new_k, new_v, st_p[:, None])
```

```python
import functools
import math

import jax
import jax.numpy as jnp
import numpy as np
from jax import lax
from jax.experimental import pallas as pl
from jax.experimental.pallas import tpu as pltpu

F32 = jnp.float32
BF16 = jnp.bfloat16

D_MODEL = 1024
GRID_W = 64
N_HEADS = 8
DIFF_HEAD_DIM = 64
HEAD_W = 2 * DIFF_HEAD_DIM
D_ATT = N_HEADS * HEAD_W
D_LRU = 1024
N_LRU_BLOCKS = 8
LRU_BLOCK = D_LRU // N_LRU_BLOCKS
LRU_C = 8.0
N_GROUPS = 6
ROPE_BASE = 10000.0
EPS = 1e-6
LAM_INIT = 0.8 - 0.6 * math.exp(-0.3 * 0)

V7X_VMEM_LIMIT_BYTES = 56 * 1024 * 1024
SUBLANES = 8
ROPE_SWAP = DIFF_HEAD_DIM // 4
BF16_ROWS_PER_VREG = 16
ACC_ROWS = HEAD_W + BF16_ROWS_PER_VREG


def _silu(x):
    return x * jax.nn.sigmoid(x)


def _dot(a, b):
    return jnp.dot(a, b, preferred_element_type=F32)


def _dot_nt(a, b):
    return lax.dot_general(a, b, (((1,), (1,)), ((), ())), preferred_element_type=F32)


def _mod_kernel(cond_ref, w_ref, b_ref, o_ref):
    s = _silu(cond_ref[...])
    o_ref[...] = _dot(s.astype(BF16), w_ref[...].astype(BF16)) + b_ref[...]


def _modulation(cond, w_ada, b_ada):
    n = cond.shape[0]
    return pl.pallas_call(
        _mod_kernel,
        grid=(3,),
        in_specs=[
            pl.BlockSpec((n, D_MODEL), lambda j: (0, 0)),
            pl.BlockSpec((D_MODEL, D_MODEL), lambda j: (0, j)),
            pl.BlockSpec((1, D_MODEL), lambda j: (0, j)),
        ],
        out_specs=pl.BlockSpec((n, D_MODEL), lambda j: (0, j)),
        out_shape=jax.ShapeDtypeStruct((n, 3 * D_MODEL), F32),
        name="modulation",
    )(cond, w_ada, b_ada)


def _rope_tables(t_len):
    half = DIFF_HEAD_DIM // 2
    nf = half // 2
    t = np.arange(t_len)
    row = (t // GRID_W).astype(np.float32)
    col = (t % GRID_W).astype(np.float32)
    inv = (ROPE_BASE ** (-np.arange(nf, dtype=np.float32) * 2.0 / half)).astype(np.float32)
    lane = np.arange(HEAD_W) % DIFF_HEAD_DIM
    use_row = lane < half
    freq = (lane % half) % nf
    first = (lane % half) < nf
    pos = np.where(use_row[None, :], row[:, None], col[:, None]).astype(np.float32)
    ang = (pos * inv[freq][None, :]).astype(np.float32).astype(np.float64)
    cos = np.cos(ang).astype(np.float32)
    sin = (np.sin(ang) * np.where(first, -1.0, 1.0)[None, :]).astype(np.float32)
    return jnp.asarray(cos), jnp.asarray(sin)


def _inproj_kernel(*refs, use_rope, emit_kv_f32):
    x_ref, scale_ref, shift_ref, gpre_ref, w_ref = refs[:5]
    refs = refs[5:]
    if use_rope:
        cos_ref, sin_ref = refs[:2]
        refs = refs[2:]
    q_ref, k_ref, v_ref, gatt_ref, xlru_ref, glru_ref = refs[:6]
    refs = refs[6:]
    if emit_kv_f32:
        kf_ref, vf_ref = refs

    x = x_ref[...]
    ms = jnp.mean(x * x, axis=-1, keepdims=True)
    y = x * lax.rsqrt(ms + EPS) * gpre_ref[...]
    h = (y * (1.0 + scale_ref[...]) + shift_ref[...]).astype(BF16)

    def proj(g):
        return _dot(h, w_ref[:, g * D_MODEL:(g + 1) * D_MODEL])

    if use_rope:
        cos = cos_ref[...]
        sin = sin_ref[...]
        lane = lax.broadcasted_iota(jnp.int32, cos.shape, 1)
        take_next = (lane % (2 * ROPE_SWAP)) < ROPE_SWAP

        def rope(p):
            outs = []
            for hd in range(N_HEADS):
                xh = p[:, hd * HEAD_W:(hd + 1) * HEAD_W]
                partner = jnp.where(take_next,
                                    pltpu.roll(xh, HEAD_W - ROPE_SWAP, 1),
                                    pltpu.roll(xh, ROPE_SWAP, 1))
                outs.append(xh * cos + partner * sin)
            return jnp.concatenate(outs, axis=-1)
    else:
        def rope(p):
            return p

    q = rope(proj(0)) * (1.0 / math.sqrt(DIFF_HEAD_DIM))
    q_ref[...] = q.T.astype(BF16)
    k = rope(proj(1))
    k_ref[...] = k.astype(BF16)
    v = proj(2)
    vt = v.T
    for hd in range(N_HEADS):
        v_ref[hd] = vt[hd * HEAD_W:(hd + 1) * HEAD_W].astype(BF16)
    if emit_kv_f32:
        kf_ref[...] = k
        vf_ref[...] = v
    gatt_ref[...] = proj(3)
    xlru_ref[...] = proj(4)
    glru_ref[...] = proj(5)


def _inproj(x, scale, shift, g_pre, w_in_bf16, *, t_len, use_rope, emit_kv_f32, tm=256):
    n_tok = x.shape[0]
    tiles_per_batch = t_len // tm
    nb = scale.shape[0]
    if nb == 1:
        mod_map = lambda i: (0, 0, 0)
    else:
        mod_map = lambda i: (i // tiles_per_batch, 0, 0)
    tok_spec = pl.BlockSpec((tm, D_MODEL), lambda i: (i, 0))
    in_specs = [
        tok_spec,
        pl.BlockSpec((None, 1, D_MODEL), mod_map),
        pl.BlockSpec((None, 1, D_MODEL), mod_map),
        pl.BlockSpec((1, D_MODEL), lambda i: (0, 0)),
        pl.BlockSpec((D_MODEL, N_GROUPS * D_MODEL), lambda i: (0, 0)),
    ]
    args = [x, scale, shift, g_pre, w_in_bf16]
    if use_rope:
        cos, sin = _rope_tables(t_len)
        rope_spec = pl.BlockSpec((tm, HEAD_W), lambda i: (i % tiles_per_batch, 0))
        in_specs += [rope_spec, rope_spec]
        args += [cos, sin]
    n_batch = n_tok // t_len
    kc = _key_chunk(t_len)
    tiles_per_chunk = kc // tm
    out_specs = [
        pl.BlockSpec((None, D_MODEL, tm), lambda i: (i // tiles_per_batch, 0, i % tiles_per_batch)),
        tok_spec,
        pl.BlockSpec((None, N_HEADS, None, HEAD_W, tm),
                     lambda i: (i // tiles_per_batch, 0, (i % tiles_per_batch) // tiles_per_chunk, 0,
                                i % tiles_per_chunk)),
        tok_spec, tok_spec, tok_spec,
    ]
    out_shape = [
        jax.ShapeDtypeStruct((n_batch, D_MODEL, t_len), BF16),
        jax.ShapeDtypeStruct((n_tok, D_MODEL), BF16),
        jax.ShapeDtypeStruct((n_batch, N_HEADS, t_len // kc, HEAD_W, kc), BF16),
    ] + [jax.ShapeDtypeStruct((n_tok, D_MODEL), F32)] * 3
    if emit_kv_f32:
        out_specs += [tok_spec] * 2
        out_shape += [jax.ShapeDtypeStruct((n_tok, D_MODEL), F32)] * 2
    return pl.pallas_call(
        functools.partial(_inproj_kernel, use_rope=use_rope, emit_kv_f32=emit_kv_f32),
        grid=(n_tok // tm,),
        in_specs=in_specs,
        out_specs=out_specs,
        out_shape=out_shape,
        compiler_params=pltpu.CompilerParams(
            dimension_semantics=("arbitrary",), vmem_limit_bytes=V7X_VMEM_LIMIT_BYTES),
        name="inproj_rope" if use_rope else "inproj",
    )(*args)


def _key_chunk(t_len):
    return min(512, t_len)


def _attn_kernel(*refs, has_cache, n_chunks, kc):
    lam_ref, gsub_ref, qt_ref, k_ref, vt_ref, gatt_ref = refs[:6]
    refs = refs[6:]
    if has_cache:
        kc_ref, vc_ref = refs[:2]
        refs = refs[2:]
    o_ref, acc_ref, s_ref = refs

    lp = lam_ref[...]
    lam = (jnp.exp(jnp.sum(lp[0:1] * lp[1:2], axis=-1, keepdims=True))
           - jnp.exp(jnp.sum(lp[2:3] * lp[3:4], axis=-1, keepdims=True)) + LAM_INIT)

    qt = qt_ref[...]
    tq = qt.shape[1]
    row = lax.broadcasted_iota(jnp.int32, qt.shape, 0)
    zero = jnp.zeros_like(qt)
    q_maps = (jnp.where(row < DIFF_HEAD_DIM, qt, zero), jnp.where(row >= DIFF_HEAD_DIM, qt, zero))

    acc_ref[...] = jnp.zeros_like(acc_ref)

    def scores(k_chunk, slot):
        maxes = []
        for idx in range(2):
            st = _dot(k_chunk, q_maps[idx])
            s_ref[slot, idx, :k_chunk.shape[0], :] = st
            maxes.append(jnp.max(st, axis=0, keepdims=True))
        return tuple(maxes)

    def accumulate(vt_chunk, slot, ms, maxes):
        n_keys = vt_chunk.shape[1]
        ones = jnp.ones((ACC_ROWS - HEAD_W, n_keys), BF16)
        vt_ext = jnp.concatenate([vt_chunk, ones], axis=0)
        new_ms = []
        for idx in range(2):
            m_new = jnp.maximum(ms[idx], maxes[idx])
            alpha = jnp.exp(ms[idx] - m_new)
            et = jnp.exp(s_ref[slot, idx, :n_keys, :] - m_new).astype(BF16)
            acc_ref[idx] = acc_ref[idx] * alpha + _dot(vt_ext, et)
            new_ms.append(m_new)
        return tuple(new_ms)

    def new_keys(j):
        return k_ref[pl.ds(pl.multiple_of(j * kc, kc), kc), :]

    m0 = jnp.full((1, tq), -jnp.inf, F32)
    ms = (m0, m0)
    if has_cache:
        cache_maxes = scores(kc_ref[...].astype(BF16), 2)
        maxes = scores(new_keys(0), 0)
        ms = accumulate(vc_ref[...].T.astype(BF16), 2, ms, cache_maxes)
    else:
        maxes = scores(new_keys(0), 0)

    def pair(t, carry):
        ms, maxes = carry
        j = 2 * t
        odd_maxes = scores(new_keys(j + 1), 1)
        ms = accumulate(vt_ref[j], 0, ms, maxes)
        maxes = scores(new_keys(j + 2), 0)
        ms = accumulate(vt_ref[j + 1], 1, ms, odd_maxes)
        return ms, maxes

    n_pairs = (n_chunks - 1) // 2
    ms, maxes = lax.fori_loop(0, n_pairs, pair, (ms, maxes))
    if n_chunks % 2 == 0:
        last_maxes = scores(new_keys(n_chunks - 1), 1)
        ms = accumulate(vt_ref[n_chunks - 2], 0, ms, maxes)
        accumulate(vt_ref[n_chunks - 1], 1, ms, last_maxes)
    else:
        accumulate(vt_ref[n_chunks - 1], 0, ms, maxes)

    acc1 = acc_ref[0]
    acc2 = acc_ref[1]
    o1 = acc1[:HEAD_W] * (1.0 / acc1[HEAD_W:HEAD_W + 1])
    o2 = acc2[:HEAD_W] * (1.0 / acc2[HEAD_W:HEAD_W + 1])
    o = (o1 - lam * o2).T

    ms = jnp.mean(o * o, axis=-1, keepdims=True)
    on = o * lax.rsqrt(ms + EPS) * gsub_ref[...] * (1.0 - LAM_INIT)
    o_ref[...] = (on * _silu(gatt_ref[...])).astype(BF16)


def _attention(lam_params, g_subln, qt, k, vt, g_att, cache_k=None, cache_v=None, *, tq=512):
    b, t_len, _ = k.shape
    has_cache = cache_k is not None
    n_chunks, kc = vt.shape[2], vt.shape[4]
    tq = min(tq, t_len)
    s_rows = max(kc, cache_k.shape[1]) if has_cache else kc
    tok_spec = pl.BlockSpec((None, tq, HEAD_W), lambda bi, h, qi: (bi, qi, h))
    in_specs = [
        pl.BlockSpec(lam_params.shape, lambda bi, h, qi: (0, 0)),
        pl.BlockSpec((1, HEAD_W), lambda bi, h, qi: (0, 0)),
        pl.BlockSpec((None, HEAD_W, tq), lambda bi, h, qi: (bi, h, qi)),
        pl.BlockSpec((None, t_len, HEAD_W), lambda bi, h, qi: (bi, 0, h)),
        pl.BlockSpec((None, None, n_chunks, HEAD_W, kc), lambda bi, h, qi: (bi, h, 0, 0, 0)),
        tok_spec,
    ]
    args = [lam_params, g_subln, qt, k, vt, g_att]
    if has_cache:
        c_spec = pl.BlockSpec((None, cache_k.shape[1], HEAD_W), lambda bi, h, qi: (bi, 0, h))
        in_specs += [c_spec, c_spec]
        args += [cache_k, cache_v]
    return pl.pallas_call(
        functools.partial(_attn_kernel, has_cache=has_cache, n_chunks=n_chunks, kc=kc),
        grid=(b, N_HEADS, t_len // tq),
        in_specs=in_specs,
        out_specs=tok_spec,
        out_shape=jax.ShapeDtypeStruct((b, t_len, D_ATT), BF16),
        scratch_shapes=[pltpu.VMEM((2, ACC_ROWS, tq), F32),
                        pltpu.VMEM((3 if has_cache else 2, 2, s_rows, tq), F32)],
        compiler_params=pltpu.CompilerParams(
            dimension_semantics=("arbitrary",) * 3, vmem_limit_bytes=V7X_VMEM_LIMIT_BYTES),
        name="diff_attn_cache" if has_cache else "diff_attn",
    )(*args)


def _lru_kernel(x_ref, g_ref, cw_ref, cb_ref, w_ref, b_ref, lam_ref, h0_ref,
                o_ref, st_ref,
                xp, a_f, b_f, a_b, b_b, hin_f, hin_b,
                *, t_len, chunk, pitch, rows):
    n_chunks = t_len // chunk
    chunks_per_tile = rows // chunk

    pad = jnp.zeros((SUBLANES, LRU_BLOCK), F32)
    xp[pl.ds(0, SUBLANES), :] = pad
    xp[pl.ds(SUBLANES, t_len), :] = x_ref[...]
    xp[pl.ds(SUBLANES + t_len, SUBLANES), :] = pad

    lam = lam_ref[...]
    sp = jnp.maximum(-lam, 0.0) + jnp.log1p(jnp.exp(-jnp.abs(lam)))
    cw = cw_ref[...]
    cb = cb_ref[...]
    wcat = w_ref[...]
    bcat = b_ref[...]

    def gate_tile(r, carry):
        t0 = pl.multiple_of(r * rows, SUBLANES)
        u = cb
        for j in range(4):
            u = u + xp[pl.ds(t0 + SUBLANES - 1 + j, rows), :] * cw[j:j + 1]
        z = _dot(u.astype(BF16), wcat) + bcat
        for d, (a_s, b_s) in enumerate(((a_f, b_f), (a_b, b_b))):
            rg = jax.nn.sigmoid(z[:, (2 * d) * LRU_BLOCK:(2 * d + 1) * LRU_BLOCK])
            ig = jax.nn.sigmoid(z[:, (2 * d + 1) * LRU_BLOCK:(2 * d + 2) * LRU_BLOCK])
            log_a = -LRU_C * rg * sp[d:d + 1]
            a = jnp.exp(log_a)
            bb = jnp.sqrt(jnp.tanh(-log_a) * (1.0 + a * a)) * (ig * u)
            for cc in range(chunks_per_tile):
                dst = pl.multiple_of((r * chunks_per_tile + cc) * pitch, SUBLANES)
                a_s[pl.ds(dst, chunk), :] = a[cc * chunk:(cc + 1) * chunk]
                b_s[pl.ds(dst, chunk), :] = bb[cc * chunk:(cc + 1) * chunk]
        return carry

    lax.fori_loop(0, t_len // rows, gate_tile, 0)

    def local_step(l, carry):
        hf, pf, hb, pb = carry
        rf = pl.ds(l, n_chunks, stride=pitch)
        a = a_f[rf, :]
        hf = a * hf + b_f[rf, :]
        pf = a * pf
        b_f[rf, :] = hf
        a_f[rf, :] = pf
        rb = pl.ds(chunk - 1 - l, n_chunks, stride=pitch)
        a = a_b[rb, :]
        hb = a * hb + b_b[rb, :]
        pb = a * pb
        b_b[rb, :] = hb
        a_b[rb, :] = pb
        return hf, pf, hb, pb

    z0 = jnp.zeros((n_chunks, LRU_BLOCK), F32)
    o0 = jnp.ones((n_chunks, LRU_BLOCK), F32)
    lax.fori_loop(0, chunk, local_step, (z0, o0, z0, o0))

    h0 = h0_ref[...]

    def carry_step(c, carry):
        hf, hb = carry
        hin_f[pl.ds(c, 1), :] = hf
        last = c * pitch + chunk - 1
        hf = b_f[pl.ds(last, 1), :] + a_f[pl.ds(last, 1), :] * hf
        cb_ = n_chunks - 1 - c
        hin_b[pl.ds(cb_, 1), :] = hb
        first = cb_ * pitch
        hb = b_b[pl.ds(first, 1), :] + a_b[pl.ds(first, 1), :] * hb
        return hf, hb

    hf_end, hb_end = lax.fori_loop(0, n_chunks, carry_step, (h0[0:1], h0[1:2]))
    st_ref[0:1, :] = hf_end
    st_ref[1:2, :] = hb_end

    def out_chunk(c, carry):
        src = pl.ds(pl.multiple_of(c * pitch, SUBLANES), chunk)
        hf = b_f[src, :] + a_f[src, :] * hin_f[pl.ds(c, 1), :]
        hb = b_b[src, :] + a_b[src, :] * hin_b[pl.ds(c, 1), :]
        dst = pl.ds(pl.multiple_of(c * chunk, SUBLANES), chunk)
        o_ref[dst, :] = ((hf + hb) * _silu(g_ref[dst, :])).astype(BF16)
        return carry

    lax.fori_loop(0, n_chunks, out_chunk, 0)


def _lru(x_lru, g_lru, conv_w, conv_b, w_cat, b_cat, lru_lam, h0):
    b, t_len, _ = x_lru.shape
    chunk = 64 if t_len >= 2048 else 32
    pitch = chunk + SUBLANES
    n_chunks = t_len // chunk
    rows = min(512, t_len)
    seq_spec = pl.BlockSpec((None, t_len, LRU_BLOCK), lambda bi, n: (bi, 0, n))
    st_spec = pl.BlockSpec((None, 2, LRU_BLOCK), lambda bi, n: (bi, 0, n))
    scan_buf = pltpu.VMEM((n_chunks * pitch, LRU_BLOCK), F32)
    return pl.pallas_call(
        functools.partial(_lru_kernel, t_len=t_len, chunk=chunk, pitch=pitch, rows=rows),
        grid=(b, N_LRU_BLOCKS),
        in_specs=[
            seq_spec, seq_spec,
            pl.BlockSpec((4, LRU_BLOCK), lambda bi, n: (0, n)),
            pl.BlockSpec((1, LRU_BLOCK), lambda bi, n: (0, n)),
            pl.BlockSpec((None, LRU_BLOCK, 4 * LRU_BLOCK), lambda bi, n: (n, 0, 0)),
            pl.BlockSpec((None, 1, 4 * LRU_BLOCK), lambda bi, n: (n, 0, 0)),
            pl.BlockSpec((2, LRU_BLOCK), lambda bi, n: (0, n)),
            st_spec,
        ],
        out_specs=[seq_spec, st_spec],
        out_shape=[jax.ShapeDtypeStruct((b, t_len, D_LRU), BF16),
                   jax.ShapeDtypeStruct((b, 2, D_LRU), F32)],
        scratch_shapes=[
            pltpu.VMEM((t_len + 2 * SUBLANES, LRU_BLOCK), F32),
            scan_buf, scan_buf, scan_buf, scan_buf,
            pltpu.VMEM((n_chunks, LRU_BLOCK), F32),
            pltpu.VMEM((n_chunks, LRU_BLOCK), F32),
        ],
        compiler_params=pltpu.CompilerParams(
            dimension_semantics=("arbitrary",) * 2, vmem_limit_bytes=V7X_VMEM_LIMIT_BYTES),
        name="rglru",
    )(x_lru, g_lru, conv_w, conv_b, w_cat, b_cat, lru_lam, h0)


def _outproj_kernel(att_ref, lru_ref, x_ref, gate_ref, wa_ref, wl_ref, gpost_ref, y_ref):
    o = _dot(att_ref[...], wa_ref[...]) + _dot(lru_ref[...], wl_ref[...])
    ms = jnp.mean(o * o, axis=-1, keepdims=True)
    n = o * lax.rsqrt(ms + EPS) * gpost_ref[...]
    y_ref[...] = x_ref[...] + gate_ref[...] * n


def _outproj(att, lru, x, gate, w_att, w_lru, g_post, *, t_len, tm=512):
    n_tok = x.shape[0]
    tm = min(tm, t_len)
    tiles_per_batch = t_len // tm
    nb = gate.shape[0]
    if nb == 1:
        mod_map = lambda i: (0, 0, 0)
    else:
        mod_map = lambda i: (i // tiles_per_batch, 0, 0)
    tok_spec = pl.BlockSpec((tm, D_MODEL), lambda i: (i, 0))
    w_spec = pl.BlockSpec((D_MODEL, D_MODEL), lambda i: (0, 0))
    return pl.pallas_call(
        _outproj_kernel,
        grid=(n_tok // tm,),
        in_specs=[tok_spec, tok_spec, tok_spec,
                  pl.BlockSpec((None, 1, D_MODEL), mod_map),
                  w_spec, w_spec,
                  pl.BlockSpec((1, D_MODEL), lambda i: (0, 0))],
        out_specs=tok_spec,
        out_shape=jax.ShapeDtypeStruct((n_tok, D_MODEL), F32),
        compiler_params=pltpu.CompilerParams(
            dimension_semantics=("arbitrary",), vmem_limit_bytes=V7X_VMEM_LIMIT_BYTES),
        name="outproj",
    )(att, lru, x, gate, w_att, w_lru, g_post)


def _sublayer(x, scale, shift, gate, cache_k, cache_v, h0, use_rope, w):
    b, t_len, _ = x.shape
    xf = x.reshape(b * t_len, D_MODEL)
    emit_kv = cache_k is None
    outs = _inproj(xf, scale, shift, w["g_pre"], w["w_in"], t_len=t_len,
                   use_rope=use_rope, emit_kv_f32=emit_kv)
    qt, vt = outs[0], outs[2]
    k, g_att, x_lru, g_lru = [outs[i].reshape(b, t_len, D_MODEL) for i in (1, 3, 4, 5)]
    att = _attention(w["lam"], w["g_subln"], qt, k, vt, g_att, cache_k, cache_v)
    lru, state = _lru(x_lru, g_lru, w["conv_w"], w["conv_b"], w["w_cat"], w["b_cat"], w["lru_lam"], h0)
    y = _outproj(att.reshape(b * t_len, D_ATT), lru.reshape(b * t_len, D_LRU), xf, gate,
                 w["w_out_att"], w["w_out_lru"], w["g_post"], t_len=t_len)
    y = y.reshape(b, t_len, D_MODEL)
    if emit_kv:
        return y, outs[6], outs[7], state
    return y, None, None, state


def kernel(x_prompt, x_sample, cache_k, cache_v, state_lru, c, c_ctx, w_ada, b_ada, g_pre, w_in, lambda_q1, lambda_k1, lambda_q2, lambda_k2, g_subln, conv_w, conv_b, w_rgate, b_rgate, w_igate, b_igate, lru_lambda, w_out, g_post):
    bp, seq, _ = x_prompt.shape
    bd, kc = cache_k.shape[0], cache_k.shape[2]
    l = 0

    cond = jnp.concatenate([c, c_ctx[None, :], jnp.zeros((SUBLANES - bd - 1, D_MODEL), F32)], axis=0)
    mod = _modulation(cond, w_ada[l], b_ada[l][None, :])
    shift, scale, gate = [mod[:, i * D_MODEL:(i + 1) * D_MODEL] for i in range(3)]

    def rows(a, lo, hi):
        return a[lo:hi][:, None, :]

    w = {
        "g_pre": g_pre[l][None, :],
        "w_in": w_in[l].astype(BF16),
        "lam": jnp.stack([lambda_q1[l], lambda_k1[l], lambda_q2[l], lambda_k2[l]], axis=0),
        "g_subln": g_subln[l][None, :],
        "conv_w": conv_w[l],
        "conv_b": conv_b[l][None, :],
        "w_cat": jnp.concatenate([w_rgate[l, 0], w_igate[l, 0], w_rgate[l, 1], w_igate[l, 1]],
                                 axis=-1).astype(BF16),
        "b_cat": jnp.concatenate(
            [bb.reshape(N_LRU_BLOCKS, 1, LRU_BLOCK)
             for bb in (b_rgate[l, 0], b_igate[l, 0], b_rgate[l, 1], b_igate[l, 1])], axis=-1),
        "lru_lam": lru_lambda[l],
        "w_out_att": w_out[l, :D_ATT].astype(BF16),
        "w_out_lru": w_out[l, D_ATT:].astype(BF16),
        "g_post": g_post[l][None, :],
    }

    y_p, new_k, new_v, st_p = _sublayer(
        x_prompt, rows(scale, bd, bd + 1), rows(shift, bd, bd + 1), rows(gate, bd, bd + 1),
        None, None, jnp.zeros((bp, 2, D_LRU), F32), False, w)

    y_s, _, _, _ = _sublayer(
        x_sample, rows(scale, 0, bd), rows(shift, 0, bd), rows(gate, 0, bd),
        cache_k[:, l].reshape(bd, kc, D_ATT), cache_v[:, l].reshape(bd, kc, D_ATT),
        state_lru[:, l], True, w)

    new_k = new_k.reshape(bp, 1, seq, N_HEADS, HEAD_W)
    new_v = new_v.reshape(bp, 1, seq, N_HEADS, HEAD_W)
    return (y_p, y_s, new_k, new_v, st_p[:, None])
```

```python
import functools
import math

import jax
import jax.numpy as jnp
import numpy as np
from jax import lax
from jax.experimental import pallas as pl
from jax.experimental.pallas import tpu as pltpu

F32 = jnp.float32
BF16 = jnp.bfloat16

D_MODEL = 1024
GRID_W = 64
N_HEADS = 8
DIFF_HEAD_DIM = 64
HEAD_W = 2 * DIFF_HEAD_DIM
D_ATT = N_HEADS * HEAD_W
D_LRU = 1024
N_LRU_BLOCKS = 8
LRU_BLOCK = D_LRU // N_LRU_BLOCKS
LRU_C = 8.0
N_GROUPS = 6
ROPE_BASE = 10000.0
EPS = 1e-6
LAM_INIT = 0.8 - 0.6 * math.exp(-0.3 * 0)
LOG2_E = math.log2(math.e)

V7X_VMEM_LIMIT_BYTES = 56 * 1024 * 1024
SUBLANES = 8
ROPE_SWAP = DIFF_HEAD_DIM // 4
BF16_ROWS_PER_VREG = 16
ACC_ROWS = HEAD_W + BF16_ROWS_PER_VREG


def _silu(x):
    return x * jax.nn.sigmoid(x)


def _dot(a, b):
    return jnp.dot(a, b, preferred_element_type=F32)


def _dot_nt(a, b):
    return lax.dot_general(a, b, (((1,), (1,)), ((), ())), preferred_element_type=F32)


def _mod_kernel(cond_ref, w_ref, b_ref, o_ref):
    s = _silu(cond_ref[...])
    o_ref[...] = _dot(s.astype(BF16), w_ref[...].astype(BF16)) + b_ref[...]


def _modulation(cond, w_ada, b_ada):
    n = cond.shape[0]
    return pl.pallas_call(
        _mod_kernel,
        grid=(3,),
        in_specs=[
            pl.BlockSpec((n, D_MODEL), lambda j: (0, 0)),
            pl.BlockSpec((D_MODEL, D_MODEL), lambda j: (0, j)),
            pl.BlockSpec((1, D_MODEL), lambda j: (0, j)),
        ],
        out_specs=pl.BlockSpec((n, D_MODEL), lambda j: (0, j)),
        out_shape=jax.ShapeDtypeStruct((n, 3 * D_MODEL), F32),
        name="modulation",
    )(cond, w_ada, b_ada)


def _rope_tables(t_len):
    half = DIFF_HEAD_DIM // 2
    nf = half // 2
    t = np.arange(t_len)
    row = (t // GRID_W).astype(np.float32)
    col = (t % GRID_W).astype(np.float32)
    inv = (ROPE_BASE ** (-np.arange(nf, dtype=np.float32) * 2.0 / half)).astype(np.float32)
    lane = np.arange(HEAD_W) % DIFF_HEAD_DIM
    use_row = lane < half
    freq = (lane % half) % nf
    first = (lane % half) < nf
    pos = np.where(use_row[None, :], row[:, None], col[:, None]).astype(np.float32)
    ang = (pos * inv[freq][None, :]).astype(np.float32).astype(np.float64)
    cos = np.cos(ang).astype(np.float32)
    sin = (np.sin(ang) * np.where(first, -1.0, 1.0)[None, :]).astype(np.float32)
    return jnp.asarray(cos), jnp.asarray(sin)


def _inproj_kernel(*refs, use_rope, emit_kv_f32):
    x_ref, scale_ref, shift_ref, gpre_ref, w_ref = refs[:5]
    refs = refs[5:]
    if use_rope:
        cos_ref, sin_ref = refs[:2]
        refs = refs[2:]
    q_ref, k_ref, v_ref, gatt_ref, xlru_ref, glru_ref = refs[:6]
    refs = refs[6:]
    if emit_kv_f32:
        kf_ref, vf_ref = refs

    x = x_ref[...]
    ms = jnp.mean(x * x, axis=-1, keepdims=True)
    y = x * lax.rsqrt(ms + EPS) * gpre_ref[...]
    h = (y * (1.0 + scale_ref[...]) + shift_ref[...]).astype(BF16)

    def proj(g):
        return _dot(h, w_ref[:, g * D_MODEL:(g + 1) * D_MODEL])

    if use_rope:
        cos = cos_ref[...]
        sin = sin_ref[...]
        lane = lax.broadcasted_iota(jnp.int32, cos.shape, 1)
        take_next = (lane % (2 * ROPE_SWAP)) < ROPE_SWAP

        def rope(p):
            outs = []
            for hd in range(N_HEADS):
                xh = p[:, hd * HEAD_W:(hd + 1) * HEAD_W]
                partner = jnp.where(take_next,
                                    pltpu.roll(xh, HEAD_W - ROPE_SWAP, 1),
                                    pltpu.roll(xh, ROPE_SWAP, 1))
                outs.append(xh * cos + partner * sin)
            return jnp.concatenate(outs, axis=-1)
    else:
        def rope(p):
            return p

    q = rope(proj(0)) * (LOG2_E / math.sqrt(DIFF_HEAD_DIM))
    q_ref[...] = q.T.astype(BF16)
    k = rope(proj(1))
    k_ref[...] = k.astype(BF16)
    v = proj(2)
    vt = v.T
    for hd in range(N_HEADS):
        v_ref[hd] = vt[hd * HEAD_W:(hd + 1) * HEAD_W].astype(BF16)
    if emit_kv_f32:
        kf_ref[...] = k
        vf_ref[...] = v
    gatt_ref[...] = proj(3)
    xlru_ref[...] = proj(4)
    glru_ref[...] = proj(5)


def _inproj(x, scale, shift, g_pre, w_in_bf16, *, t_len, use_rope, emit_kv_f32, tm=256):
    n_tok = x.shape[0]
    tiles_per_batch = t_len // tm
    nb = scale.shape[0]
    if nb == 1:
        mod_map = lambda i: (0, 0, 0)
    else:
        mod_map = lambda i: (i // tiles_per_batch, 0, 0)
    tok_spec = pl.BlockSpec((tm, D_MODEL), lambda i: (i, 0))
    in_specs = [
        tok_spec,
        pl.BlockSpec((None, 1, D_MODEL), mod_map),
        pl.BlockSpec((None, 1, D_MODEL), mod_map),
        pl.BlockSpec((1, D_MODEL), lambda i: (0, 0)),
        pl.BlockSpec((D_MODEL, N_GROUPS * D_MODEL), lambda i: (0, 0)),
    ]
    args = [x, scale, shift, g_pre, w_in_bf16]
    if use_rope:
        cos, sin = _rope_tables(t_len)
        rope_spec = pl.BlockSpec((tm, HEAD_W), lambda i: (i % tiles_per_batch, 0))
        in_specs += [rope_spec, rope_spec]
        args += [cos, sin]
    n_batch = n_tok // t_len
    kc = _key_chunk(t_len)
    tiles_per_chunk = kc // tm
    out_specs = [
        pl.BlockSpec((None, D_MODEL, tm), lambda i: (i // tiles_per_batch, 0, i % tiles_per_batch)),
        tok_spec,
        pl.BlockSpec((None, N_HEADS, None, HEAD_W, tm),
                     lambda i: (i // tiles_per_batch, 0, (i % tiles_per_batch) // tiles_per_chunk, 0,
                                i % tiles_per_chunk)),
        tok_spec, tok_spec, tok_spec,
    ]
    out_shape = [
        jax.ShapeDtypeStruct((n_batch, D_MODEL, t_len), BF16),
        jax.ShapeDtypeStruct((n_tok, D_MODEL), BF16),
        jax.ShapeDtypeStruct((n_batch, N_HEADS, t_len // kc, HEAD_W, kc), BF16),
    ] + [jax.ShapeDtypeStruct((n_tok, D_MODEL), F32)] * 3
    if emit_kv_f32:
        out_specs += [tok_spec] * 2
        out_shape += [jax.ShapeDtypeStruct((n_tok, D_MODEL), F32)] * 2
    return pl.pallas_call(
        functools.partial(_inproj_kernel, use_rope=use_rope, emit_kv_f32=emit_kv_f32),
        grid=(n_tok // tm,),
        in_specs=in_specs,
        out_specs=out_specs,
        out_shape=out_shape,
        compiler_params=pltpu.CompilerParams(
            dimension_semantics=("arbitrary",), vmem_limit_bytes=V7X_VMEM_LIMIT_BYTES),
        name="inproj_rope" if use_rope else "inproj",
    )(*args)


def _key_chunk(t_len):
    return min(512, t_len)


def _attn_kernel(*refs, has_cache, n_chunks, kc, heads_per_step):
    lam_ref, gsub_ref, qt_ref, k_ref, vt_ref, gatt_ref = refs[:6]
    refs = refs[6:]
    cache_refs = None
    if has_cache:
        cache_refs = refs[:2]
        refs = refs[2:]
    o_ref, acc_ref, s_ref = refs

    lp = lam_ref[...]
    lam = (jnp.exp(jnp.sum(lp[0:1] * lp[1:2], axis=-1, keepdims=True))
           - jnp.exp(jnp.sum(lp[2:3] * lp[3:4], axis=-1, keepdims=True)) + LAM_INIT)

    for hh in range(heads_per_step):
        lanes = slice(hh * HEAD_W, (hh + 1) * HEAD_W)
        _attn_head(lam, gsub_ref[...], qt_ref.at[lanes, :], k_ref.at[:, lanes], vt_ref.at[hh],
                   gatt_ref.at[:, lanes], cache_refs, pl.program_id(1) * heads_per_step + hh,
                   o_ref.at[:, lanes], acc_ref, s_ref, n_chunks=n_chunks, kc=kc)


def _attn_head(lam, g_subln, qt_ref, k_ref, vt_ref, gatt_ref, cache_refs, head, o_ref, acc_ref, s_ref,
               *, n_chunks, kc):
    has_cache = cache_refs is not None
    qt = qt_ref[...]
    tq = qt.shape[1]
    row = lax.broadcasted_iota(jnp.int32, qt.shape, 0)
    zero = jnp.zeros_like(qt)
    q_maps = (jnp.where(row < DIFF_HEAD_DIM, qt, zero), jnp.where(row >= DIFF_HEAD_DIM, qt, zero))

    acc_ref[...] = jnp.zeros_like(acc_ref)

    def scores(k_chunk, slot):
        maxes = []
        for idx in range(2):
            st = _dot(k_chunk, q_maps[idx])
            s_ref[slot, idx, :k_chunk.shape[0], :] = st
            maxes.append(jnp.max(st, axis=0, keepdims=True))
        return tuple(maxes)

    def accumulate(vt_chunk, slot, ms, maxes):
        n_keys = vt_chunk.shape[1]
        ones = jnp.ones((ACC_ROWS - HEAD_W, n_keys), BF16)
        vt_ext = jnp.concatenate([vt_chunk, ones], axis=0)
        new_ms = []
        for idx in range(2):
            m_new = jnp.maximum(ms[idx], maxes[idx])
            alpha = jnp.exp2(ms[idx] - m_new)
            et = jnp.exp2(s_ref[slot, idx, :n_keys, :] - m_new).astype(BF16)
            acc_ref[idx] = acc_ref[idx] * alpha + _dot(vt_ext, et)
            new_ms.append(m_new)
        return tuple(new_ms)

    def new_keys(j):
        return k_ref[pl.ds(pl.multiple_of(j * kc, kc), kc), :]

    m0 = jnp.full((1, tq), -jnp.inf, F32)
    ms = (m0, m0)
    if has_cache:
        kc_ref, vc_ref = cache_refs
        head_rows = pl.ds(head, kc_ref.shape[0] // N_HEADS, stride=N_HEADS)
        cache_maxes = scores(kc_ref[head_rows, :].astype(BF16), 2)
        maxes = scores(new_keys(0), 0)
        ms = accumulate(vc_ref[head_rows, :].T.astype(BF16), 2, ms, cache_maxes)
    else:
        maxes = scores(new_keys(0), 0)

    def pair(t, carry):
        ms, maxes = carry
        j = 2 * t
        odd_maxes = scores(new_keys(j + 1), 1)
        ms = accumulate(vt_ref[j], 0, ms, maxes)
        maxes = scores(new_keys(j + 2), 0)
        ms = accumulate(vt_ref[j + 1], 1, ms, odd_maxes)
        return ms, maxes

    n_pairs = (n_chunks - 1) // 2
    ms, maxes = lax.fori_loop(0, n_pairs, pair, (ms, maxes))
    if n_chunks % 2 == 0:
        last_maxes = scores(new_keys(n_chunks - 1), 1)
        ms = accumulate(vt_ref[n_chunks - 2], 0, ms, maxes)
        accumulate(vt_ref[n_chunks - 1], 1, ms, last_maxes)
    else:
        accumulate(vt_ref[n_chunks - 1], 0, ms, maxes)

    acc1 = acc_ref[0]
    acc2 = acc_ref[1]
    o1 = acc1[:HEAD_W] * (1.0 / acc1[HEAD_W:HEAD_W + 1])
    o2 = acc2[:HEAD_W] * (1.0 / acc2[HEAD_W:HEAD_W + 1])
    o = (o1 - lam * o2).T

    ms = jnp.mean(o * o, axis=-1, keepdims=True)
    on = o * lax.rsqrt(ms + EPS) * g_subln * (1.0 - LAM_INIT)
    o_ref[...] = (on * _silu(gatt_ref[...])).astype(BF16)


def _attention(lam_params, g_subln, qt, k, vt, g_att, cache_k=None, cache_v=None, *, tq=512):
    b, t_len, _ = k.shape
    has_cache = cache_k is not None
    n_chunks, kc = vt.shape[2], vt.shape[4]
    tq = min(tq, t_len)
    heads_per_step = N_HEADS if n_chunks == 1 and not has_cache else 1
    width = heads_per_step * HEAD_W
    s_rows = max(kc, cache_k.shape[1] // N_HEADS) if has_cache else kc
    tok_spec = pl.BlockSpec((None, tq, width), lambda bi, h, qi: (bi, qi, h))
    in_specs = [
        pl.BlockSpec(lam_params.shape, lambda bi, h, qi: (0, 0)),
        pl.BlockSpec((1, HEAD_W), lambda bi, h, qi: (0, 0)),
        pl.BlockSpec((None, width, tq), lambda bi, h, qi: (bi, h, qi)),
        pl.BlockSpec((None, t_len, width), lambda bi, h, qi: (bi, 0, h)),
        pl.BlockSpec((None, heads_per_step, n_chunks, HEAD_W, kc), lambda bi, h, qi: (bi, h, 0, 0, 0)),
        tok_spec,
    ]
    args = [lam_params, g_subln, qt, k, vt, g_att]
    if has_cache:
        c_spec = pl.BlockSpec((None,) + cache_k.shape[1:], lambda bi, h, qi: (bi, 0, 0))
        in_specs += [c_spec, c_spec]
        args += [cache_k, cache_v]
    return pl.pallas_call(
        functools.partial(_attn_kernel, has_cache=has_cache, n_chunks=n_chunks, kc=kc,
                          heads_per_step=heads_per_step),
        grid=(b, N_HEADS // heads_per_step, t_len // tq),
        in_specs=in_specs,
        out_specs=tok_spec,
        out_shape=jax.ShapeDtypeStruct((b, t_len, D_ATT), BF16),
        scratch_shapes=[pltpu.VMEM((2, ACC_ROWS, tq), F32),
                        pltpu.VMEM((3 if has_cache else 2, 2, s_rows, tq), F32)],
        compiler_params=pltpu.CompilerParams(
            dimension_semantics=("arbitrary",) * 3, vmem_limit_bytes=V7X_VMEM_LIMIT_BYTES),
        name="diff_attn_cache" if has_cache else "diff_attn",
    )(*args)


def _lru_kernel(x_ref, g_ref, cw_ref, cb_ref, w_ref, b_ref, lam_ref, h0_ref,
                o_ref, st_ref,
                xp, a_f, b_f, a_b, b_b, hin_f, hin_b,
                *, t_len, chunk, pitch, rows):
    g_f, p_f, g_b, p_b = b_f, a_f, b_b, a_b
    n_chunks = t_len // chunk
    chunks_per_tile = rows // chunk

    pad = jnp.zeros((SUBLANES, LRU_BLOCK), F32)
    xp[pl.ds(0, SUBLANES), :] = pad
    xp[pl.ds(SUBLANES, t_len), :] = x_ref[...]
    xp[pl.ds(SUBLANES + t_len, SUBLANES), :] = pad

    lam = lam_ref[...]
    sp = jnp.maximum(-lam, 0.0) + jnp.log1p(jnp.exp(-jnp.abs(lam)))
    half_scale = (-0.5 * LRU_C) * sp
    cw = cw_ref[...]
    cb = cb_ref[...]
    wcat = w_ref[...]
    bcat = b_ref[...]

    def gate_tile(r, carry):
        t0 = pl.multiple_of(r * rows, SUBLANES)
        u = cb
        for j in range(4):
            u = u + xp[pl.ds(t0 + SUBLANES - 1 + j, rows), :] * cw[j:j + 1]
        zh = _dot(u.astype(BF16), wcat) + bcat
        u_half = 0.5 * u
        for d, (a_s, b_s) in enumerate(((a_f, b_f), (a_b, b_b))):
            tr = jnp.tanh(zh[:, (2 * d) * LRU_BLOCK:(2 * d + 1) * LRU_BLOCK])
            ti = jnp.tanh(zh[:, (2 * d + 1) * LRU_BLOCK:(2 * d + 2) * LRU_BLOCK])
            log_a = half_scale[d:d + 1] + half_scale[d:d + 1] * tr
            a = jnp.exp(log_a)
            gain_sq = jnp.tanh(log_a) * (-1.0 - a * a)
            gain = jnp.where(gain_sq > 0.0, gain_sq * lax.rsqrt(gain_sq), 0.0)
            bb = gain * (u_half + u_half * ti)
            for cc in range(chunks_per_tile):
                dst = pl.multiple_of(r * (chunks_per_tile * pitch), SUBLANES) + cc * pitch
                a_s[pl.ds(dst, chunk), :] = a[cc * chunk:(cc + 1) * chunk]
                b_s[pl.ds(dst, chunk), :] = bb[cc * chunk:(cc + 1) * chunk]
        return carry

    lax.fori_loop(0, t_len // rows, gate_tile, 0)

    def local_step(l, carry):
        hf, pf, hb, pb = carry
        rf = pl.ds(l, n_chunks, stride=pitch)
        a = a_f[rf, :]
        hf = a * hf + b_f[rf, :]
        pf = a * pf
        g_f[rf, :] = hf
        p_f[rf, :] = pf
        rb = pl.ds(chunk - 1 - l, n_chunks, stride=pitch)
        a = a_b[rb, :]
        hb = a * hb + b_b[rb, :]
        pb = a * pb
        g_b[rb, :] = hb
        p_b[rb, :] = pb
        return hf, pf, hb, pb

    z0 = jnp.zeros((n_chunks, LRU_BLOCK), F32)
    o0 = jnp.ones((n_chunks, LRU_BLOCK), F32)
    lax.fori_loop(0, chunk, local_step, (z0, o0, z0, o0), unroll=4)

    h0 = h0_ref[...]

    def carry_step(c, carry):
        hf, hb = carry
        hin_f[pl.ds(c, 1), :] = hf
        last = c * pitch + chunk - 1
        hf = g_f[pl.ds(last, 1), :] + p_f[pl.ds(last, 1), :] * hf
        cb_ = n_chunks - 1 - c
        hin_b[pl.ds(cb_, 1), :] = hb
        first = cb_ * pitch
        hb = g_b[pl.ds(first, 1), :] + p_b[pl.ds(first, 1), :] * hb
        return hf, hb

    hf_end, hb_end = lax.fori_loop(0, n_chunks, carry_step, (h0[0:1], h0[1:2]))
    st_ref[0:1, :] = hf_end
    st_ref[1:2, :] = hb_end

    def out_pair(cp, carry):
        for par in range(2):
            c = 2 * cp + par
            src = pl.ds(pl.multiple_of(cp * (2 * pitch), SUBLANES) + par * pitch, chunk)
            hf = g_f[src, :] + p_f[src, :] * hin_f[pl.ds(c, 1), :]
            hb = g_b[src, :] + p_b[src, :] * hin_b[pl.ds(c, 1), :]
            dst = pl.ds(pl.multiple_of(c * chunk, SUBLANES), chunk)
            o_ref[dst, :] = ((hf + hb) * _silu(g_ref[dst, :])).astype(BF16)
        return carry

    lax.fori_loop(0, n_chunks // 2, out_pair, 0, unroll=2)


def _lru(x_lru, g_lru, conv_w, conv_b, w_cat, b_cat, lru_lam, h0):
    b, t_len, _ = x_lru.shape
    chunk = 64 if t_len >= 2048 else 32
    pitch = chunk + SUBLANES // 2
    n_chunks = t_len // chunk
    rows = min(512, t_len)
    seq_spec = pl.BlockSpec((None, t_len, LRU_BLOCK), lambda bi, n: (bi, 0, n))
    st_spec = pl.BlockSpec((None, 2, LRU_BLOCK), lambda bi, n: (bi, 0, n))
    scan_buf = pltpu.VMEM((n_chunks * pitch, LRU_BLOCK), F32)
    return pl.pallas_call(
        functools.partial(_lru_kernel, t_len=t_len, chunk=chunk, pitch=pitch, rows=rows),
        grid=(b, N_LRU_BLOCKS),
        in_specs=[
            seq_spec, seq_spec,
            pl.BlockSpec((4, LRU_BLOCK), lambda bi, n: (0, n)),
            pl.BlockSpec((1, LRU_BLOCK), lambda bi, n: (0, n)),
            pl.BlockSpec((None, LRU_BLOCK, 4 * LRU_BLOCK), lambda bi, n: (n, 0, 0)),
            pl.BlockSpec((None, 1, 4 * LRU_BLOCK), lambda bi, n: (n, 0, 0)),
            pl.BlockSpec((2, LRU_BLOCK), lambda bi, n: (0, n)),
            st_spec,
        ],
        out_specs=[seq_spec, st_spec],
        out_shape=[jax.ShapeDtypeStruct((b, t_len, D_LRU), BF16),
                   jax.ShapeDtypeStruct((b, 2, D_LRU), F32)],
        scratch_shapes=[
            pltpu.VMEM((t_len + 2 * SUBLANES, LRU_BLOCK), F32),
            scan_buf, scan_buf, scan_buf, scan_buf,
            pltpu.VMEM((n_chunks, LRU_BLOCK), F32),
            pltpu.VMEM((n_chunks, LRU_BLOCK), F32),
        ],
        compiler_params=pltpu.CompilerParams(
            dimension_semantics=("arbitrary",) * 2, vmem_limit_bytes=V7X_VMEM_LIMIT_BYTES),
        name="rglru",
    )(x_lru, g_lru, conv_w, conv_b, w_cat, b_cat, lru_lam, h0)


def _outproj_kernel(att_ref, lru_ref, x_ref, gate_ref, wa_ref, wl_ref, gpost_ref, y_ref):
    o = _dot(att_ref[...], wa_ref[...]) + _dot(lru_ref[...], wl_ref[...])
    ms = jnp.mean(o * o, axis=-1, keepdims=True)
    n = o * lax.rsqrt(ms + EPS) * gpost_ref[...]
    y_ref[...] = x_ref[...] + gate_ref[...] * n


def _outproj(att, lru, x, gate, w_att, w_lru, g_post, *, t_len, tm=512):
    n_tok = x.shape[0]
    tm = min(tm, t_len)
    tiles_per_batch = t_len // tm
    nb = gate.shape[0]
    if nb == 1:
        mod_map = lambda i: (0, 0, 0)
    else:
        mod_map = lambda i: (i // tiles_per_batch, 0, 0)
    tok_spec = pl.BlockSpec((tm, D_MODEL), lambda i: (i, 0))
    w_spec = pl.BlockSpec((D_MODEL, D_MODEL), lambda i: (0, 0))
    return pl.pallas_call(
        _outproj_kernel,
        grid=(n_tok // tm,),
        in_specs=[tok_spec, tok_spec, tok_spec,
                  pl.BlockSpec((None, 1, D_MODEL), mod_map),
                  w_spec, w_spec,
                  pl.BlockSpec((1, D_MODEL), lambda i: (0, 0))],
        out_specs=tok_spec,
        out_shape=jax.ShapeDtypeStruct((n_tok, D_MODEL), F32),
        compiler_params=pltpu.CompilerParams(
            dimension_semantics=("arbitrary",), vmem_limit_bytes=V7X_VMEM_LIMIT_BYTES),
        name="outproj",
    )(att, lru, x, gate, w_att, w_lru, g_post)


def _sublayer(x, scale, shift, gate, cache_k, cache_v, h0, use_rope, w):
    b, t_len, _ = x.shape
    xf = x.reshape(b * t_len, D_MODEL)
    emit_kv = cache_k is None
    outs = _inproj(xf, scale, shift, w["g_pre"], w["w_in"], t_len=t_len,
                   use_rope=use_rope, emit_kv_f32=emit_kv)
    qt, vt = outs[0], outs[2]
    k, g_att, x_lru, g_lru = [outs[i].reshape(b, t_len, D_MODEL) for i in (1, 3, 4, 5)]
    att = _attention(w["lam"], w["g_subln"], qt, k, vt, g_att, cache_k, cache_v)
    lru, state = _lru(x_lru, g_lru, w["conv_w"], w["conv_b"], w["w_cat"], w["b_cat"], w["lru_lam"], h0)
    y = _outproj(att.reshape(b * t_len, D_ATT), lru.reshape(b * t_len, D_LRU), xf, gate,
                 w["w_out_att"], w["w_out_lru"], w["g_post"], t_len=t_len)
    y = y.reshape(b, t_len, D_MODEL)
    if emit_kv:
        return y, outs[6], outs[7], state
    return y, None, None, state


def kernel(x_prompt, x_sample, cache_k, cache_v, state_lru, c, c_ctx, w_ada, b_ada, g_pre, w_in, lambda_q1, lambda_k1, lambda_q2, lambda_k2, g_subln, conv_w, conv_b, w_rgate, b_rgate, w_igate, b_igate, lru_lambda, w_out, g_post):
    bp, seq, _ = x_prompt.shape
    bd, kc = cache_k.shape[0], cache_k.shape[2]
    assert cache_k.shape[1] == 1 and w_in.shape[0] == 1, "single-layer step only"
    l = 0

    cond = jnp.concatenate([c, c_ctx[None, :], jnp.zeros((SUBLANES - bd - 1, D_MODEL), F32)], axis=0)
    mod = _modulation(cond, w_ada[l], b_ada[l][None, :])
    shift, scale, gate = [mod[:, i * D_MODEL:(i + 1) * D_MODEL] for i in range(3)]

    def rows(a, lo, hi):
        return a[lo:hi][:, None, :]

    w = {
        "g_pre": g_pre[l][None, :],
        "w_in": w_in[l].astype(BF16),
        "lam": jnp.stack([lambda_q1[l], lambda_k1[l], lambda_q2[l], lambda_k2[l]], axis=0),
        "g_subln": g_subln[l][None, :],
        "conv_w": conv_w[l],
        "conv_b": conv_b[l][None, :],
        "w_cat": (0.5 * jnp.concatenate([w_rgate[l, 0], w_igate[l, 0], w_rgate[l, 1], w_igate[l, 1]],
                                        axis=-1)).astype(BF16),
        "b_cat": 0.5 * jnp.concatenate(
            [bb.reshape(N_LRU_BLOCKS, 1, LRU_BLOCK)
             for bb in (b_rgate[l, 0], b_igate[l, 0], b_rgate[l, 1], b_igate[l, 1])], axis=-1),
        "lru_lam": lru_lambda[l],
        "w_out_att": w_out[l, :D_ATT].astype(BF16),
        "w_out_lru": w_out[l, D_ATT:].astype(BF16),
        "g_post": g_post[l][None, :],
    }

    y_p, new_k, new_v, st_p = _sublayer(
        x_prompt, rows(scale, bd, bd + 1), rows(shift, bd, bd + 1), rows(gate, bd, bd + 1),
        None, None, jnp.zeros((bp, 2, D_LRU), F32), False, w)

    y_s, _, _, _ = _sublayer(
        x_sample, rows(scale, 0, bd), rows(shift, 0, bd), rows(gate, 0, bd),
        cache_k.reshape(bd, kc * N_HEADS, HEAD_W), cache_v.reshape(bd, kc * N_HEADS, HEAD_W),
        state_lru.reshape(bd, 2, D_LRU), True, w)

    new_k = new_k.reshape(bp, 1, seq, N_HEADS, HEAD_W)
    new_v = new_v.reshape(bp, 1, seq, N_HEADS, HEAD_W)
    return (y_p, y_s, new_k, new_v, st_p[:, None])
```

```python
import functools
import math

import jax
import jax.numpy as jnp
import numpy as np
from jax import lax
from jax.experimental import pallas as pl
from jax.experimental.pallas import tpu as pltpu

F32 = jnp.float32
BF16 = jnp.bfloat16

D_MODEL = 1024
GRID_W = 64
N_HEADS = 8
DIFF_HEAD_DIM = 64
HEAD_W = 2 * DIFF_HEAD_DIM
D_ATT = N_HEADS * HEAD_W
D_LRU = 1024
N_LRU_BLOCKS = 8
LRU_BLOCK = D_LRU // N_LRU_BLOCKS
LRU_C = 8.0
N_GROUPS = 6
ROPE_BASE = 10000.0
EPS = 1e-6
LAM_INIT = 0.8 - 0.6 * math.exp(-0.3 * 0)
LOG2_E = math.log2(math.e)

V7X_VMEM_LIMIT_BYTES = 56 * 1024 * 1024
SUBLANES = 8
ROPE_SWAP = DIFF_HEAD_DIM // 4
BF16_ROWS_PER_VREG = 16
ACC_ROWS = HEAD_W + BF16_ROWS_PER_VREG


def _silu(x):
    return x * jax.nn.sigmoid(x)


def _dot(a, b):
    return jnp.dot(a, b, preferred_element_type=F32)


def _dot_nt(a, b):
    return lax.dot_general(a, b, (((1,), (1,)), ((), ())), preferred_element_type=F32)


def _mod_kernel(cond_ref, w_ref, b_ref, o_ref):
    s = _silu(cond_ref[...])
    o_ref[...] = _dot(s.astype(BF16), w_ref[...].astype(BF16)) + b_ref[...]


def _modulation(cond, w_ada, b_ada):
    n = cond.shape[0]
    return pl.pallas_call(
        _mod_kernel,
        grid=(3,),
        in_specs=[
            pl.BlockSpec((n, D_MODEL), lambda j: (0, 0)),
            pl.BlockSpec((D_MODEL, D_MODEL), lambda j: (0, j)),
            pl.BlockSpec((1, D_MODEL), lambda j: (0, j)),
        ],
        out_specs=pl.BlockSpec((n, D_MODEL), lambda j: (0, j)),
        out_shape=jax.ShapeDtypeStruct((n, 3 * D_MODEL), F32),
        name="modulation",
    )(cond, w_ada, b_ada)


def _rope_tables(t_len):
    half = DIFF_HEAD_DIM // 2
    nf = half // 2
    t = np.arange(t_len)
    row = (t // GRID_W).astype(np.float32)
    col = (t % GRID_W).astype(np.float32)
    inv = (ROPE_BASE ** (-np.arange(nf, dtype=np.float32) * 2.0 / half)).astype(np.float32)
    lane = np.arange(HEAD_W) % DIFF_HEAD_DIM
    use_row = lane < half
    freq = (lane % half) % nf
    first = (lane % half) < nf
    pos = np.where(use_row[None, :], row[:, None], col[:, None]).astype(np.float32)
    ang = (pos * inv[freq][None, :]).astype(np.float32).astype(np.float64)
    cos = np.cos(ang).astype(np.float32)
    sin = (np.sin(ang) * np.where(first, -1.0, 1.0)[None, :]).astype(np.float32)
    return jnp.asarray(cos), jnp.asarray(sin)


def _inproj_kernel(*refs, use_rope, emit_kv_f32):
    x_ref, scale_ref, shift_ref, gpre_ref, w_ref = refs[:5]
    refs = refs[5:]
    if use_rope:
        cos_ref, sin_ref = refs[:2]
        refs = refs[2:]
    q_ref, k_ref, v_ref, gatt_ref, xlru_ref, glru_ref = refs[:6]
    refs = refs[6:]
    if emit_kv_f32:
        kf_ref, vf_ref = refs

    x = x_ref[...]
    ms = jnp.mean(x * x, axis=-1, keepdims=True)
    y = x * lax.rsqrt(ms + EPS) * gpre_ref[...]
    h = (y * (1.0 + scale_ref[...]) + shift_ref[...]).astype(BF16)

    def proj(g):
        return _dot(h, w_ref[:, g * D_MODEL:(g + 1) * D_MODEL])

    if use_rope:
        cos = cos_ref[...]
        sin = sin_ref[...]
        lane = lax.broadcasted_iota(jnp.int32, cos.shape, 1)
        take_next = (lane % (2 * ROPE_SWAP)) < ROPE_SWAP

        def rope(p):
            outs = []
            for hd in range(N_HEADS):
                xh = p[:, hd * HEAD_W:(hd + 1) * HEAD_W]
                partner = jnp.where(take_next,
                                    pltpu.roll(xh, HEAD_W - ROPE_SWAP, 1),
                                    pltpu.roll(xh, ROPE_SWAP, 1))
                outs.append(xh * cos + partner * sin)
            return jnp.concatenate(outs, axis=-1)
    else:
        def rope(p):
            return p

    q = rope(proj(0)) * (LOG2_E / math.sqrt(DIFF_HEAD_DIM))
    q_ref[...] = q.T.astype(BF16)
    k = rope(proj(1))
    k_ref[...] = k.astype(BF16)
    v = proj(2)
    vt = v.T
    for hd in range(N_HEADS):
        v_ref[hd] = vt[hd * HEAD_W:(hd + 1) * HEAD_W].astype(BF16)
    if emit_kv_f32:
        kf_ref[...] = k
        vf_ref[...] = v
    gatt_ref[...] = proj(3)
    xlru_ref[...] = proj(4)
    glru_ref[...] = proj(5)


def _inproj(x, scale, shift, g_pre, w_in_bf16, *, t_len, use_rope, emit_kv_f32, tm=256):
    n_tok = x.shape[0]
    tiles_per_batch = t_len // tm
    nb = scale.shape[0]
    if nb == 1:
        mod_map = lambda i: (0, 0, 0)
    else:
        mod_map = lambda i: (i // tiles_per_batch, 0, 0)
    tok_spec = pl.BlockSpec((tm, D_MODEL), lambda i: (i, 0))
    in_specs = [
        tok_spec,
        pl.BlockSpec((None, 1, D_MODEL), mod_map),
        pl.BlockSpec((None, 1, D_MODEL), mod_map),
        pl.BlockSpec((1, D_MODEL), lambda i: (0, 0)),
        pl.BlockSpec((D_MODEL, N_GROUPS * D_MODEL), lambda i: (0, 0)),
    ]
    args = [x, scale, shift, g_pre, w_in_bf16]
    if use_rope:
        cos, sin = _rope_tables(t_len)
        rope_spec = pl.BlockSpec((tm, HEAD_W), lambda i: (i % tiles_per_batch, 0))
        in_specs += [rope_spec, rope_spec]
        args += [cos, sin]
    n_batch = n_tok // t_len
    kc = _key_chunk(t_len)
    tiles_per_chunk = kc // tm
    out_specs = [
        pl.BlockSpec((None, D_MODEL, tm), lambda i: (i // tiles_per_batch, 0, i % tiles_per_batch)),
        tok_spec,
        pl.BlockSpec((None, N_HEADS, None, HEAD_W, tm),
                     lambda i: (i // tiles_per_batch, 0, (i % tiles_per_batch) // tiles_per_chunk, 0,
                                i % tiles_per_chunk)),
        tok_spec, tok_spec, tok_spec,
    ]
    out_shape = [
        jax.ShapeDtypeStruct((n_batch, D_MODEL, t_len), BF16),
        jax.ShapeDtypeStruct((n_tok, D_MODEL), BF16),
        jax.ShapeDtypeStruct((n_batch, N_HEADS, t_len // kc, HEAD_W, kc), BF16),
    ] + [jax.ShapeDtypeStruct((n_tok, D_MODEL), F32)] * 3
    if emit_kv_f32:
        out_specs += [tok_spec] * 2
        out_shape += [jax.ShapeDtypeStruct((n_tok, D_MODEL), F32)] * 2
    return pl.pallas_call(
        functools.partial(_inproj_kernel, use_rope=use_rope, emit_kv_f32=emit_kv_f32),
        grid=(n_tok // tm,),
        in_specs=in_specs,
        out_specs=out_specs,
        out_shape=out_shape,
        compiler_params=pltpu.CompilerParams(
            dimension_semantics=("arbitrary",), vmem_limit_bytes=V7X_VMEM_LIMIT_BYTES),
        name="inproj_rope" if use_rope else "inproj",
    )(*args)


def _key_chunk(t_len):
    return min(512, t_len)


def _attn_kernel(*refs, has_cache, n_chunks, kc, heads_per_step):
    lam_ref, gsub_ref, qt_ref, k_ref, vt_ref, gatt_ref = refs[:6]
    refs = refs[6:]
    cache_refs = None
    if has_cache:
        cache_refs = refs[:2]
        refs = refs[2:]
    o_ref, acc_ref, s_ref, e_ref = refs

    lp = lam_ref[...]
    lam = (jnp.exp(jnp.sum(lp[0:1] * lp[1:2], axis=-1, keepdims=True))
           - jnp.exp(jnp.sum(lp[2:3] * lp[3:4], axis=-1, keepdims=True)) + LAM_INIT)

    for hh in range(heads_per_step):
        lanes = slice(hh * HEAD_W, (hh + 1) * HEAD_W)
        _attn_head(lam, gsub_ref[...], qt_ref.at[lanes, :], k_ref.at[:, lanes], vt_ref.at[hh],
                   gatt_ref.at[:, lanes], cache_refs, pl.program_id(1) * heads_per_step + hh,
                   o_ref.at[:, lanes], acc_ref, s_ref, e_ref, n_chunks=n_chunks, kc=kc)


def _attn_head(lam, g_subln, qt_ref, k_ref, vt_ref, gatt_ref, cache_refs, head, o_ref, acc_ref, s_ref,
               e_ref, *, n_chunks, kc):
    has_cache = cache_refs is not None
    qt = qt_ref[...]
    tq = qt.shape[1]
    row = lax.broadcasted_iota(jnp.int32, qt.shape, 0)
    zero = jnp.zeros_like(qt)
    q_maps = (jnp.where(row < DIFF_HEAD_DIM, qt, zero), jnp.where(row >= DIFF_HEAD_DIM, qt, zero))

    acc_ref[...] = jnp.zeros_like(acc_ref)

    n_cache = 1 if has_cache else 0
    n_total = n_chunks + n_cache
    if has_cache:
        kc_ref, vc_ref = cache_refs
        head_rows = pl.ds(head, kc_ref.shape[0] // N_HEADS, stride=N_HEADS)

    def keys_of(c):
        if has_cache and isinstance(c, int) and c == 0:
            return kc_ref[head_rows, :].astype(BF16)
        start = (c - n_cache) * kc
        return k_ref[pl.ds(start if isinstance(c, int) else pl.multiple_of(start, kc), kc), :]

    def values_of(c):
        if has_cache and isinstance(c, int) and c == 0:
            return vc_ref[head_rows, :].T.astype(BF16)
        return vt_ref[c - n_cache]

    def scores(c, slot):
        k_chunk = keys_of(c)
        maxes = []
        for idx in range(2):
            st = _dot(k_chunk, q_maps[idx])
            s_ref[slot, idx, :k_chunk.shape[0], :] = st
            maxes.append(jnp.max(st, axis=0, keepdims=True))
        return tuple(maxes)

    def probs(n_keys, slot, ms, maxes):
        new_ms, alphas = [], []
        for idx in range(2):
            m_new = jnp.maximum(ms[idx], maxes[idx])
            alphas.append(jnp.exp2(ms[idx] - m_new))
            e_ref[slot, idx, :n_keys, :] = jnp.exp2(s_ref[slot, idx, :n_keys, :] - m_new).astype(BF16)
            new_ms.append(m_new)
        return tuple(new_ms), tuple(alphas)

    def values(c, slot, alphas):
        vt_chunk = values_of(c)
        n_keys = vt_chunk.shape[1]
        ones = jnp.ones((ACC_ROWS - HEAD_W, n_keys), BF16)
        vt_ext = jnp.concatenate([vt_chunk, ones], axis=0)
        for idx in range(2):
            acc_ref[idx] = acc_ref[idx] * alphas[idx] + _dot(vt_ext, e_ref[slot, idx, :n_keys, :])

    def n_keys_of(c):
        return kc_ref.shape[0] // N_HEADS if (has_cache and isinstance(c, int) and c == 0) else kc

    def step(i, slot, ms, maxes_next, alphas_cur):
        maxes_after = scores(i + 2, slot)
        ms, alphas_next = probs(n_keys_of(i + 1), 1 - slot, ms, maxes_next)
        values(i, slot, alphas_cur)
        return ms, maxes_after, alphas_next

    m0 = jnp.full((1, tq), -jnp.inf, F32)
    ms = (m0, m0)
    maxes_cur = scores(0, 0)
    if n_total == 1:
        ms, alphas_cur = probs(n_keys_of(0), 0, ms, maxes_cur)
        values(0, 0, alphas_cur)
    else:
        maxes_next = scores(1, 1)
        ms, alphas_cur = probs(n_keys_of(0), 0, ms, maxes_cur)
        n_steps = n_total - 2
        first_loop = n_cache
        for i in range(min(first_loop, n_steps)):
            ms, maxes_next, alphas_cur = step(i, i % 2, ms, maxes_next, alphas_cur)
        n_pairs = max(n_steps - first_loop, 0) // 2

        def pair(t, carry):
            ms, maxes_next, alphas_cur = carry
            i = first_loop + 2 * t
            ms, maxes_next, alphas_cur = step(i, first_loop % 2, ms, maxes_next, alphas_cur)
            return step(i + 1, (first_loop + 1) % 2, ms, maxes_next, alphas_cur)

        ms, maxes_next, alphas_cur = lax.fori_loop(0, n_pairs, pair, (ms, maxes_next, alphas_cur))
        for i in range(first_loop + 2 * n_pairs, n_steps):
            ms, maxes_next, alphas_cur = step(i, i % 2, ms, maxes_next, alphas_cur)
        last = n_total - 1
        ms, alphas_last = probs(n_keys_of(last), last % 2, ms, maxes_next)
        values(last - 1, (last - 1) % 2, alphas_cur)
        values(last, last % 2, alphas_last)

    acc1 = acc_ref[0]
    acc2 = acc_ref[1]
    o1 = acc1[:HEAD_W] * (1.0 / acc1[HEAD_W:HEAD_W + 1])
    o2 = acc2[:HEAD_W] * (1.0 / acc2[HEAD_W:HEAD_W + 1])
    o = (o1 - lam * o2).T

    ms = jnp.mean(o * o, axis=-1, keepdims=True)
    on = o * lax.rsqrt(ms + EPS) * g_subln * (1.0 - LAM_INIT)
    o_ref[...] = (on * _silu(gatt_ref[...])).astype(BF16)


def _attention(lam_params, g_subln, qt, k, vt, g_att, cache_k=None, cache_v=None, *, tq=2048):
    b, t_len, _ = k.shape
    has_cache = cache_k is not None
    n_chunks, kc = vt.shape[2], vt.shape[4]
    tq = min(tq, t_len)
    heads_per_step = N_HEADS if n_chunks == 1 and not has_cache else 1
    width = heads_per_step * HEAD_W
    s_rows = max(kc, cache_k.shape[1] // N_HEADS) if has_cache else kc
    tok_spec = pl.BlockSpec((None, tq, width), lambda bi, h, qi: (bi, qi, h))
    in_specs = [
        pl.BlockSpec(lam_params.shape, lambda bi, h, qi: (0, 0)),
        pl.BlockSpec((1, HEAD_W), lambda bi, h, qi: (0, 0)),
        pl.BlockSpec((None, width, tq), lambda bi, h, qi: (bi, h, qi)),
        pl.BlockSpec((None, t_len, width), lambda bi, h, qi: (bi, 0, h)),
        pl.BlockSpec((None, heads_per_step, n_chunks, HEAD_W, kc), lambda bi, h, qi: (bi, h, 0, 0, 0)),
        tok_spec,
    ]
    args = [lam_params, g_subln, qt, k, vt, g_att]
    if has_cache:
        c_spec = pl.BlockSpec((None,) + cache_k.shape[1:], lambda bi, h, qi: (bi, 0, 0))
        in_specs += [c_spec, c_spec]
        args += [cache_k, cache_v]
    return pl.pallas_call(
        functools.partial(_attn_kernel, has_cache=has_cache, n_chunks=n_chunks, kc=kc,
                          heads_per_step=heads_per_step),
        grid=(b, N_HEADS // heads_per_step, t_len // tq),
        in_specs=in_specs,
        out_specs=tok_spec,
        out_shape=jax.ShapeDtypeStruct((b, t_len, D_ATT), BF16),
        scratch_shapes=[pltpu.VMEM((2, ACC_ROWS, tq), F32),
                        pltpu.VMEM((2, 2, s_rows, tq), F32),
                        pltpu.VMEM((2, 2, s_rows, tq), BF16)],
        compiler_params=pltpu.CompilerParams(
            dimension_semantics=("arbitrary",) * 3, vmem_limit_bytes=V7X_VMEM_LIMIT_BYTES),
        name="diff_attn_cache" if has_cache else "diff_attn",
    )(*args)


def _lru_kernel(x_ref, g_ref, cw_ref, cb_ref, w_ref, b_ref, lam_ref, h0_ref,
                o_ref, st_ref,
                xp, a_f, b_f, a_b, b_b, hin_f, hin_b,
                *, t_len, chunk, pitch, rows):
    g_f, p_f, g_b, p_b = b_f, a_f, b_b, a_b
    n_chunks = t_len // chunk
    chunks_per_tile = rows // chunk

    pad = jnp.zeros((SUBLANES, LRU_BLOCK), F32)
    xp[pl.ds(0, SUBLANES), :] = pad
    xp[pl.ds(SUBLANES, t_len), :] = x_ref[...]
    xp[pl.ds(SUBLANES + t_len, SUBLANES), :] = pad

    lam = lam_ref[...]
    sp = jnp.maximum(-lam, 0.0) + jnp.log1p(jnp.exp(-jnp.abs(lam)))
    half_scale = (-0.5 * LRU_C) * sp
    cw = cw_ref[...]
    cb = cb_ref[...]
    wcat = w_ref[...]
    bcat = b_ref[...]

    def gate_tile(r, carry):
        t0 = pl.multiple_of(r * rows, SUBLANES)
        u = cb
        for j in range(4):
            u = u + xp[pl.ds(t0 + SUBLANES - 1 + j, rows), :] * cw[j:j + 1]
        zh = _dot(u.astype(BF16), wcat) + bcat
        u_half = 0.5 * u
        for d, (a_s, b_s) in enumerate(((a_f, b_f), (a_b, b_b))):
            tr = jnp.tanh(zh[:, (2 * d) * LRU_BLOCK:(2 * d + 1) * LRU_BLOCK])
            ti = jnp.tanh(zh[:, (2 * d + 1) * LRU_BLOCK:(2 * d + 2) * LRU_BLOCK])
            log_a = half_scale[d:d + 1] + half_scale[d:d + 1] * tr
            a = jnp.exp(log_a)
            gain_sq = jnp.tanh(log_a) * (-1.0 - a * a)
            gain = jnp.where(gain_sq > 0.0, gain_sq * lax.rsqrt(gain_sq), 0.0)
            bb = gain * (u_half + u_half * ti)
            for cc in range(chunks_per_tile):
                dst = pl.multiple_of(r * (chunks_per_tile * pitch), SUBLANES) + cc * pitch
                a_s[pl.ds(dst, chunk), :] = a[cc * chunk:(cc + 1) * chunk]
                b_s[pl.ds(dst, chunk), :] = bb[cc * chunk:(cc + 1) * chunk]
        return carry

    lax.fori_loop(0, t_len // rows, gate_tile, 0)

    def local_step(l, carry):
        hf, pf, hb, pb = carry
        rf = pl.ds(l, n_chunks, stride=pitch)
        a = a_f[rf, :]
        hf = a * hf + b_f[rf, :]
        pf = a * pf
        g_f[rf, :] = hf
        p_f[rf, :] = pf
        rb = pl.ds(chunk - 1 - l, n_chunks, stride=pitch)
        a = a_b[rb, :]
        hb = a * hb + b_b[rb, :]
        pb = a * pb
        g_b[rb, :] = hb
        p_b[rb, :] = pb
        return hf, pf, hb, pb

    z0 = jnp.zeros((n_chunks, LRU_BLOCK), F32)
    o0 = jnp.ones((n_chunks, LRU_BLOCK), F32)
    lax.fori_loop(0, chunk, local_step, (z0, o0, z0, o0), unroll=4)

    h0 = h0_ref[...]

    def carry_step(c, carry):
        hf, hb = carry
        hin_f[pl.ds(c, 1), :] = hf
        last = c * pitch + chunk - 1
        hf = g_f[pl.ds(last, 1), :] + p_f[pl.ds(last, 1), :] * hf
        cb_ = n_chunks - 1 - c
        hin_b[pl.ds(cb_, 1), :] = hb
        first = cb_ * pitch
        hb = g_b[pl.ds(first, 1), :] + p_b[pl.ds(first, 1), :] * hb
        return hf, hb

    hf_end, hb_end = lax.fori_loop(0, n_chunks, carry_step, (h0[0:1], h0[1:2]))
    st_ref[0:1, :] = hf_end
    st_ref[1:2, :] = hb_end

    def out_pair(cp, carry):
        for par in range(2):
            c = 2 * cp + par
            src = pl.ds(pl.multiple_of(cp * (2 * pitch), SUBLANES) + par * pitch, chunk)
            hf = g_f[src, :] + p_f[src, :] * hin_f[pl.ds(c, 1), :]
            hb = g_b[src, :] + p_b[src, :] * hin_b[pl.ds(c, 1), :]
            dst = pl.ds(pl.multiple_of(c * chunk, SUBLANES), chunk)
            o_ref[dst, :] = ((hf + hb) * _silu(g_ref[dst, :])).astype(BF16)
        return carry

    lax.fori_loop(0, n_chunks // 2, out_pair, 0, unroll=2)


def _lru(x_lru, g_lru, conv_w, conv_b, w_cat, b_cat, lru_lam, h0):
    b, t_len, _ = x_lru.shape
    chunk = 64 if t_len >= 2048 else 32
    pitch = chunk + SUBLANES // 2
    n_chunks = t_len // chunk
    rows = min(512, t_len)
    seq_spec = pl.BlockSpec((None, t_len, LRU_BLOCK), lambda bi, n: (bi, 0, n))
    st_spec = pl.BlockSpec((None, 2, LRU_BLOCK), lambda bi, n: (bi, 0, n))
    scan_buf = pltpu.VMEM((n_chunks * pitch, LRU_BLOCK), F32)
    return pl.pallas_call(
        functools.partial(_lru_kernel, t_len=t_len, chunk=chunk, pitch=pitch, rows=rows),
        grid=(b, N_LRU_BLOCKS),
        in_specs=[
            seq_spec, seq_spec,
            pl.BlockSpec((4, LRU_BLOCK), lambda bi, n: (0, n)),
            pl.BlockSpec((1, LRU_BLOCK), lambda bi, n: (0, n)),
            pl.BlockSpec((None, LRU_BLOCK, 4 * LRU_BLOCK), lambda bi, n: (n, 0, 0)),
            pl.BlockSpec((None, 1, 4 * LRU_BLOCK), lambda bi, n: (n, 0, 0)),
            pl.BlockSpec((2, LRU_BLOCK), lambda bi, n: (0, n)),
            st_spec,
        ],
        out_specs=[seq_spec, st_spec],
        out_shape=[jax.ShapeDtypeStruct((b, t_len, D_LRU), BF16),
                   jax.ShapeDtypeStruct((b, 2, D_LRU), F32)],
        scratch_shapes=[
            pltpu.VMEM((t_len + 2 * SUBLANES, LRU_BLOCK), F32),
            scan_buf, scan_buf, scan_buf, scan_buf,
            pltpu.VMEM((n_chunks, LRU_BLOCK), F32),
            pltpu.VMEM((n_chunks, LRU_BLOCK), F32),
        ],
        compiler_params=pltpu.CompilerParams(
            dimension_semantics=("arbitrary",) * 2, vmem_limit_bytes=V7X_VMEM_LIMIT_BYTES),
        name="rglru",
    )(x_lru, g_lru, conv_w, conv_b, w_cat, b_cat, lru_lam, h0)


def _outproj_kernel(att_ref, lru_ref, x_ref, gate_ref, wa_ref, wl_ref, gpost_ref, y_ref):
    o = _dot(att_ref[...], wa_ref[...]) + _dot(lru_ref[...], wl_ref[...])
    ms = jnp.mean(o * o, axis=-1, keepdims=True)
    n = o * lax.rsqrt(ms + EPS) * gpost_ref[...]
    y_ref[...] = x_ref[...] + gate_ref[...] * n


def _outproj(att, lru, x, gate, w_att, w_lru, g_post, *, t_len, tm=512):
    n_tok = x.shape[0]
    tm = min(tm, t_len)
    tiles_per_batch = t_len // tm
    nb = gate.shape[0]
    if nb == 1:
        mod_map = lambda i: (0, 0, 0)
    else:
        mod_map = lambda i: (i // tiles_per_batch, 0, 0)
    tok_spec = pl.BlockSpec((tm, D_MODEL), lambda i: (i, 0))
    w_spec = pl.BlockSpec((D_MODEL, D_MODEL), lambda i: (0, 0))
    return pl.pallas_call(
        _outproj_kernel,
        grid=(n_tok // tm,),
        in_specs=[tok_spec, tok_spec, tok_spec,
                  pl.BlockSpec((None, 1, D_MODEL), mod_map),
                  w_spec, w_spec,
                  pl.BlockSpec((1, D_MODEL), lambda i: (0, 0))],
        out_specs=tok_spec,
        out_shape=jax.ShapeDtypeStruct((n_tok, D_MODEL), F32),
        compiler_params=pltpu.CompilerParams(
            dimension_semantics=("arbitrary",), vmem_limit_bytes=V7X_VMEM_LIMIT_BYTES),
        name="outproj",
    )(att, lru, x, gate, w_att, w_lru, g_post)


def _sublayer(x, scale, shift, gate, cache_k, cache_v, h0, use_rope, w):
    b, t_len, _ = x.shape
    xf = x.reshape(b * t_len, D_MODEL)
    emit_kv = cache_k is None
    outs = _inproj(xf, scale, shift, w["g_pre"], w["w_in"], t_len=t_len,
                   use_rope=use_rope, emit_kv_f32=emit_kv)
    qt, vt = outs[0], outs[2]
    k, g_att, x_lru, g_lru = [outs[i].reshape(b, t_len, D_MODEL) for i in (1, 3, 4, 5)]
    att = _attention(w["lam"], w["g_subln"], qt, k, vt, g_att, cache_k, cache_v)
    lru, state = _lru(x_lru, g_lru, w["conv_w"], w["conv_b"], w["w_cat"], w["b_cat"], w["lru_lam"], h0)
    y = _outproj(att.reshape(b * t_len, D_ATT), lru.reshape(b * t_len, D_LRU), xf, gate,
                 w["w_out_att"], w["w_out_lru"], w["g_post"], t_len=t_len)
    y = y.reshape(b, t_len, D_MODEL)
    if emit_kv:
        return y, outs[6], outs[7], state
    return y, None, None, state


def kernel(x_prompt, x_sample, cache_k, cache_v, state_lru, c, c_ctx, w_ada, b_ada, g_pre, w_in, lambda_q1, lambda_k1, lambda_q2, lambda_k2, g_subln, conv_w, conv_b, w_rgate, b_rgate, w_igate, b_igate, lru_lambda, w_out, g_post):
    bp, seq, _ = x_prompt.shape
    bd, kc = cache_k.shape[0], cache_k.shape[2]
    assert cache_k.shape[1] == 1 and w_in.shape[0] == 1, "single-layer step only"
    l = 0

    cond = jnp.concatenate([c, c_ctx[None, :], jnp.zeros((SUBLANES - bd - 1, D_MODEL), F32)], axis=0)
    mod = _modulation(cond, w_ada[l], b_ada[l][None, :])
    shift, scale, gate = [mod[:, i * D_MODEL:(i + 1) * D_MODEL] for i in range(3)]

    def rows(a, lo, hi):
        return a[lo:hi][:, None, :]

    w = {
        "g_pre": g_pre[l][None, :],
        "w_in": w_in[l].astype(BF16),
        "lam": jnp.stack([lambda_q1[l], lambda_k1[l], lambda_q2[l], lambda_k2[l]], axis=0),
        "g_subln": g_subln[l][None, :],
        "conv_w": conv_w[l],
        "conv_b": conv_b[l][None, :],
        "w_cat": (0.5 * jnp.concatenate([w_rgate[l, 0], w_igate[l, 0], w_rgate[l, 1], w_igate[l, 1]],
                                        axis=-1)).astype(BF16),
        "b_cat": 0.5 * jnp.concatenate(
            [bb.reshape(N_LRU_BLOCKS, 1, LRU_BLOCK)
             for bb in (b_rgate[l, 0], b_igate[l, 0], b_rgate[l, 1], b_igate[l, 1])], axis=-1),
        "lru_lam": lru_lambda[l],
        "w_out_att": w_out[l, :D_ATT].astype(BF16),
        "w_out_lru": w_out[l, D_ATT:].astype(BF16),
        "g_post": g_post[l][None, :],
    }

    y_p, new_k, new_v, st_p = _sublayer(
        x_prompt, rows(scale, bd, bd + 1), rows(shift, bd, bd + 1), rows(gate, bd, bd + 1),
        None, None, jnp.zeros((bp, 2, D_LRU), F32), False, w)

    y_s, _, _, _ = _sublayer(
        x_sample, rows(scale, 0, bd), rows(shift, 0, bd), rows(gate, 0, bd),
        cache_k.reshape(bd, kc * N_HEADS, HEAD_W), cache_v.reshape(bd, kc * N_HEADS, HEAD_W),
        state_lru.reshape(bd, 2, D_LRU), True, w)

    new_k = new_k.reshape(bp, 1, seq, N_HEADS, HEAD_W)
    new_v = new_v.reshape(bp, 1, seq, N_HEADS, HEAD_W)
    return (y_p, y_s, new_k, new_v, st_p[:, None])
```

```python
import functools
import math

import jax
import jax.numpy as jnp
import numpy as np
from jax import lax
from jax.experimental import pallas as pl
from jax.experimental.pallas import tpu as pltpu

F32 = jnp.float32
BF16 = jnp.bfloat16

D_MODEL = 1024
GRID_W = 64
N_HEADS = 8
DIFF_HEAD_DIM = 64
HEAD_W = 2 * DIFF_HEAD_DIM
D_ATT = N_HEADS * HEAD_W
D_LRU = 1024
N_LRU_BLOCKS = 8
LRU_BLOCK = D_LRU // N_LRU_BLOCKS
LRU_C = 8.0
N_GROUPS = 6
ROPE_BASE = 10000.0
EPS = 1e-6
LAM_INIT = 0.8 - 0.6 * math.exp(-0.3 * 0)
LOG2_E = math.log2(math.e)

V7X_VMEM_LIMIT_BYTES = 56 * 1024 * 1024
SUBLANES = 8
ROPE_SWAP = DIFF_HEAD_DIM // 4
BF16_ROWS_PER_VREG = 16
ACC_ROWS = HEAD_W + BF16_ROWS_PER_VREG


def _silu(x):
    return x * jax.nn.sigmoid(x)


def _dot(a, b):
    return jnp.dot(a, b, preferred_element_type=F32)


def _dot_nt(a, b):
    return lax.dot_general(a, b, (((1,), (1,)), ((), ())), preferred_element_type=F32)


def _mod_kernel(cond_ref, w_ref, b_ref, o_ref):
    s = _silu(cond_ref[...])
    o_ref[...] = _dot(s.astype(BF16), w_ref[...].astype(BF16)) + b_ref[...]


def _modulation(cond, w_ada, b_ada):
    n = cond.shape[0]
    return pl.pallas_call(
        _mod_kernel,
        grid=(3,),
        in_specs=[
            pl.BlockSpec((n, D_MODEL), lambda j: (0, 0)),
            pl.BlockSpec((D_MODEL, D_MODEL), lambda j: (0, j)),
            pl.BlockSpec((1, D_MODEL), lambda j: (0, j)),
        ],
        out_specs=pl.BlockSpec((n, D_MODEL), lambda j: (0, j)),
        out_shape=jax.ShapeDtypeStruct((n, 3 * D_MODEL), F32),
        name="modulation",
    )(cond, w_ada, b_ada)


def _rope_tables(t_len):
    half = DIFF_HEAD_DIM // 2
    nf = half // 2
    t = np.arange(t_len)
    row = (t // GRID_W).astype(np.float32)
    col = (t % GRID_W).astype(np.float32)
    inv = (ROPE_BASE ** (-np.arange(nf, dtype=np.float32) * 2.0 / half)).astype(np.float32)
    lane = np.arange(HEAD_W) % DIFF_HEAD_DIM
    use_row = lane < half
    freq = (lane % half) % nf
    first = (lane % half) < nf
    pos = np.where(use_row[None, :], row[:, None], col[:, None]).astype(np.float32)
    ang = (pos * inv[freq][None, :]).astype(np.float32).astype(np.float64)
    cos = np.cos(ang).astype(np.float32)
    sin = (np.sin(ang) * np.where(first, -1.0, 1.0)[None, :]).astype(np.float32)
    return jnp.asarray(cos), jnp.asarray(sin)


def _inproj_kernel(*refs, use_rope, emit_kv_f32):
    x_ref, scale_ref, shift_ref, gpre_ref, w_ref = refs[:5]
    refs = refs[5:]
    if use_rope:
        cos_ref, sin_ref = refs[:2]
        refs = refs[2:]
    q_ref, k_ref, v_ref, gatt_ref, xlru_ref, glru_ref = refs[:6]
    refs = refs[6:]
    if emit_kv_f32:
        kf_ref, vf_ref = refs

    x = x_ref[...]
    ms = jnp.mean(x * x, axis=-1, keepdims=True)
    y = x * lax.rsqrt(ms + EPS) * gpre_ref[...]
    h = (y * (1.0 + scale_ref[...]) + shift_ref[...]).astype(BF16)

    def proj(g):
        return _dot(h, w_ref[:, g * D_MODEL:(g + 1) * D_MODEL])

    if use_rope:
        cos = cos_ref[...]
        sin = sin_ref[...]
        lane = lax.broadcasted_iota(jnp.int32, cos.shape, 1)
        take_next = (lane % (2 * ROPE_SWAP)) < ROPE_SWAP

        def rope(p):
            outs = []
            for hd in range(N_HEADS):
                xh = p[:, hd * HEAD_W:(hd + 1) * HEAD_W]
                partner = jnp.where(take_next,
                                    pltpu.roll(xh, HEAD_W - ROPE_SWAP, 1),
                                    pltpu.roll(xh, ROPE_SWAP, 1))
                outs.append(xh * cos + partner * sin)
            return jnp.concatenate(outs, axis=-1)
    else:
        def rope(p):
            return p

    q = rope(proj(0)) * (LOG2_E / math.sqrt(DIFF_HEAD_DIM))
    q_ref[...] = q.T.astype(BF16)
    k = rope(proj(1))
    k_ref[...] = k.astype(BF16)
    v = proj(2)
    vt = v.T
    for hd in range(N_HEADS):
        v_ref[hd] = vt[hd * HEAD_W:(hd + 1) * HEAD_W].astype(BF16)
    if emit_kv_f32:
        kf_ref[...] = k
        vf_ref[...] = v
    gatt_ref[...] = proj(3)
    xlru_ref[...] = proj(4)
    glru_ref[...] = proj(5)


def _inproj(x, scale, shift, g_pre, w_in_bf16, *, t_len, use_rope, emit_kv_f32, tm=256):
    n_tok = x.shape[0]
    tiles_per_batch = t_len // tm
    nb = scale.shape[0]
    if nb == 1:
        mod_map = lambda i: (0, 0, 0)
    else:
        mod_map = lambda i: (i // tiles_per_batch, 0, 0)
    tok_spec = pl.BlockSpec((tm, D_MODEL), lambda i: (i, 0))
    in_specs = [
        tok_spec,
        pl.BlockSpec((None, 1, D_MODEL), mod_map),
        pl.BlockSpec((None, 1, D_MODEL), mod_map),
        pl.BlockSpec((1, D_MODEL), lambda i: (0, 0)),
        pl.BlockSpec((D_MODEL, N_GROUPS * D_MODEL), lambda i: (0, 0)),
    ]
    args = [x, scale, shift, g_pre, w_in_bf16]
    if use_rope:
        cos, sin = _rope_tables(t_len)
        rope_spec = pl.BlockSpec((tm, HEAD_W), lambda i: (i % tiles_per_batch, 0))
        in_specs += [rope_spec, rope_spec]
        args += [cos, sin]
    n_batch = n_tok // t_len
    kc = _key_chunk(t_len)
    tiles_per_chunk = kc // tm
    out_specs = [
        pl.BlockSpec((None, D_MODEL, tm), lambda i: (i // tiles_per_batch, 0, i % tiles_per_batch)),
        tok_spec,
        pl.BlockSpec((None, N_HEADS, None, HEAD_W, tm),
                     lambda i: (i // tiles_per_batch, 0, (i % tiles_per_batch) // tiles_per_chunk, 0,
                                i % tiles_per_chunk)),
        tok_spec, tok_spec, tok_spec,
    ]
    out_shape = [
        jax.ShapeDtypeStruct((n_batch, D_MODEL, t_len), BF16),
        jax.ShapeDtypeStruct((n_tok, D_MODEL), BF16),
        jax.ShapeDtypeStruct((n_batch, N_HEADS, t_len // kc, HEAD_W, kc), BF16),
    ] + [jax.ShapeDtypeStruct((n_tok, D_MODEL), F32)] * 3
    if emit_kv_f32:
        out_specs += [tok_spec] * 2
        out_shape += [jax.ShapeDtypeStruct((n_tok, D_MODEL), F32)] * 2
    return pl.pallas_call(
        functools.partial(_inproj_kernel, use_rope=use_rope, emit_kv_f32=emit_kv_f32),
        grid=(n_tok // tm,),
        in_specs=in_specs,
        out_specs=out_specs,
        out_shape=out_shape,
        compiler_params=pltpu.CompilerParams(
            dimension_semantics=("arbitrary",), vmem_limit_bytes=V7X_VMEM_LIMIT_BYTES),
        name="inproj_rope" if use_rope else "inproj",
    )(*args)


def _key_chunk(t_len):
    return min(512, t_len)


def _attn_kernel(*refs, has_cache, n_chunks, kc, heads_per_step, lagged_tiles, q_tiles):
    lam_ref, gsub_ref, qt_ref, k_ref, vt_ref, gatt_ref = refs[:6]
    refs = refs[6:]
    cache_refs = None
    if has_cache:
        cache_refs = refs[:2]
        refs = refs[2:]
    o_ref, acc_ref, s_ref, e_ref = refs

    lp = lam_ref[...]
    lam = (jnp.exp(jnp.sum(lp[0:1] * lp[1:2], axis=-1, keepdims=True))
           - jnp.exp(jnp.sum(lp[2:3] * lp[3:4], axis=-1, keepdims=True)) + LAM_INIT)
    g_subln = gsub_ref[...]

    if lagged_tiles is None:
        for hh in range(heads_per_step):
            lanes = slice(hh * HEAD_W, (hh + 1) * HEAD_W)
            _attn_accumulate(qt_ref.at[lanes, :], k_ref.at[:, lanes], vt_ref.at[hh], cache_refs,
                             pl.program_id(1) * heads_per_step + hh, acc_ref.at[0], s_ref, e_ref,
                             n_chunks=n_chunks, kc=kc)
            _attn_finalize(lam, g_subln, acc_ref.at[0], gatt_ref.at[:, lanes], o_ref.at[:, lanes])
        return

    step = pl.program_id(0)
    parity = step % 2
    head = (jnp.minimum(step, lagged_tiles - 1) // q_tiles) % N_HEADS

    def finalize_previous():
        _attn_finalize(lam, g_subln, acc_ref.at[1 - parity], gatt_ref, o_ref)

    @pl.when(step == 0)
    def _():
        acc_ref[1] = jnp.ones(acc_ref.shape[1:], F32)

    @pl.when(step < lagged_tiles)
    def _():
        _attn_accumulate(qt_ref, k_ref, vt_ref.at[0], cache_refs, head, acc_ref.at[parity],
                         s_ref, e_ref, n_chunks=n_chunks, kc=kc, overlapped_work=finalize_previous)

    @pl.when(step == lagged_tiles)
    def _():
        finalize_previous()


def _attn_finalize(lam, g_subln, acc_ref, gatt_ref, o_ref):
    acc1 = acc_ref[0]
    acc2 = acc_ref[1]
    o1 = acc1[:HEAD_W] * (1.0 / acc1[HEAD_W:HEAD_W + 1])
    o2 = acc2[:HEAD_W] * (1.0 / acc2[HEAD_W:HEAD_W + 1])
    o = (o1 - lam * o2).T
    ms = jnp.mean(o * o, axis=-1, keepdims=True)
    on = o * lax.rsqrt(ms + EPS) * g_subln * (1.0 - LAM_INIT)
    o_ref[...] = (on * _silu(gatt_ref[...])).astype(BF16)


def _attn_accumulate(qt_ref, k_ref, vt_ref, cache_refs, head, acc_ref, s_ref, e_ref, *, n_chunks, kc,
                     overlapped_work=None):
    has_cache = cache_refs is not None
    qt = qt_ref[...]
    tq = qt.shape[1]
    row = lax.broadcasted_iota(jnp.int32, qt.shape, 0)
    zero = jnp.zeros_like(qt)
    q_maps = (jnp.where(row < DIFF_HEAD_DIM, qt, zero), jnp.where(row >= DIFF_HEAD_DIM, qt, zero))

    acc_ref[...] = jnp.zeros(acc_ref.shape, F32)

    n_cache = 1 if has_cache else 0
    n_total = n_chunks + n_cache
    if has_cache:
        kc_ref, vc_ref = cache_refs
        head_rows = pl.ds(head, kc_ref.shape[0] // N_HEADS, stride=N_HEADS)

    def keys_of(c):
        if has_cache and isinstance(c, int) and c == 0:
            return kc_ref[head_rows, :].astype(BF16)
        start = (c - n_cache) * kc
        return k_ref[pl.ds(start if isinstance(c, int) else pl.multiple_of(start, kc), kc), :]

    def values_of(c):
        if has_cache and isinstance(c, int) and c == 0:
            return vc_ref[head_rows, :].T.astype(BF16)
        return vt_ref[c - n_cache]

    def scores(c, slot):
        k_chunk = keys_of(c)
        maxes = []
        for idx in range(2):
            st = _dot(k_chunk, q_maps[idx])
            s_ref[slot, idx, :k_chunk.shape[0], :] = st
            maxes.append(jnp.max(st, axis=0, keepdims=True))
        return tuple(maxes)

    def probs(n_keys, slot, ms, maxes):
        new_ms, alphas = [], []
        for idx in range(2):
            m_new = jnp.maximum(ms[idx], maxes[idx])
            alphas.append(jnp.exp2(ms[idx] - m_new))
            e_ref[slot, idx, :n_keys, :] = jnp.exp2(s_ref[slot, idx, :n_keys, :] - m_new).astype(BF16)
            new_ms.append(m_new)
        return tuple(new_ms), tuple(alphas)

    def values(c, slot, alphas):
        vt_chunk = values_of(c)
        n_keys = vt_chunk.shape[1]
        ones = jnp.ones((ACC_ROWS - HEAD_W, n_keys), BF16)
        vt_ext = jnp.concatenate([vt_chunk, ones], axis=0)
        for idx in range(2):
            acc_ref[idx] = acc_ref[idx] * alphas[idx] + _dot(vt_ext, e_ref[slot, idx, :n_keys, :])

    def n_keys_of(c):
        return kc_ref.shape[0] // N_HEADS if (has_cache and isinstance(c, int) and c == 0) else kc

    def step(i, slot, ms, maxes_next, alphas_cur):
        maxes_after = scores(i + 2, slot)
        ms, alphas_next = probs(n_keys_of(i + 1), 1 - slot, ms, maxes_next)
        values(i, slot, alphas_cur)
        return ms, maxes_after, alphas_next

    m0 = jnp.full((1, tq), -jnp.inf, F32)
    ms = (m0, m0)
    maxes_cur = scores(0, 0)
    if n_total == 1:
        ms, alphas_cur = probs(n_keys_of(0), 0, ms, maxes_cur)
        values(0, 0, alphas_cur)
    else:
        maxes_next = scores(1, 1)
        ms, alphas_cur = probs(n_keys_of(0), 0, ms, maxes_cur)
        if overlapped_work is not None:
            overlapped_work()
        n_steps = n_total - 2
        first_loop = n_cache
        for i in range(min(first_loop, n_steps)):
            ms, maxes_next, alphas_cur = step(i, i % 2, ms, maxes_next, alphas_cur)
        n_pairs = max(n_steps - first_loop, 0) // 2

        def pair(t, carry):
            ms, maxes_next, alphas_cur = carry
            i = first_loop + 2 * t
            ms, maxes_next, alphas_cur = step(i, first_loop % 2, ms, maxes_next, alphas_cur)
            return step(i + 1, (first_loop + 1) % 2, ms, maxes_next, alphas_cur)

        ms, maxes_next, alphas_cur = lax.fori_loop(0, n_pairs, pair, (ms, maxes_next, alphas_cur))
        for i in range(first_loop + 2 * n_pairs, n_steps):
            ms, maxes_next, alphas_cur = step(i, i % 2, ms, maxes_next, alphas_cur)
        last = n_total - 1
        ms, alphas_last = probs(n_keys_of(last), last % 2, ms, maxes_next)
        values(last - 1, (last - 1) % 2, alphas_cur)
        values(last, last % 2, alphas_last)


def _attention(lam_params, g_subln, qt, k, vt, g_att, cache_k=None, cache_v=None, *, tq=2048):
    b, t_len, _ = k.shape
    has_cache = cache_k is not None
    n_chunks, kc = vt.shape[2], vt.shape[4]
    tq = min(tq, t_len)
    heads_per_step = N_HEADS if n_chunks == 1 and not has_cache else 1
    width = heads_per_step * HEAD_W
    s_rows = max(kc, cache_k.shape[1] // N_HEADS) if has_cache else kc
    q_tiles = t_len // tq
    lagged = heads_per_step == 1
    if lagged:
        n_tiles = b * N_HEADS * q_tiles
        grid = (n_tiles + 1,)

        def item(s):
            return s // (N_HEADS * q_tiles), (s // q_tiles) % N_HEADS, s % q_tiles

        def cur(f):
            return lambda s: f(*item(jnp.minimum(s, n_tiles - 1)))

        def prev(f):
            return lambda s: f(*item(jnp.maximum(s - 1, 0)))

        const = lambda s: (0, 0)
    else:
        n_tiles = None
        grid = (b, N_HEADS // heads_per_step, q_tiles)
        cur = prev = lambda f: f
        const = lambda bi, h, qi: (0, 0)
    tok_spec = pl.BlockSpec((None, tq, width), prev(lambda bi, h, qi: (bi, qi, h)))
    in_specs = [
        pl.BlockSpec(lam_params.shape, const),
        pl.BlockSpec((1, HEAD_W), const),
        pl.BlockSpec((None, width, tq), cur(lambda bi, h, qi: (bi, h, qi))),
        pl.BlockSpec((None, t_len, width), cur(lambda bi, h, qi: (bi, 0, h))),
        pl.BlockSpec((None, heads_per_step, n_chunks, HEAD_W, kc), cur(lambda bi, h, qi: (bi, h, 0, 0, 0))),
        tok_spec,
    ]
    args = [lam_params, g_subln, qt, k, vt, g_att]
    if has_cache:
        c_spec = pl.BlockSpec((None,) + cache_k.shape[1:], cur(lambda bi, h, qi: (bi, 0, 0)))
        in_specs += [c_spec, c_spec]
        args += [cache_k, cache_v]
    return pl.pallas_call(
        functools.partial(_attn_kernel, has_cache=has_cache, n_chunks=n_chunks, kc=kc,
                          heads_per_step=heads_per_step, lagged_tiles=n_tiles, q_tiles=q_tiles),
        grid=grid,
        in_specs=in_specs,
        out_specs=tok_spec,
        out_shape=jax.ShapeDtypeStruct((b, t_len, D_ATT), BF16),
        scratch_shapes=[pltpu.VMEM((2, 2, ACC_ROWS, tq), F32),
                        pltpu.VMEM((2, 2, s_rows, tq), F32),
                        pltpu.VMEM((2, 2, s_rows, tq), BF16)],
        compiler_params=pltpu.CompilerParams(
            dimension_semantics=("arbitrary",) * len(grid), vmem_limit_bytes=V7X_VMEM_LIMIT_BYTES),
        name="diff_attn_cache" if has_cache else "diff_attn",
    )(*args)


def _lru_kernel(x_ref, g_ref, cw_ref, cb_ref, w_ref, b_ref, lam_ref, h0_ref, o_ref, st_ref, *scratch,
                blocks_per_step, **static):
    for j in range(blocks_per_step):
        lanes = slice(j * LRU_BLOCK, (j + 1) * LRU_BLOCK)
        _lru_block(x_ref.at[:, lanes], g_ref.at[:, lanes], cw_ref.at[:, lanes], cb_ref.at[:, lanes],
                   w_ref.at[j], b_ref.at[j], lam_ref.at[:, lanes], h0_ref.at[:, lanes],
                   o_ref.at[:, lanes], st_ref.at[:, lanes], *scratch, **static)


def _lru_block(x_ref, g_ref, cw_ref, cb_ref, w_ref, b_ref, lam_ref, h0_ref,
               o_ref, st_ref,
               xp, a_f, b_f, a_b, b_b, hin_f, hin_b,
               *, t_len, chunk, pitch, rows):
    g_f, p_f, g_b, p_b = b_f, a_f, b_b, a_b
    n_chunks = t_len // chunk
    chunks_per_tile = rows // chunk

    pad = jnp.zeros((SUBLANES, LRU_BLOCK), F32)
    xp[pl.ds(0, SUBLANES), :] = pad
    xp[pl.ds(SUBLANES, t_len), :] = x_ref[...]
    xp[pl.ds(SUBLANES + t_len, SUBLANES), :] = pad

    lam = lam_ref[...]
    sp = jnp.maximum(-lam, 0.0) + jnp.log1p(jnp.exp(-jnp.abs(lam)))
    half_scale = (-0.5 * LRU_C) * sp
    cw = cw_ref[...]
    cb = cb_ref[...]
    wcat = w_ref[...]
    bcat = b_ref[...]

    def gate_tile(r, carry):
        t0 = pl.multiple_of(r * rows, SUBLANES)
        u = cb
        for j in range(4):
            u = u + xp[pl.ds(t0 + SUBLANES - 1 + j, rows), :] * cw[j:j + 1]
        zh = _dot(u.astype(BF16), wcat) + bcat
        u_half = 0.5 * u
        for d, (a_s, b_s) in enumerate(((a_f, b_f), (a_b, b_b))):
            tr = jnp.tanh(zh[:, (2 * d) * LRU_BLOCK:(2 * d + 1) * LRU_BLOCK])
            ti = jnp.tanh(zh[:, (2 * d + 1) * LRU_BLOCK:(2 * d + 2) * LRU_BLOCK])
            log_a = half_scale[d:d + 1] + half_scale[d:d + 1] * tr
            a = jnp.exp(log_a)
            gain_sq = jnp.tanh(log_a) * (-1.0 - a * a)
            gain = jnp.where(gain_sq > 0.0, gain_sq * lax.rsqrt(gain_sq), 0.0)
            bb = gain * (u_half + u_half * ti)
            for cc in range(chunks_per_tile):
                dst = pl.multiple_of(r * (chunks_per_tile * pitch), SUBLANES) + cc * pitch
                a_s[pl.ds(dst, chunk), :] = a[cc * chunk:(cc + 1) * chunk]
                b_s[pl.ds(dst, chunk), :] = bb[cc * chunk:(cc + 1) * chunk]
        return carry

    lax.fori_loop(0, t_len // rows, gate_tile, 0)

    def local_step(l, carry):
        hf, pf, hb, pb = carry
        rf = pl.ds(l, n_chunks, stride=pitch)
        a = a_f[rf, :]
        hf = a * hf + b_f[rf, :]
        pf = a * pf
        g_f[rf, :] = hf
        p_f[rf, :] = pf
        rb = pl.ds(chunk - 1 - l, n_chunks, stride=pitch)
        a = a_b[rb, :]
        hb = a * hb + b_b[rb, :]
        pb = a * pb
        g_b[rb, :] = hb
        p_b[rb, :] = pb
        return hf, pf, hb, pb

    z0 = jnp.zeros((n_chunks, LRU_BLOCK), F32)
    o0 = jnp.ones((n_chunks, LRU_BLOCK), F32)
    lax.fori_loop(0, chunk, local_step, (z0, o0, z0, o0), unroll=4)

    h0 = h0_ref[...]

    def carry_step(c, carry):
        hf, hb = carry
        hin_f[pl.ds(c, 1), :] = hf
        last = c * pitch + chunk - 1
        hf = g_f[pl.ds(last, 1), :] + p_f[pl.ds(last, 1), :] * hf
        cb_ = n_chunks - 1 - c
        hin_b[pl.ds(cb_, 1), :] = hb
        first = cb_ * pitch
        hb = g_b[pl.ds(first, 1), :] + p_b[pl.ds(first, 1), :] * hb
        return hf, hb

    hf_end, hb_end = lax.fori_loop(0, n_chunks, carry_step, (h0[0:1], h0[1:2]))
    st_ref[0:1, :] = hf_end
    st_ref[1:2, :] = hb_end

    def out_pair(cp, carry):
        for par in range(2):
            c = 2 * cp + par
            src = pl.ds(pl.multiple_of(cp * (2 * pitch), SUBLANES) + par * pitch, chunk)
            hf = g_f[src, :] + p_f[src, :] * hin_f[pl.ds(c, 1), :]
            hb = g_b[src, :] + p_b[src, :] * hin_b[pl.ds(c, 1), :]
            dst = pl.ds(pl.multiple_of(c * chunk, SUBLANES), chunk)
            o_ref[dst, :] = ((hf + hb) * _silu(g_ref[dst, :])).astype(BF16)
        return carry

    lax.fori_loop(0, n_chunks // 2, out_pair, 0, unroll=2)


def _lru(x_lru, g_lru, conv_w, conv_b, w_cat, b_cat, lru_lam, h0):
    b, t_len, _ = x_lru.shape
    chunk = 64 if t_len >= 2048 else 32
    pitch = chunk + SUBLANES // 2
    n_chunks = t_len // chunk
    rows = min(512, t_len)
    blocks_per_step = N_LRU_BLOCKS if t_len <= 512 else 1
    width = blocks_per_step * LRU_BLOCK
    seq_spec = pl.BlockSpec((None, t_len, width), lambda bi, n: (bi, 0, n))
    st_spec = pl.BlockSpec((None, 2, width), lambda bi, n: (bi, 0, n))
    scan_buf = pltpu.VMEM((n_chunks * pitch, LRU_BLOCK), F32)
    return pl.pallas_call(
        functools.partial(_lru_kernel, blocks_per_step=blocks_per_step,
                          t_len=t_len, chunk=chunk, pitch=pitch, rows=rows),
        grid=(b, N_LRU_BLOCKS // blocks_per_step),
        in_specs=[
            seq_spec, seq_spec,
            pl.BlockSpec((4, width), lambda bi, n: (0, n)),
            pl.BlockSpec((1, width), lambda bi, n: (0, n)),
            pl.BlockSpec((blocks_per_step, LRU_BLOCK, 4 * LRU_BLOCK), lambda bi, n: (n, 0, 0)),
            pl.BlockSpec((blocks_per_step, 1, 4 * LRU_BLOCK), lambda bi, n: (n, 0, 0)),
            pl.BlockSpec((2, width), lambda bi, n: (0, n)),
            st_spec,
        ],
        out_specs=[seq_spec, st_spec],
        out_shape=[jax.ShapeDtypeStruct((b, t_len, D_LRU), BF16),
                   jax.ShapeDtypeStruct((b, 2, D_LRU), F32)],
        scratch_shapes=[
            pltpu.VMEM((t_len + 2 * SUBLANES, LRU_BLOCK), F32),
            scan_buf, scan_buf, scan_buf, scan_buf,
            pltpu.VMEM((n_chunks, LRU_BLOCK), F32),
            pltpu.VMEM((n_chunks, LRU_BLOCK), F32),
        ],
        compiler_params=pltpu.CompilerParams(
            dimension_semantics=("arbitrary",) * 2, vmem_limit_bytes=V7X_VMEM_LIMIT_BYTES),
        name="rglru",
    )(x_lru, g_lru, conv_w, conv_b, w_cat, b_cat, lru_lam, h0)


def _outproj_kernel(att_ref, lru_ref, x_ref, gate_ref, wa_ref, wl_ref, gpost_ref, y_ref):
    o = _dot(att_ref[...], wa_ref[...]) + _dot(lru_ref[...], wl_ref[...])
    ms = jnp.mean(o * o, axis=-1, keepdims=True)
    n = o * lax.rsqrt(ms + EPS) * gpost_ref[...]
    y_ref[...] = x_ref[...] + gate_ref[...] * n


def _outproj(att, lru, x, gate, w_att, w_lru, g_post, *, t_len, tm=512):
    n_tok = x.shape[0]
    tm = min(tm, t_len)
    tiles_per_batch = t_len // tm
    nb = gate.shape[0]
    if nb == 1:
        mod_map = lambda i: (0, 0, 0)
    else:
        mod_map = lambda i: (i // tiles_per_batch, 0, 0)
    tok_spec = pl.BlockSpec((tm, D_MODEL), lambda i: (i, 0))
    w_spec = pl.BlockSpec((D_MODEL, D_MODEL), lambda i: (0, 0))
    return pl.pallas_call(
        _outproj_kernel,
        grid=(n_tok // tm,),
        in_specs=[tok_spec, tok_spec, tok_spec,
                  pl.BlockSpec((None, 1, D_MODEL), mod_map),
                  w_spec, w_spec,
                  pl.BlockSpec((1, D_MODEL), lambda i: (0, 0))],
        out_specs=tok_spec,
        out_shape=jax.ShapeDtypeStruct((n_tok, D_MODEL), F32),
        compiler_params=pltpu.CompilerParams(
            dimension_semantics=("arbitrary",), vmem_limit_bytes=V7X_VMEM_LIMIT_BYTES),
        name="outproj",
    )(att, lru, x, gate, w_att, w_lru, g_post)


def _sublayer(x, scale, shift, gate, cache_k, cache_v, h0, use_rope, w):
    b, t_len, _ = x.shape
    xf = x.reshape(b * t_len, D_MODEL)
    emit_kv = cache_k is None
    outs = _inproj(xf, scale, shift, w["g_pre"], w["w_in"], t_len=t_len,
                   use_rope=use_rope, emit_kv_f32=emit_kv)
    qt, vt = outs[0], outs[2]
    k, g_att, x_lru, g_lru = [outs[i].reshape(b, t_len, D_MODEL) for i in (1, 3, 4, 5)]
    att = _attention(w["lam"], w["g_subln"], qt, k, vt, g_att, cache_k, cache_v)
    lru, state = _lru(x_lru, g_lru, w["conv_w"], w["conv_b"], w["w_cat"], w["b_cat"], w["lru_lam"], h0)
    y = _outproj(att.reshape(b * t_len, D_ATT), lru.reshape(b * t_len, D_LRU), xf, gate,
                 w["w_out_att"], w["w_out_lru"], w["g_post"], t_len=t_len)
    y = y.reshape(b, t_len, D_MODEL)
    if emit_kv:
        return y, outs[6], outs[7], state
    return y, None, None, state


def kernel(x_prompt, x_sample, cache_k, cache_v, state_lru, c, c_ctx, w_ada, b_ada, g_pre, w_in, lambda_q1, lambda_k1, lambda_q2, lambda_k2, g_subln, conv_w, conv_b, w_rgate, b_rgate, w_igate, b_igate, lru_lambda, w_out, g_post):
    bp, seq, _ = x_prompt.shape
    bd, kc = cache_k.shape[0], cache_k.shape[2]
    assert cache_k.shape[1] == 1 and w_in.shape[0] == 1, "single-layer step only"
    l = 0

    cond = jnp.concatenate([c, c_ctx[None, :], jnp.zeros((SUBLANES - bd - 1, D_MODEL), F32)], axis=0)
    mod = _modulation(cond, w_ada[l], b_ada[l][None, :])
    shift, scale, gate = [mod[:, i * D_MODEL:(i + 1) * D_MODEL] for i in range(3)]

    def rows(a, lo, hi):
        return a[lo:hi][:, None, :]

    w = {
        "g_pre": g_pre[l][None, :],
        "w_in": w_in[l].astype(BF16),
        "lam": jnp.stack([lambda_q1[l], lambda_k1[l], lambda_q2[l], lambda_k2[l]], axis=0),
        "g_subln": g_subln[l][None, :],
        "conv_w": conv_w[l],
        "conv_b": conv_b[l][None, :],
        "w_cat": (0.5 * jnp.concatenate([w_rgate[l, 0], w_igate[l, 0], w_rgate[l, 1], w_igate[l, 1]],
                                        axis=-1)).astype(BF16),
        "b_cat": 0.5 * jnp.concatenate(
            [bb.reshape(N_LRU_BLOCKS, 1, LRU_BLOCK)
             for bb in (b_rgate[l, 0], b_igate[l, 0], b_rgate[l, 1], b_igate[l, 1])], axis=-1),
        "lru_lam": lru_lambda[l],
        "w_out_att": w_out[l, :D_ATT].astype(BF16),
        "w_out_lru": w_out[l, D_ATT:].astype(BF16),
        "g_post": g_post[l][None, :],
    }

    y_p, new_k, new_v, st_p = _sublayer(
        x_prompt, rows(scale, bd, bd + 1), rows(shift, bd, bd + 1), rows(gate, bd, bd + 1),
        None, None, jnp.zeros((bp, 2, D_LRU), F32), False, w)

    y_s, _, _, _ = _sublayer(
        x_sample, rows(scale, 0, bd), rows(shift, 0, bd), rows(gate, 0, bd),
        cache_k.reshape(bd, kc * N_HEADS, HEAD_W), cache_v.reshape(bd, kc * N_HEADS, HEAD_W),
        state_lru.reshape(bd, 2, D_LRU), True, w)

    new_k = new_k.reshape(bp, 1, seq, N_HEADS, HEAD_W)
    new_v = new_v.reshape(bp, 1, seq, N_HEADS, HEAD_W)
    return (y_p, y_s, new_k, new_v, st_p[:, None])
```

```python
import functools
import math

import jax
import jax.numpy as jnp
import numpy as np
from jax import lax
from jax.experimental import pallas as pl
from jax.experimental.pallas import tpu as pltpu

F32 = jnp.float32
BF16 = jnp.bfloat16

D_MODEL = 1024
GRID_W = 64
N_HEADS = 8
DIFF_HEAD_DIM = 64
HEAD_W = 2 * DIFF_HEAD_DIM
D_ATT = N_HEADS * HEAD_W
D_LRU = 1024
N_LRU_BLOCKS = 8
LRU_BLOCK = D_LRU // N_LRU_BLOCKS
LRU_C = 8.0
N_GROUPS = 6
ROPE_BASE = 10000.0
EPS = 1e-6
LAM_INIT = 0.8 - 0.6 * math.exp(-0.3 * 0)
LOG2_E = math.log2(math.e)

V7X_VMEM_LIMIT_BYTES = 56 * 1024 * 1024
SUBLANES = 8
ROPE_SWAP = DIFF_HEAD_DIM // 4
BF16_ROWS_PER_VREG = 16
ACC_ROWS = HEAD_W + BF16_ROWS_PER_VREG
CACHE_CHUNK = 256


def _silu(x):
    return x * jax.nn.sigmoid(x)


def _dot(a, b):
    return jnp.dot(a, b, preferred_element_type=F32)


def _dot_nt(a, b):
    return lax.dot_general(a, b, (((1,), (1,)), ((), ())), preferred_element_type=F32)


def _mod_kernel(cond_ref, w_ref, b_ref, o_ref):
    s = _silu(cond_ref[...])
    o_ref[...] = _dot(s.astype(BF16), w_ref[...].astype(BF16)) + b_ref[...]


def _modulation(cond, w_ada, b_ada):
    n = cond.shape[0]
    return pl.pallas_call(
        _mod_kernel,
        grid=(3,),
        in_specs=[
            pl.BlockSpec((n, D_MODEL), lambda j: (0, 0)),
            pl.BlockSpec((D_MODEL, D_MODEL), lambda j: (0, j)),
            pl.BlockSpec((1, D_MODEL), lambda j: (0, j)),
        ],
        out_specs=pl.BlockSpec((n, D_MODEL), lambda j: (0, j)),
        out_shape=jax.ShapeDtypeStruct((n, 3 * D_MODEL), F32),
        name="modulation",
    )(cond, w_ada, b_ada)


def _rope_tables(t_len):
    half = DIFF_HEAD_DIM // 2
    nf = half // 2
    t = np.arange(t_len)
    row = (t // GRID_W).astype(np.float32)
    col = (t % GRID_W).astype(np.float32)
    inv = (ROPE_BASE ** (-np.arange(nf, dtype=np.float32) * 2.0 / half)).astype(np.float32)
    lane = np.arange(HEAD_W) % DIFF_HEAD_DIM
    use_row = lane < half
    freq = (lane % half) % nf
    first = (lane % half) < nf
    pos = np.where(use_row[None, :], row[:, None], col[:, None]).astype(np.float32)
    ang = (pos * inv[freq][None, :]).astype(np.float32).astype(np.float64)
    cos = np.cos(ang).astype(np.float32)
    sin = (np.sin(ang) * np.where(first, -1.0, 1.0)[None, :]).astype(np.float32)
    return jnp.asarray(cos), jnp.asarray(sin)


def _inproj_kernel(*refs, use_rope, emit_kv_f32):
    x_ref, scale_ref, shift_ref, gpre_ref, w_ref = refs[:5]
    refs = refs[5:]
    if use_rope:
        cos_ref, sin_ref = refs[:2]
        refs = refs[2:]
    q_ref, k_ref, v_ref, gatt_ref, xlru_ref, glru_ref = refs[:6]
    refs = refs[6:]
    if emit_kv_f32:
        kf_ref, vf_ref = refs

    x = x_ref[...]
    ms = jnp.mean(x * x, axis=-1, keepdims=True)
    y = x * lax.rsqrt(ms + EPS) * gpre_ref[...]
    h = (y * (1.0 + scale_ref[...]) + shift_ref[...]).astype(BF16)

    def proj(g):
        return _dot(h, w_ref[:, g * D_MODEL:(g + 1) * D_MODEL])

    if use_rope:
        cos = cos_ref[...]
        sin = sin_ref[...]
        lane = lax.broadcasted_iota(jnp.int32, cos.shape, 1)
        take_next = (lane % (2 * ROPE_SWAP)) < ROPE_SWAP

        def rope(p):
            outs = []
            for hd in range(N_HEADS):
                xh = p[:, hd * HEAD_W:(hd + 1) * HEAD_W]
                partner = jnp.where(take_next,
                                    pltpu.roll(xh, HEAD_W - ROPE_SWAP, 1),
                                    pltpu.roll(xh, ROPE_SWAP, 1))
                outs.append(xh * cos + partner * sin)
            return jnp.concatenate(outs, axis=-1)
    else:
        def rope(p):
            return p

    q = rope(proj(0)) * (LOG2_E / math.sqrt(DIFF_HEAD_DIM))
    q_ref[...] = q.T.astype(BF16)
    k = rope(proj(1))
    k_ref[...] = k.astype(BF16)
    v = proj(2)
    vt = v.T
    for hd in range(N_HEADS):
        v_ref[hd] = vt[hd * HEAD_W:(hd + 1) * HEAD_W].astype(BF16)
    if emit_kv_f32:
        kf_ref[...] = k
        vf_ref[...] = v
    gatt_ref[...] = proj(3)
    xlru_ref[...] = proj(4)
    glru_ref[...] = proj(5)


def _inproj(x, scale, shift, g_pre, w_in_bf16, *, t_len, use_rope, emit_kv_f32, tm=256):
    n_tok = x.shape[0]
    tiles_per_batch = t_len // tm
    nb = scale.shape[0]
    if nb == 1:
        mod_map = lambda i: (0, 0, 0)
    else:
        mod_map = lambda i: (i // tiles_per_batch, 0, 0)
    tok_spec = pl.BlockSpec((tm, D_MODEL), lambda i: (i, 0))
    in_specs = [
        tok_spec,
        pl.BlockSpec((None, 1, D_MODEL), mod_map),
        pl.BlockSpec((None, 1, D_MODEL), mod_map),
        pl.BlockSpec((1, D_MODEL), lambda i: (0, 0)),
        pl.BlockSpec((D_MODEL, N_GROUPS * D_MODEL), lambda i: (0, 0)),
    ]
    args = [x, scale, shift, g_pre, w_in_bf16]
    if use_rope:
        cos, sin = _rope_tables(t_len)
        rope_spec = pl.BlockSpec((tm, HEAD_W), lambda i: (i % tiles_per_batch, 0))
        in_specs += [rope_spec, rope_spec]
        args += [cos, sin]
    n_batch = n_tok // t_len
    kc = _key_chunk(t_len)
    tiles_per_chunk = kc // tm
    out_specs = [
        pl.BlockSpec((None, D_MODEL, tm), lambda i: (i // tiles_per_batch, 0, i % tiles_per_batch)),
        tok_spec,
        pl.BlockSpec((None, N_HEADS, None, HEAD_W, tm),
                     lambda i: (i // tiles_per_batch, 0, (i % tiles_per_batch) // tiles_per_chunk, 0,
                                i % tiles_per_chunk)),
        tok_spec, tok_spec, tok_spec,
    ]
    out_shape = [
        jax.ShapeDtypeStruct((n_batch, D_MODEL, t_len), BF16),
        jax.ShapeDtypeStruct((n_tok, D_MODEL), BF16),
        jax.ShapeDtypeStruct((n_batch, N_HEADS, t_len // kc, HEAD_W, kc), BF16),
    ] + [jax.ShapeDtypeStruct((n_tok, D_MODEL), F32)] * 3
    if emit_kv_f32:
        out_specs += [tok_spec] * 2
        out_shape += [jax.ShapeDtypeStruct((n_tok, D_MODEL), F32)] * 2
    return pl.pallas_call(
        functools.partial(_inproj_kernel, use_rope=use_rope, emit_kv_f32=emit_kv_f32),
        grid=(n_tok // tm,),
        in_specs=in_specs,
        out_specs=out_specs,
        out_shape=out_shape,
        compiler_params=pltpu.CompilerParams(
            dimension_semantics=("arbitrary",), vmem_limit_bytes=V7X_VMEM_LIMIT_BYTES),
        name="inproj_rope" if use_rope else "inproj",
    )(*args)


def _key_chunk(t_len):
    return min(512, t_len)


def _attn_kernel(*refs, has_cache, n_chunks, kc, heads_per_step):
    lam_ref, gsub_ref, qt_ref, k_ref, vt_ref, gatt_ref = refs[:6]
    refs = refs[6:]
    cache_refs = None
    if has_cache:
        cache_refs = refs[:2]
        refs = refs[2:]
    o_ref, acc_ref, s_ref, e_ref = refs

    lp = lam_ref[...]
    lam = (jnp.exp(jnp.sum(lp[0:1] * lp[1:2], axis=-1, keepdims=True))
           - jnp.exp(jnp.sum(lp[2:3] * lp[3:4], axis=-1, keepdims=True)) + LAM_INIT)
    g_subln = gsub_ref[...]

    for hh in range(heads_per_step):
        lanes = slice(hh * HEAD_W, (hh + 1) * HEAD_W)
        _attn_accumulate(qt_ref.at[lanes, :], k_ref.at[:, lanes], vt_ref.at[hh], cache_refs,
                         pl.program_id(1) * heads_per_step + hh, acc_ref, s_ref, e_ref,
                         n_chunks=n_chunks, kc=kc)
        _attn_finalize(lam, g_subln, acc_ref, gatt_ref.at[:, lanes], o_ref.at[:, lanes])


def _attn_finalize(lam, g_subln, acc_ref, gatt_ref, o_ref):
    acc1 = acc_ref[0]
    acc2 = acc_ref[1]
    o1 = acc1[:HEAD_W] * (1.0 / acc1[HEAD_W:HEAD_W + 1])
    o2 = acc2[:HEAD_W] * (1.0 / acc2[HEAD_W:HEAD_W + 1])
    ot = o1 - lam * o2
    ms = jnp.mean(ot * ot, axis=0, keepdims=True)
    on = (ot * lax.rsqrt(ms + EPS)).T * (g_subln * (1.0 - LAM_INIT))
    o_ref[...] = (on * _silu(gatt_ref[...])).astype(BF16)


def _attn_accumulate(qt_ref, k_ref, vt_ref, cache_refs, head, acc_ref, s_ref, e_ref, *, n_chunks, kc):
    has_cache = cache_refs is not None
    qt = qt_ref[...]
    tq = qt.shape[1]
    row = lax.broadcasted_iota(jnp.int32, qt.shape, 0)
    zero = jnp.zeros_like(qt)
    q_maps = (jnp.where(row < DIFF_HEAD_DIM, qt, zero), jnp.where(row >= DIFF_HEAD_DIM, qt, zero))

    acc_ref[...] = jnp.zeros(acc_ref.shape, F32)

    n_cache = 0
    if has_cache:
        kc_ref, vc_ref = cache_refs
        cache_len = kc_ref.shape[0] // N_HEADS
        cache_chunk = min(CACHE_CHUNK, cache_len)
        n_cache = cache_len // cache_chunk
    n_total = n_chunks + n_cache

    def is_cache(c):
        return isinstance(c, int) and c < n_cache

    def cache_rows(c):
        return pl.ds(head + c * cache_chunk * N_HEADS, cache_chunk, stride=N_HEADS)

    def keys_of(c):
        if is_cache(c):
            return kc_ref[cache_rows(c), :].astype(BF16)
        start = (c - n_cache) * kc
        return k_ref[pl.ds(start if isinstance(c, int) else pl.multiple_of(start, kc), kc), :]

    def values_of(c):
        if is_cache(c):
            return vc_ref[cache_rows(c), :].T.astype(BF16)
        return vt_ref[c - n_cache]

    def scores(c, slot):
        k_chunk = keys_of(c)
        maxes = []
        for idx in range(2):
            st = _dot(k_chunk, q_maps[idx])
            s_ref[slot, idx, :k_chunk.shape[0], :] = st
            maxes.append(jnp.max(st, axis=0, keepdims=True))
        return tuple(maxes)

    def probs(n_keys, slot, ms, maxes):
        new_ms, alphas = [], []
        for idx in range(2):
            m_new = jnp.maximum(ms[idx], maxes[idx])
            alphas.append(jnp.exp2(ms[idx] - m_new))
            e_ref[slot, idx, :n_keys, :] = jnp.exp2(s_ref[slot, idx, :n_keys, :] - m_new).astype(BF16)
            new_ms.append(m_new)
        return tuple(new_ms), tuple(alphas)

    def values(c, slot, alphas):
        vt_chunk = values_of(c)
        n_keys = vt_chunk.shape[1]
        ones = jnp.ones((ACC_ROWS - HEAD_W, n_keys), BF16)
        vt_ext = jnp.concatenate([vt_chunk, ones], axis=0)
        for idx in range(2):
            acc_ref[idx] = acc_ref[idx] * alphas[idx] + _dot(vt_ext, e_ref[slot, idx, :n_keys, :])

    def n_keys_of(c):
        return cache_chunk if is_cache(c) else kc

    def step(i, slot, ms, maxes_next, alphas_cur):
        maxes_after = scores(i + 2, slot)
        ms, alphas_next = probs(n_keys_of(i + 1), 1 - slot, ms, maxes_next)
        values(i, slot, alphas_cur)
        return ms, maxes_after, alphas_next

    m0 = jnp.full((1, tq), -jnp.inf, F32)
    ms = (m0, m0)
    maxes_cur = scores(0, 0)
    if n_total == 1:
        ms, alphas_cur = probs(n_keys_of(0), 0, ms, maxes_cur)
        values(0, 0, alphas_cur)
    else:
        maxes_next = scores(1, 1)
        ms, alphas_cur = probs(n_keys_of(0), 0, ms, maxes_cur)
        n_steps = n_total - 2
        first_loop = n_cache
        for i in range(min(first_loop, n_steps)):
            ms, maxes_next, alphas_cur = step(i, i % 2, ms, maxes_next, alphas_cur)
        n_pairs = max(n_steps - first_loop, 0) // 2

        def pair(t, carry):
            ms, maxes_next, alphas_cur = carry
            i = first_loop + 2 * t
            ms, maxes_next, alphas_cur = step(i, first_loop % 2, ms, maxes_next, alphas_cur)
            return step(i + 1, (first_loop + 1) % 2, ms, maxes_next, alphas_cur)

        ms, maxes_next, alphas_cur = lax.fori_loop(0, n_pairs, pair, (ms, maxes_next, alphas_cur))
        for i in range(first_loop + 2 * n_pairs, n_steps):
            ms, maxes_next, alphas_cur = step(i, i % 2, ms, maxes_next, alphas_cur)
        last = n_total - 1
        ms, alphas_last = probs(n_keys_of(last), last % 2, ms, maxes_next)
        values(last - 1, (last - 1) % 2, alphas_cur)
        values(last, last % 2, alphas_last)


def _attention(lam_params, g_subln, qt, k, vt, g_att, cache_k=None, cache_v=None, *, tq=2048):
    b, t_len, _ = k.shape
    has_cache = cache_k is not None
    n_chunks, kc = vt.shape[2], vt.shape[4]
    tq = min(tq, t_len)
    heads_per_step = N_HEADS if n_chunks == 1 and not has_cache else 1
    width = heads_per_step * HEAD_W
    s_rows = max(kc, cache_k.shape[1] // N_HEADS) if has_cache else kc
    tok_spec = pl.BlockSpec((None, tq, width), lambda bi, h, qi: (bi, qi, h))
    in_specs = [
        pl.BlockSpec(lam_params.shape, lambda bi, h, qi: (0, 0)),
        pl.BlockSpec((1, HEAD_W), lambda bi, h, qi: (0, 0)),
        pl.BlockSpec((None, width, tq), lambda bi, h, qi: (bi, h, qi)),
        pl.BlockSpec((None, t_len, width), lambda bi, h, qi: (bi, 0, h)),
        pl.BlockSpec((None, heads_per_step, n_chunks, HEAD_W, kc), lambda bi, h, qi: (bi, h, 0, 0, 0)),
        tok_spec,
    ]
    args = [lam_params, g_subln, qt, k, vt, g_att]
    if has_cache:
        c_spec = pl.BlockSpec((None,) + cache_k.shape[1:], lambda bi, h, qi: (bi, 0, 0))
        in_specs += [c_spec, c_spec]
        args += [cache_k, cache_v]
    return pl.pallas_call(
        functools.partial(_attn_kernel, has_cache=has_cache, n_chunks=n_chunks, kc=kc,
                          heads_per_step=heads_per_step),
        grid=(b, N_HEADS // heads_per_step, t_len // tq),
        in_specs=in_specs,
        out_specs=tok_spec,
        out_shape=jax.ShapeDtypeStruct((b, t_len, D_ATT), BF16),
        scratch_shapes=[pltpu.VMEM((2, ACC_ROWS, tq), F32),
                        pltpu.VMEM((2, 2, s_rows, tq), F32),
                        pltpu.VMEM((2, 2, s_rows, tq), BF16)],
        compiler_params=pltpu.CompilerParams(
            dimension_semantics=("arbitrary",) * 3, vmem_limit_bytes=V7X_VMEM_LIMIT_BYTES),
        name="diff_attn_cache" if has_cache else "diff_attn",
    )(*args)


def _lru_kernel(x_ref, g_ref, cw_ref, cb_ref, w_ref, b_ref, lam_ref, h0_ref, o_ref, st_ref, *scratch,
                blocks_per_step, **static):
    for j in range(blocks_per_step):
        lanes = slice(j * LRU_BLOCK, (j + 1) * LRU_BLOCK)
        _lru_block(x_ref.at[:, lanes], g_ref.at[:, lanes], cw_ref.at[:, lanes], cb_ref.at[:, lanes],
                   w_ref.at[j], b_ref.at[j], lam_ref.at[:, lanes], h0_ref.at[:, lanes],
                   o_ref.at[:, lanes], st_ref.at[:, lanes], *scratch, **static)


def _lru_block(x_ref, g_ref, cw_ref, cb_ref, w_ref, b_ref, lam_ref, h0_ref,
               o_ref, st_ref,
               xp, a_f, b_f, a_b, b_b, hin_f, hin_b,
               *, t_len, chunk, pitch, rows):
    g_f, p_f, g_b, p_b = b_f, a_f, b_b, a_b
    n_chunks = t_len // chunk
    chunks_per_tile = rows // chunk

    pad = jnp.zeros((SUBLANES, LRU_BLOCK), F32)
    xp[pl.ds(0, SUBLANES), :] = pad
    xp[pl.ds(SUBLANES, t_len), :] = x_ref[...]
    xp[pl.ds(SUBLANES + t_len, SUBLANES), :] = pad

    lam = lam_ref[...]
    sp = jnp.maximum(-lam, 0.0) + jnp.log1p(jnp.exp(-jnp.abs(lam)))
    half_scale = (-0.5 * LRU_C) * sp
    cw = cw_ref[...]
    cb = cb_ref[...]
    wcat = w_ref[...]
    bcat = b_ref[...]

    def gate_tile(r, carry):
        t0 = pl.multiple_of(r * rows, SUBLANES)
        u = cb
        for j in range(4):
            u = u + xp[pl.ds(t0 + SUBLANES - 1 + j, rows), :] * cw[j:j + 1]
        zh = _dot(u.astype(BF16), wcat) + bcat
        u_half = 0.5 * u
        for d, (a_s, b_s) in enumerate(((a_f, b_f), (a_b, b_b))):
            tr = jnp.tanh(zh[:, (2 * d) * LRU_BLOCK:(2 * d + 1) * LRU_BLOCK])
            ti = jnp.tanh(zh[:, (2 * d + 1) * LRU_BLOCK:(2 * d + 2) * LRU_BLOCK])
            log_a = half_scale[d:d + 1] + half_scale[d:d + 1] * tr
            a = jnp.exp(log_a)
            gain_sq = jnp.tanh(log_a) * (-1.0 - a * a)
            gain = jnp.where(gain_sq > 0.0, gain_sq * lax.rsqrt(gain_sq), 0.0)
            bb = gain * (u_half + u_half * ti)
            for cc in range(chunks_per_tile):
                dst = pl.multiple_of(r * (chunks_per_tile * pitch), SUBLANES) + cc * pitch
                a_s[pl.ds(dst, chunk), :] = a[cc * chunk:(cc + 1) * chunk]
                b_s[pl.ds(dst, chunk), :] = bb[cc * chunk:(cc + 1) * chunk]
        return carry

    lax.fori_loop(0, t_len // rows, gate_tile, 0)

    def local_step(l, carry):
        hf, pf, hb, pb = carry
        rf = pl.ds(l, n_chunks, stride=pitch)
        a = a_f[rf, :]
        hf = a * hf + b_f[rf, :]
        pf = a * pf
        g_f[rf, :] = hf
        p_f[rf, :] = pf
        rb = pl.ds(chunk - 1 - l, n_chunks, stride=pitch)
        a = a_b[rb, :]
        hb = a * hb + b_b[rb, :]
        pb = a * pb
        g_b[rb, :] = hb
        p_b[rb, :] = pb
        return hf, pf, hb, pb

    z0 = jnp.zeros((n_chunks, LRU_BLOCK), F32)
    o0 = jnp.ones((n_chunks, LRU_BLOCK), F32)
    lax.fori_loop(0, chunk, local_step, (z0, o0, z0, o0), unroll=4)

    h0 = h0_ref[...]

    def carry_step(c, carry):
        hf, hb = carry
        hin_f[pl.ds(c, 1), :] = hf
        last = c * pitch + chunk - 1
        hf = g_f[pl.ds(last, 1), :] + p_f[pl.ds(last, 1), :] * hf
        cb_ = n_chunks - 1 - c
        hin_b[pl.ds(cb_, 1), :] = hb
        first = cb_ * pitch
        hb = g_b[pl.ds(first, 1), :] + p_b[pl.ds(first, 1), :] * hb
        return hf, hb

    hf_end, hb_end = lax.fori_loop(0, n_chunks, carry_step, (h0[0:1], h0[1:2]))
    st_ref[0:1, :] = hf_end
    st_ref[1:2, :] = hb_end

    def out_pair(cp, carry):
        for par in range(2):
            c = 2 * cp + par
            src = pl.ds(pl.multiple_of(cp * (2 * pitch), SUBLANES) + par * pitch, chunk)
            hf = g_f[src, :] + p_f[src, :] * hin_f[pl.ds(c, 1), :]
            hb = g_b[src, :] + p_b[src, :] * hin_b[pl.ds(c, 1), :]
            dst = pl.ds(pl.multiple_of(c * chunk, SUBLANES), chunk)
            o_ref[dst, :] = ((hf + hb) * _silu(g_ref[dst, :])).astype(BF16)
        return carry

    lax.fori_loop(0, n_chunks // 2, out_pair, 0, unroll=2)


def _lru(x_lru, g_lru, conv_w, conv_b, w_cat, b_cat, lru_lam, h0):
    b, t_len, _ = x_lru.shape
    chunk = 64 if t_len >= 2048 else 32
    pitch = chunk + SUBLANES // 2
    n_chunks = t_len // chunk
    rows = min(512, t_len)
    blocks_per_step = N_LRU_BLOCKS if t_len <= 512 else 1
    width = blocks_per_step * LRU_BLOCK
    seq_spec = pl.BlockSpec((None, t_len, width), lambda bi, n: (bi, 0, n))
    st_spec = pl.BlockSpec((None, 2, width), lambda bi, n: (bi, 0, n))
    scan_buf = pltpu.VMEM((n_chunks * pitch, LRU_BLOCK), F32)
    return pl.pallas_call(
        functools.partial(_lru_kernel, blocks_per_step=blocks_per_step,
                          t_len=t_len, chunk=chunk, pitch=pitch, rows=rows),
        grid=(b, N_LRU_BLOCKS // blocks_per_step),
        in_specs=[
            seq_spec, seq_spec,
            pl.BlockSpec((4, width), lambda bi, n: (0, n)),
            pl.BlockSpec((1, width), lambda bi, n: (0, n)),
            pl.BlockSpec((blocks_per_step, LRU_BLOCK, 4 * LRU_BLOCK), lambda bi, n: (n, 0, 0)),
            pl.BlockSpec((blocks_per_step, 1, 4 * LRU_BLOCK), lambda bi, n: (n, 0, 0)),
            pl.BlockSpec((2, width), lambda bi, n: (0, n)),
            st_spec,
        ],
        out_specs=[seq_spec, st_spec],
        out_shape=[jax.ShapeDtypeStruct((b, t_len, D_LRU), BF16),
                   jax.ShapeDtypeStruct((b, 2, D_LRU), F32)],
        scratch_shapes=[
            pltpu.VMEM((t_len + 2 * SUBLANES, LRU_BLOCK), F32),
            scan_buf, scan_buf, scan_buf, scan_buf,
            pltpu.VMEM((n_chunks, LRU_BLOCK), F32),
            pltpu.VMEM((n_chunks, LRU_BLOCK), F32),
        ],
        compiler_params=pltpu.CompilerParams(
            dimension_semantics=("arbitrary",) * 2, vmem_limit_bytes=V7X_VMEM_LIMIT_BYTES),
        name="rglru",
    )(x_lru, g_lru, conv_w, conv_b, w_cat, b_cat, lru_lam, h0)


def _outproj_kernel(att_ref, lru_ref, x_ref, gate_ref, wa_ref, wl_ref, gpost_ref, y_ref):
    o = _dot(att_ref[...], wa_ref[...]) + _dot(lru_ref[...], wl_ref[...])
    ms = jnp.mean(o * o, axis=-1, keepdims=True)
    n = o * lax.rsqrt(ms + EPS) * gpost_ref[...]
    y_ref[...] = x_ref[...] + gate_ref[...] * n


def _outproj(att, lru, x, gate, w_att, w_lru, g_post, *, t_len, tm=512):
    n_tok = x.shape[0]
    tm = min(tm, t_len)
    tiles_per_batch = t_len // tm
    nb = gate.shape[0]
    if nb == 1:
        mod_map = lambda i: (0, 0, 0)
    else:
        mod_map = lambda i: (i // tiles_per_batch, 0, 0)
    tok_spec = pl.BlockSpec((tm, D_MODEL), lambda i: (i, 0))
    w_spec = pl.BlockSpec((D_MODEL, D_MODEL), lambda i: (0, 0))
    return pl.pallas_call(
        _outproj_kernel,
        grid=(n_tok // tm,),
        in_specs=[tok_spec, tok_spec, tok_spec,
                  pl.BlockSpec((None, 1, D_MODEL), mod_map),
                  w_spec, w_spec,
                  pl.BlockSpec((1, D_MODEL), lambda i: (0, 0))],
        out_specs=tok_spec,
        out_shape=jax.ShapeDtypeStruct((n_tok, D_MODEL), F32),
        compiler_params=pltpu.CompilerParams(
            dimension_semantics=("arbitrary",), vmem_limit_bytes=V7X_VMEM_LIMIT_BYTES),
        name="outproj",
    )(att, lru, x, gate, w_att, w_lru, g_post)


def _sublayer(x, scale, shift, gate, cache_k, cache_v, h0, use_rope, w):
    b, t_len, _ = x.shape
    xf = x.reshape(b * t_len, D_MODEL)
    emit_kv = cache_k is None
    outs = _inproj(xf, scale, shift, w["g_pre"], w["w_in"], t_len=t_len,
                   use_rope=use_rope, emit_kv_f32=emit_kv)
    qt, vt = outs[0], outs[2]
    k, g_att, x_lru, g_lru = [outs[i].reshape(b, t_len, D_MODEL) for i in (1, 3, 4, 5)]
    att = _attention(w["lam"], w["g_subln"], qt, k, vt, g_att, cache_k, cache_v)
    lru, state = _lru(x_lru, g_lru, w["conv_w"], w["conv_b"], w["w_cat"], w["b_cat"], w["lru_lam"], h0)
    y = _outproj(att.reshape(b * t_len, D_ATT), lru.reshape(b * t_len, D_LRU), xf, gate,
                 w["w_out_att"], w["w_out_lru"], w["g_post"], t_len=t_len)
    y = y.reshape(b, t_len, D_MODEL)
    if emit_kv:
        return y, outs[6], outs[7], state
    return y, None, None, state


def kernel(x_prompt, x_sample, cache_k, cache_v, state_lru, c, c_ctx, w_ada, b_ada, g_pre, w_in, lambda_q1, lambda_k1, lambda_q2, lambda_k2, g_subln, conv_w, conv_b, w_rgate, b_rgate, w_igate, b_igate, lru_lambda, w_out, g_post):
    bp, seq, _ = x_prompt.shape
    bd, kc = cache_k.shape[0], cache_k.shape[2]
    assert cache_k.shape[1] == 1 and w_in.shape[0] == 1, "single-layer step only"
    l = 0

    cond = jnp.concatenate([c, c_ctx[None, :], jnp.zeros((SUBLANES - bd - 1, D_MODEL), F32)], axis=0)
    mod = _modulation(cond, w_ada[l], b_ada[l][None, :])
    shift, scale, gate = [mod[:, i * D_MODEL:(i + 1) * D_MODEL] for i in range(3)]

    def rows(a, lo, hi):
        return a[lo:hi][:, None, :]

    w = {
        "g_pre": g_pre[l][None, :],
        "w_in": w_in[l].astype(BF16),
        "lam": jnp.stack([lambda_q1[l], lambda_k1[l], lambda_q2[l], lambda_k2[l]], axis=0),
        "g_subln": g_subln[l][None, :],
        "conv_w": conv_w[l],
        "conv_b": conv_b[l][None, :],
        "w_cat": (0.5 * jnp.concatenate([w_rgate[l, 0], w_igate[l, 0], w_rgate[l, 1], w_igate[l, 1]],
                                        axis=-1)).astype(BF16),
        "b_cat": 0.5 * jnp.concatenate(
            [bb.reshape(N_LRU_BLOCKS, 1, LRU_BLOCK)
             for bb in (b_rgate[l, 0], b_igate[l, 0], b_rgate[l, 1], b_igate[l, 1])], axis=-1),
        "lru_lam": lru_lambda[l],
        "w_out_att": w_out[l, :D_ATT].astype(BF16),
        "w_out_lru": w_out[l, D_ATT:].astype(BF16),
        "g_post": g_post[l][None, :],
    }

    y_p, new_k, new_v, st_p = _sublayer(
        x_prompt, rows(scale, bd, bd + 1), rows(shift, bd, bd + 1), rows(gate, bd, bd + 1),
        None, None, jnp.zeros((bp, 2, D_LRU), F32), False, w)

    y_s, _, _, _ = _sublayer(
        x_sample, rows(scale, 0, bd), rows(shift, 0, bd), rows(gate, 0, bd),
        cache_k.reshape(bd, kc * N_HEADS, HEAD_W), cache_v.reshape(bd, kc * N_HEADS, HEAD_W),
        state_lru.reshape(bd, 2, D_LRU), True, w)

    new_k = new_k.reshape(bp, 1, seq, N_HEADS, HEAD_W)
    new_v = new_v.reshape(bp, 1, seq, N_HEADS, HEAD_W)
    return (y_p, y_s, new_k, new_v, st_p[:, None])
```

```python
import functools
import math

import jax
import jax.numpy as jnp
import numpy as np
from jax import lax
from jax.experimental import pallas as pl
from jax.experimental.pallas import tpu as pltpu

F32 = jnp.float32
BF16 = jnp.bfloat16

D_MODEL = 1024
GRID_W = 64
N_HEADS = 8
DIFF_HEAD_DIM = 64
HEAD_W = 2 * DIFF_HEAD_DIM
D_ATT = N_HEADS * HEAD_W
D_LRU = 1024
N_LRU_BLOCKS = 8
LRU_BLOCK = D_LRU // N_LRU_BLOCKS
LRU_C = 8.0
N_GROUPS = 6
ROPE_BASE = 10000.0
EPS = 1e-6
LAM_INIT = 0.8 - 0.6 * math.exp(-0.3 * 0)
LOG2_E = math.log2(math.e)

V7X_VMEM_LIMIT_BYTES = 56 * 1024 * 1024
SUBLANES = 8
ROPE_SWAP = DIFF_HEAD_DIM // 4
BF16_ROWS_PER_VREG = 16
ACC_ROWS = HEAD_W + BF16_ROWS_PER_VREG
CACHE_CHUNK = 256
BIAS_SPLIT = 3


def _silu(x):
    return x * jax.nn.sigmoid(x)


def _dot(a, b):
    return jnp.dot(a, b, preferred_element_type=F32)


def _dot_nt(a, b):
    return lax.dot_general(a, b, (((1,), (1,)), ((), ())), preferred_element_type=F32)


def _mod_kernel(cond_ref, w_ref, b_ref, o_ref):
    s = _silu(cond_ref[...])
    o_ref[...] = _dot(s.astype(BF16), w_ref[...].astype(BF16)) + b_ref[...]


def _modulation(cond, w_ada, b_ada):
    n = cond.shape[0]
    return pl.pallas_call(
        _mod_kernel,
        grid=(3,),
        in_specs=[
            pl.BlockSpec((n, D_MODEL), lambda j: (0, 0)),
            pl.BlockSpec((D_MODEL, D_MODEL), lambda j: (0, j)),
            pl.BlockSpec((1, D_MODEL), lambda j: (0, j)),
        ],
        out_specs=pl.BlockSpec((n, D_MODEL), lambda j: (0, j)),
        out_shape=jax.ShapeDtypeStruct((n, 3 * D_MODEL), F32),
        name="modulation",
    )(cond, w_ada, b_ada)


def _rope_tables(t_len):
    half = DIFF_HEAD_DIM // 2
    nf = half // 2
    t = np.arange(t_len)
    row = (t // GRID_W).astype(np.float32)
    col = (t % GRID_W).astype(np.float32)
    inv = (ROPE_BASE ** (-np.arange(nf, dtype=np.float32) * 2.0 / half)).astype(np.float32)
    lane = np.arange(HEAD_W) % DIFF_HEAD_DIM
    use_row = lane < half
    freq = (lane % half) % nf
    first = (lane % half) < nf
    pos = np.where(use_row[None, :], row[:, None], col[:, None]).astype(np.float32)
    ang = (pos * inv[freq][None, :]).astype(np.float32).astype(np.float64)
    cos = np.cos(ang).astype(np.float32)
    sin = (np.sin(ang) * np.where(first, -1.0, 1.0)[None, :]).astype(np.float32)
    return jnp.asarray(cos), jnp.asarray(sin)


def _inproj_kernel(*refs, use_rope, emit_kv_f32):
    x_ref, scale_ref, shift_ref, gpre_ref, w_ref = refs[:5]
    refs = refs[5:]
    if use_rope:
        cos_ref, sin_ref = refs[:2]
        refs = refs[2:]
    q_ref, k_ref, v_ref, gatt_ref, xlru_ref, glru_ref = refs[:6]
    refs = refs[6:]
    if emit_kv_f32:
        kf_ref, vf_ref = refs

    x = x_ref[...]
    ms = jnp.mean(x * x, axis=-1, keepdims=True)
    y = x * lax.rsqrt(ms + EPS) * gpre_ref[...]
    h = (y * (1.0 + scale_ref[...]) + shift_ref[...]).astype(BF16)

    def proj(g):
        return _dot(h, w_ref[:, g * D_MODEL:(g + 1) * D_MODEL])

    if use_rope:
        cos = cos_ref[...]
        sin = sin_ref[...]
        lane = lax.broadcasted_iota(jnp.int32, cos.shape, 1)
        take_next = (lane % (2 * ROPE_SWAP)) < ROPE_SWAP

        def rope(p):
            outs = []
            for hd in range(N_HEADS):
                xh = p[:, hd * HEAD_W:(hd + 1) * HEAD_W]
                partner = jnp.where(take_next,
                                    pltpu.roll(xh, HEAD_W - ROPE_SWAP, 1),
                                    pltpu.roll(xh, ROPE_SWAP, 1))
                outs.append(xh * cos + partner * sin)
            return jnp.concatenate(outs, axis=-1)
    else:
        def rope(p):
            return p

    q = rope(proj(0)) * (LOG2_E / math.sqrt(DIFF_HEAD_DIM))
    q_ref[...] = q.T.astype(BF16)
    k = rope(proj(1))
    k_ref[...] = k.astype(BF16)
    v = proj(2)
    vt = v.T
    for hd in range(N_HEADS):
        v_ref[hd] = vt[hd * HEAD_W:(hd + 1) * HEAD_W].astype(BF16)
    if emit_kv_f32:
        kf_ref[...] = k
        vf_ref[...] = v
    gatt_ref[...] = proj(3)
    xlru_ref[...] = proj(4)
    glru_ref[...] = proj(5)


def _inproj(x, scale, shift, g_pre, w_in_bf16, *, t_len, use_rope, emit_kv_f32, tm=256):
    n_tok = x.shape[0]
    tiles_per_batch = t_len // tm
    nb = scale.shape[0]
    if nb == 1:
        mod_map = lambda i: (0, 0, 0)
    else:
        mod_map = lambda i: (i // tiles_per_batch, 0, 0)
    tok_spec = pl.BlockSpec((tm, D_MODEL), lambda i: (i, 0))
    in_specs = [
        tok_spec,
        pl.BlockSpec((None, 1, D_MODEL), mod_map),
        pl.BlockSpec((None, 1, D_MODEL), mod_map),
        pl.BlockSpec((1, D_MODEL), lambda i: (0, 0)),
        pl.BlockSpec((D_MODEL, N_GROUPS * D_MODEL), lambda i: (0, 0)),
    ]
    args = [x, scale, shift, g_pre, w_in_bf16]
    if use_rope:
        cos, sin = _rope_tables(t_len)
        rope_spec = pl.BlockSpec((tm, HEAD_W), lambda i: (i % tiles_per_batch, 0))
        in_specs += [rope_spec, rope_spec]
        args += [cos, sin]
    n_batch = n_tok // t_len
    kc = _key_chunk(t_len)
    tiles_per_chunk = kc // tm
    out_specs = [
        pl.BlockSpec((None, D_MODEL, tm), lambda i: (i // tiles_per_batch, 0, i % tiles_per_batch)),
        tok_spec,
        pl.BlockSpec((None, N_HEADS, None, HEAD_W, tm),
                     lambda i: (i // tiles_per_batch, 0, (i % tiles_per_batch) // tiles_per_chunk, 0,
                                i % tiles_per_chunk)),
        tok_spec, tok_spec, tok_spec,
    ]
    out_shape = [
        jax.ShapeDtypeStruct((n_batch, D_MODEL, t_len), BF16),
        jax.ShapeDtypeStruct((n_tok, D_MODEL), BF16),
        jax.ShapeDtypeStruct((n_batch, N_HEADS, t_len // kc, HEAD_W, kc), BF16),
    ] + [jax.ShapeDtypeStruct((n_tok, D_MODEL), F32)] * 3
    if emit_kv_f32:
        out_specs += [tok_spec] * 2
        out_shape += [jax.ShapeDtypeStruct((n_tok, D_MODEL), F32)] * 2
    return pl.pallas_call(
        functools.partial(_inproj_kernel, use_rope=use_rope, emit_kv_f32=emit_kv_f32),
        grid=(n_tok // tm,),
        in_specs=in_specs,
        out_specs=out_specs,
        out_shape=out_shape,
        compiler_params=pltpu.CompilerParams(
            dimension_semantics=("arbitrary",), vmem_limit_bytes=V7X_VMEM_LIMIT_BYTES),
        name="inproj_rope" if use_rope else "inproj",
    )(*args)


def _key_chunk(t_len):
    return min(512, t_len)


def _attn_kernel(*refs, has_cache, n_chunks, kc, heads_per_step):
    lam_ref, gsub_ref, qt_ref, k_ref, vt_ref, gatt_ref = refs[:6]
    refs = refs[6:]
    cache_refs = None
    if has_cache:
        cache_refs = refs[:2]
        refs = refs[2:]
    o_ref, acc_ref, s_ref, e_ref = refs

    lp = lam_ref[...]
    lam = (jnp.exp(jnp.sum(lp[0:1] * lp[1:2], axis=-1, keepdims=True))
           - jnp.exp(jnp.sum(lp[2:3] * lp[3:4], axis=-1, keepdims=True)) + LAM_INIT)
    g_subln = gsub_ref[...]

    for hh in range(heads_per_step):
        lanes = slice(hh * HEAD_W, (hh + 1) * HEAD_W)
        _attn_accumulate(qt_ref.at[lanes, :], k_ref.at[:, lanes], vt_ref.at[hh], cache_refs,
                         pl.program_id(1) * heads_per_step + hh, acc_ref, s_ref, e_ref,
                         n_chunks=n_chunks, kc=kc)
        _attn_finalize(lam, g_subln, acc_ref, gatt_ref.at[:, lanes], o_ref.at[:, lanes])


def _attn_finalize(lam, g_subln, acc_ref, gatt_ref, o_ref):
    acc1 = acc_ref[0]
    acc2 = acc_ref[1]
    o1 = acc1[:HEAD_W] * (1.0 / acc1[HEAD_W:HEAD_W + 1])
    o2 = acc2[:HEAD_W] * (1.0 / acc2[HEAD_W:HEAD_W + 1])
    ot = o1 - lam * o2
    ms = jnp.mean(ot * ot, axis=0, keepdims=True)
    on = (ot * lax.rsqrt(ms + EPS)).T * (g_subln * (1.0 - LAM_INIT))
    o_ref[...] = (on * _silu(gatt_ref[...])).astype(BF16)


def _attn_accumulate(qt_ref, k_ref, vt_ref, cache_refs, head, acc_ref, s_ref, e_ref, *, n_chunks, kc):
    has_cache = cache_refs is not None
    qt = qt_ref[...]
    tq = qt.shape[1]
    row = lax.broadcasted_iota(jnp.int32, qt.shape, 0)
    zero = jnp.zeros_like(qt)
    q_maps = (jnp.where(row < DIFF_HEAD_DIM, qt, zero), jnp.where(row >= DIFF_HEAD_DIM, qt, zero))

    acc_ref[...] = jnp.zeros(acc_ref.shape, F32)

    n_cache = 0
    if has_cache:
        kc_ref, vc_ref = cache_refs
        cache_len = kc_ref.shape[0] // N_HEADS
        cache_chunk = min(CACHE_CHUNK, cache_len)
        n_cache = cache_len // cache_chunk
    n_total = n_chunks + n_cache

    def is_cache(c):
        return isinstance(c, int) and c < n_cache

    def cache_rows(c):
        return pl.ds(head + c * cache_chunk * N_HEADS, cache_chunk, stride=N_HEADS)

    def keys_of(c):
        if is_cache(c):
            return kc_ref[cache_rows(c), :].astype(BF16)
        start = (c - n_cache) * kc
        return k_ref[pl.ds(start if isinstance(c, int) else pl.multiple_of(start, kc), kc), :]

    def values_of(c):
        if is_cache(c):
            return vc_ref[cache_rows(c), :].T.astype(BF16)
        return vt_ref[c - n_cache]

    def scores(c, slot):
        k_chunk = keys_of(c)
        maxes = []
        for idx in range(2):
            st = _dot(k_chunk, q_maps[idx])
            s_ref[slot, idx, :k_chunk.shape[0], :] = st
            maxes.append(jnp.max(st, axis=0, keepdims=True))
        return tuple(maxes)

    def probs(n_keys, slot, ms, maxes):
        new_ms, alphas = [], []
        for idx in range(2):
            m_new = jnp.maximum(ms[idx], maxes[idx])
            alphas.append(jnp.exp2(ms[idx] - m_new))
            e_ref[slot, idx, :n_keys, :] = jnp.exp2(s_ref[slot, idx, :n_keys, :] - m_new).astype(BF16)
            new_ms.append(m_new)
        return tuple(new_ms), tuple(alphas)

    def values(c, slot, alphas):
        vt_chunk = values_of(c)
        n_keys = vt_chunk.shape[1]
        ones = jnp.ones((ACC_ROWS - HEAD_W, n_keys), BF16)
        vt_ext = jnp.concatenate([vt_chunk, ones], axis=0)
        for idx in range(2):
            acc_ref[idx] = acc_ref[idx] * alphas[idx] + _dot(vt_ext, e_ref[slot, idx, :n_keys, :])

    def n_keys_of(c):
        return cache_chunk if is_cache(c) else kc

    def step(i, slot, ms, maxes_next, alphas_cur):
        maxes_after = scores(i + 2, slot)
        ms, alphas_next = probs(n_keys_of(i + 1), 1 - slot, ms, maxes_next)
        values(i, slot, alphas_cur)
        return ms, maxes_after, alphas_next

    m0 = jnp.full((1, tq), -jnp.inf, F32)
    ms = (m0, m0)
    maxes_cur = scores(0, 0)
    if n_total == 1:
        ms, alphas_cur = probs(n_keys_of(0), 0, ms, maxes_cur)
        values(0, 0, alphas_cur)
    else:
        maxes_next = scores(1, 1)
        ms, alphas_cur = probs(n_keys_of(0), 0, ms, maxes_cur)
        n_steps = n_total - 2
        first_loop = n_cache
        for i in range(min(first_loop, n_steps)):
            ms, maxes_next, alphas_cur = step(i, i % 2, ms, maxes_next, alphas_cur)
        n_pairs = max(n_steps - first_loop, 0) // 2

        def pair(t, carry):
            ms, maxes_next, alphas_cur = carry
            i = first_loop + 2 * t
            ms, maxes_next, alphas_cur = step(i, first_loop % 2, ms, maxes_next, alphas_cur)
            return step(i + 1, (first_loop + 1) % 2, ms, maxes_next, alphas_cur)

        ms, maxes_next, alphas_cur = lax.fori_loop(0, n_pairs, pair, (ms, maxes_next, alphas_cur))
        for i in range(first_loop + 2 * n_pairs, n_steps):
            ms, maxes_next, alphas_cur = step(i, i % 2, ms, maxes_next, alphas_cur)
        last = n_total - 1
        ms, alphas_last = probs(n_keys_of(last), last % 2, ms, maxes_next)
        values(last - 1, (last - 1) % 2, alphas_cur)
        values(last, last % 2, alphas_last)


def _attention(lam_params, g_subln, qt, k, vt, g_att, cache_k=None, cache_v=None, *, tq=2048):
    b, t_len, _ = k.shape
    has_cache = cache_k is not None
    n_chunks, kc = vt.shape[2], vt.shape[4]
    tq = min(tq, t_len)
    heads_per_step = N_HEADS if n_chunks == 1 and not has_cache else 1
    width = heads_per_step * HEAD_W
    s_rows = max(kc, cache_k.shape[1] // N_HEADS) if has_cache else kc
    tok_spec = pl.BlockSpec((None, tq, width), lambda bi, h, qi: (bi, qi, h))
    in_specs = [
        pl.BlockSpec(lam_params.shape, lambda bi, h, qi: (0, 0)),
        pl.BlockSpec((1, HEAD_W), lambda bi, h, qi: (0, 0)),
        pl.BlockSpec((None, width, tq), lambda bi, h, qi: (bi, h, qi)),
        pl.BlockSpec((None, t_len, width), lambda bi, h, qi: (bi, 0, h)),
        pl.BlockSpec((None, heads_per_step, n_chunks, HEAD_W, kc), lambda bi, h, qi: (bi, h, 0, 0, 0)),
        tok_spec,
    ]
    args = [lam_params, g_subln, qt, k, vt, g_att]
    if has_cache:
        c_spec = pl.BlockSpec((None,) + cache_k.shape[1:], lambda bi, h, qi: (bi, 0, 0))
        in_specs += [c_spec, c_spec]
        args += [cache_k, cache_v]
    return pl.pallas_call(
        functools.partial(_attn_kernel, has_cache=has_cache, n_chunks=n_chunks, kc=kc,
                          heads_per_step=heads_per_step),
        grid=(b, N_HEADS // heads_per_step, t_len // tq),
        in_specs=in_specs,
        out_specs=tok_spec,
        out_shape=jax.ShapeDtypeStruct((b, t_len, D_ATT), BF16),
        scratch_shapes=[pltpu.VMEM((2, ACC_ROWS, tq), F32),
                        pltpu.VMEM((2, 2, s_rows, tq), F32),
                        pltpu.VMEM((2, 2, s_rows, tq), BF16)],
        compiler_params=pltpu.CompilerParams(
            dimension_semantics=("arbitrary",) * 3, vmem_limit_bytes=V7X_VMEM_LIMIT_BYTES),
        name="diff_attn_cache" if has_cache else "diff_attn",
    )(*args)


def _lru_kernel(x_ref, g_ref, cw_ref, cb_ref, w_ref, lam_ref, h0_ref, o_ref, st_ref, *scratch,
                blocks_per_step, **static):
    for j in range(blocks_per_step):
        lanes = slice(j * LRU_BLOCK, (j + 1) * LRU_BLOCK)
        _lru_block(x_ref.at[:, lanes], g_ref.at[:, lanes], cw_ref.at[:, lanes], cb_ref.at[:, lanes],
                   w_ref.at[j], lam_ref.at[:, lanes], h0_ref.at[:, lanes],
                   o_ref.at[:, lanes], st_ref.at[:, lanes], *scratch, **static)


def _lru_block(x_ref, g_ref, cw_ref, cb_ref, w_ref, lam_ref, h0_ref,
               o_ref, st_ref,
               xp, a_f, b_f, a_b, b_b, hin_f, hin_b,
               *, t_len, chunk, pitch, rows):
    g_f, p_f, g_b, p_b = b_f, a_f, b_b, a_b
    n_chunks = t_len // chunk
    chunks_per_tile = rows // chunk

    pad = jnp.zeros((SUBLANES, LRU_BLOCK), F32)
    xp[pl.ds(0, SUBLANES), :] = pad
    xp[pl.ds(SUBLANES, t_len), :] = x_ref[...]
    xp[pl.ds(SUBLANES + t_len, SUBLANES), :] = pad

    lam = lam_ref[...]
    sp = jnp.maximum(-lam, 0.0) + jnp.log1p(jnp.exp(-jnp.abs(lam)))
    half_scale = (-0.5 * LRU_C) * sp
    cw = cw_ref[...]
    cb = cb_ref[...]
    w_ext = w_ref[...]
    lane = lax.broadcasted_iota(jnp.int32, (rows, LRU_BLOCK), 1)
    bias_taps = jnp.where(lane < BIAS_SPLIT, 1.0, 0.0).astype(BF16)

    def gate_tile(r, carry):
        t0 = pl.multiple_of(r * rows, SUBLANES)
        u = cb
        for j in range(4):
            u = u + xp[pl.ds(t0 + SUBLANES - 1 + j, rows), :] * cw[j:j + 1]
        zh = _dot(jnp.concatenate([u.astype(BF16), bias_taps], axis=1), w_ext)
        u_half = 0.5 * u
        for d, (a_s, b_s) in enumerate(((a_f, b_f), (a_b, b_b))):
            tr = jnp.tanh(zh[:, (2 * d) * LRU_BLOCK:(2 * d + 1) * LRU_BLOCK])
            ti = jnp.tanh(zh[:, (2 * d + 1) * LRU_BLOCK:(2 * d + 2) * LRU_BLOCK])
            log_a = half_scale[d:d + 1] + half_scale[d:d + 1] * tr
            a = jnp.exp(log_a)
            gain_sq = jnp.tanh(log_a) * (-1.0 - a * a)
            gain = jnp.where(gain_sq > 0.0, gain_sq * lax.rsqrt(gain_sq), 0.0)
            bb = gain * (u_half + u_half * ti)
            for cc in range(chunks_per_tile):
                dst = pl.multiple_of(r * (chunks_per_tile * pitch), SUBLANES) + cc * pitch
                a_s[pl.ds(dst, chunk), :] = a[cc * chunk:(cc + 1) * chunk]
                b_s[pl.ds(dst, chunk), :] = bb[cc * chunk:(cc + 1) * chunk]
        return carry

    lax.fori_loop(0, t_len // rows, gate_tile, 0)

    def local_step(l, carry):
        hf, pf, hb, pb = carry
        rf = pl.ds(l, n_chunks, stride=pitch)
        a = a_f[rf, :]
        hf = a * hf + b_f[rf, :]
        pf = a * pf
        g_f[rf, :] = hf
        p_f[rf, :] = pf
        rb = pl.ds(chunk - 1 - l, n_chunks, stride=pitch)
        a = a_b[rb, :]
        hb = a * hb + b_b[rb, :]
        pb = a * pb
        g_b[rb, :] = hb
        p_b[rb, :] = pb
        return hf, pf, hb, pb

    z0 = jnp.zeros((n_chunks, LRU_BLOCK), F32)
    o0 = jnp.ones((n_chunks, LRU_BLOCK), F32)
    end_f, decay_f, end_b, decay_b = lax.fori_loop(0, chunk, local_step, (z0, o0, z0, o0), unroll=4)

    h0 = h0_ref[...]
    chunk_id = lax.broadcasted_iota(jnp.int32, (n_chunks, LRU_BLOCK), 0)

    def compose(end, decay, towards_higher):
        shift = 1
        while shift < n_chunks:
            if towards_higher:
                valid = chunk_id >= shift
                amount = shift
            else:
                valid = chunk_id < n_chunks - shift
                amount = n_chunks - shift
            prev_end = jnp.where(valid, pltpu.roll(end, amount, 0), 0.0)
            prev_decay = jnp.where(valid, pltpu.roll(decay, amount, 0), 1.0)
            end = end + decay * prev_end
            decay = decay * prev_decay
            shift *= 2
        return end, decay

    end_f, decay_f = compose(end_f, decay_f, True)
    after_f = end_f + decay_f * h0[0:1]
    hin_f[...] = jnp.where(chunk_id == 0, h0[0:1], pltpu.roll(after_f, 1, 0))
    st_ref[0:1, :] = after_f[n_chunks - 1:n_chunks]
    end_b, decay_b = compose(end_b, decay_b, False)
    after_b = end_b + decay_b * h0[1:2]
    hin_b[...] = jnp.where(chunk_id == n_chunks - 1, h0[1:2], pltpu.roll(after_b, n_chunks - 1, 0))
    st_ref[1:2, :] = after_b[0:1]

    def out_pair(cp, carry):
        for par in range(2):
            c = 2 * cp + par
            src = pl.ds(pl.multiple_of(cp * (2 * pitch), SUBLANES) + par * pitch, chunk)
            hf = g_f[src, :] + p_f[src, :] * hin_f[pl.ds(c, 1), :]
            hb = g_b[src, :] + p_b[src, :] * hin_b[pl.ds(c, 1), :]
            dst = pl.ds(pl.multiple_of(c * chunk, SUBLANES), chunk)
            o_ref[dst, :] = ((hf + hb) * _silu(g_ref[dst, :])).astype(BF16)
        return carry

    lax.fori_loop(0, n_chunks // 2, out_pair, 0, unroll=2)


def _lru_gate_operand(w_r, w_i, b_r, b_i):
    weights = 0.5 * jnp.concatenate([w_r[0], w_i[0], w_r[1], w_i[1]], axis=-1)
    bias = 0.5 * jnp.concatenate(
        [bb.reshape(N_LRU_BLOCKS, 1, LRU_BLOCK) for bb in (b_r[0], b_i[0], b_r[1], b_i[1])], axis=-1)
    parts, rest = [], bias
    for _ in range(BIAS_SPLIT):
        gamma = rest * (2.0 ** 16 + 1.0)
        part = gamma - (gamma - rest)
        parts.append(part.astype(BF16))
        rest = rest - part
    zeros = jnp.zeros((N_LRU_BLOCKS, LRU_BLOCK - BIAS_SPLIT, 4 * LRU_BLOCK), BF16)
    return jnp.concatenate([weights.astype(BF16)] + parts + [zeros], axis=1)


def _lru(x_lru, g_lru, conv_w, conv_b, w_ext, lru_lam, h0):
    b, t_len, _ = x_lru.shape
    chunk = 64 if t_len >= 2048 else 32
    pitch = chunk + SUBLANES // 2
    n_chunks = t_len // chunk
    rows = min(2048, t_len)
    blocks_per_step = N_LRU_BLOCKS if t_len <= 512 else 1
    width = blocks_per_step * LRU_BLOCK
    seq_spec = pl.BlockSpec((None, t_len, width), lambda bi, n: (bi, 0, n))
    st_spec = pl.BlockSpec((None, 2, width), lambda bi, n: (bi, 0, n))
    scan_buf = pltpu.VMEM((n_chunks * pitch, LRU_BLOCK), F32)
    return pl.pallas_call(
        functools.partial(_lru_kernel, blocks_per_step=blocks_per_step,
                          t_len=t_len, chunk=chunk, pitch=pitch, rows=rows),
        grid=(b, N_LRU_BLOCKS // blocks_per_step),
        in_specs=[
            seq_spec, seq_spec,
            pl.BlockSpec((4, width), lambda bi, n: (0, n)),
            pl.BlockSpec((1, width), lambda bi, n: (0, n)),
            pl.BlockSpec((blocks_per_step, 2 * LRU_BLOCK, 4 * LRU_BLOCK), lambda bi, n: (n, 0, 0)),
            pl.BlockSpec((2, width), lambda bi, n: (0, n)),
            st_spec,
        ],
        out_specs=[seq_spec, st_spec],
        out_shape=[jax.ShapeDtypeStruct((b, t_len, D_LRU), BF16),
                   jax.ShapeDtypeStruct((b, 2, D_LRU), F32)],
        scratch_shapes=[
            pltpu.VMEM((t_len + 2 * SUBLANES, LRU_BLOCK), F32),
            scan_buf, scan_buf, scan_buf, scan_buf,
            pltpu.VMEM((n_chunks, LRU_BLOCK), F32),
            pltpu.VMEM((n_chunks, LRU_BLOCK), F32),
        ],
        compiler_params=pltpu.CompilerParams(
            dimension_semantics=("arbitrary",) * 2, vmem_limit_bytes=V7X_VMEM_LIMIT_BYTES),
        name="rglru",
    )(x_lru, g_lru, conv_w, conv_b, w_ext, lru_lam, h0)


def _outproj_kernel(att_ref, lru_ref, x_ref, gate_ref, wa_ref, wl_ref, gpost_ref, y_ref):
    o = _dot(att_ref[...], wa_ref[...]) + _dot(lru_ref[...], wl_ref[...])
    ms = jnp.mean(o * o, axis=-1, keepdims=True)
    n = o * lax.rsqrt(ms + EPS) * gpost_ref[...]
    y_ref[...] = x_ref[...] + gate_ref[...] * n


def _outproj(att, lru, x, gate, w_att, w_lru, g_post, *, t_len, tm=512):
    n_tok = x.shape[0]
    tm = min(tm, t_len)
    tiles_per_batch = t_len // tm
    nb = gate.shape[0]
    if nb == 1:
        mod_map = lambda i: (0, 0, 0)
    else:
        mod_map = lambda i: (i // tiles_per_batch, 0, 0)
    tok_spec = pl.BlockSpec((tm, D_MODEL), lambda i: (i, 0))
    w_spec = pl.BlockSpec((D_MODEL, D_MODEL), lambda i: (0, 0))
    return pl.pallas_call(
        _outproj_kernel,
        grid=(n_tok // tm,),
        in_specs=[tok_spec, tok_spec, tok_spec,
                  pl.BlockSpec((None, 1, D_MODEL), mod_map),
                  w_spec, w_spec,
                  pl.BlockSpec((1, D_MODEL), lambda i: (0, 0))],
        out_specs=tok_spec,
        out_shape=jax.ShapeDtypeStruct((n_tok, D_MODEL), F32),
        compiler_params=pltpu.CompilerParams(
            dimension_semantics=("arbitrary",), vmem_limit_bytes=V7X_VMEM_LIMIT_BYTES),
        name="outproj",
    )(att, lru, x, gate, w_att, w_lru, g_post)


def _sublayer(x, scale, shift, gate, cache_k, cache_v, h0, use_rope, w):
    b, t_len, _ = x.shape
    xf = x.reshape(b * t_len, D_MODEL)
    emit_kv = cache_k is None
    outs = _inproj(xf, scale, shift, w["g_pre"], w["w_in"], t_len=t_len,
                   use_rope=use_rope, emit_kv_f32=emit_kv)
    qt, vt = outs[0], outs[2]
    k, g_att, x_lru, g_lru = [outs[i].reshape(b, t_len, D_MODEL) for i in (1, 3, 4, 5)]
    att = _attention(w["lam"], w["g_subln"], qt, k, vt, g_att, cache_k, cache_v)
    lru, state = _lru(x_lru, g_lru, w["conv_w"], w["conv_b"], w["w_ext"], w["lru_lam"], h0)
    y = _outproj(att.reshape(b * t_len, D_ATT), lru.reshape(b * t_len, D_LRU), xf, gate,
                 w["w_out_att"], w["w_out_lru"], w["g_post"], t_len=t_len)
    y = y.reshape(b, t_len, D_MODEL)
    if emit_kv:
        return y, outs[6], outs[7], state
    return y, None, None, state


def kernel(x_prompt, x_sample, cache_k, cache_v, state_lru, c, c_ctx, w_ada, b_ada, g_pre, w_in, lambda_q1, lambda_k1, lambda_q2, lambda_k2, g_subln, conv_w, conv_b, w_rgate, b_rgate, w_igate, b_igate, lru_lambda, w_out, g_post):
    bp, seq, _ = x_prompt.shape
    bd, kc = cache_k.shape[0], cache_k.shape[2]
    assert cache_k.shape[1] == 1 and w_in.shape[0] == 1, "single-layer step only"
    l = 0

    cond = jnp.concatenate([c, c_ctx[None, :], jnp.zeros((SUBLANES - bd - 1, D_MODEL), F32)], axis=0)
    mod = _modulation(cond, w_ada[l], b_ada[l][None, :])
    shift, scale, gate = [mod[:, i * D_MODEL:(i + 1) * D_MODEL] for i in range(3)]

    def rows(a, lo, hi):
        return a[lo:hi][:, None, :]

    w = {
        "g_pre": g_pre[l][None, :],
        "w_in": w_in[l].astype(BF16),
        "lam": jnp.stack([lambda_q1[l], lambda_k1[l], lambda_q2[l], lambda_k2[l]], axis=0),
        "g_subln": g_subln[l][None, :],
        "conv_w": conv_w[l],
        "conv_b": conv_b[l][None, :],
        "w_ext": _lru_gate_operand(w_rgate[l], w_igate[l], b_rgate[l], b_igate[l]),
        "lru_lam": lru_lambda[l],
        "w_out_att": w_out[l, :D_ATT].astype(BF16),
        "w_out_lru": w_out[l, D_ATT:].astype(BF16),
        "g_post": g_post[l][None, :],
    }

    y_p, new_k, new_v, st_p = _sublayer(
        x_prompt, rows(scale, bd, bd + 1), rows(shift, bd, bd + 1), rows(gate, bd, bd + 1),
        None, None, jnp.zeros((bp, 2, D_LRU), F32), False, w)

    y_s, _, _, _ = _sublayer(
        x_sample, rows(scale, 0, bd), rows(shift, 0, bd), rows(gate, 0, bd),
        cache_k.reshape(bd, kc * N_HEADS, HEAD_W), cache_v.reshape(bd, kc * N_HEADS, HEAD_W),
        state_lru.reshape(bd, 2, D_LRU), True, w)

    new_k = new_k.reshape(bp, 1, seq, N_HEADS, HEAD_W)
    new_v = new_v.reshape(bp, 1, seq, N_HEADS, HEAD_W)
    return (y_p, y_s, new_k, new_v, st_p[:, None])
```

```python
import functools
import math

import jax
import jax.numpy as jnp
import numpy as np
from jax import lax
from jax.experimental import pallas as pl
from jax.experimental.pallas import tpu as pltpu

F32 = jnp.float32
BF16 = jnp.bfloat16

D_MODEL = 1024
GRID_W = 64
N_HEADS = 8
DIFF_HEAD_DIM = 64
HEAD_W = 2 * DIFF_HEAD_DIM
D_ATT = N_HEADS * HEAD_W
D_LRU = 1024
N_LRU_BLOCKS = 8
LRU_BLOCK = D_LRU // N_LRU_BLOCKS
LRU_C = 8.0
N_GROUPS = 6
ROPE_BASE = 10000.0
EPS = 1e-6
LAM_INIT = 0.8 - 0.6 * math.exp(-0.3 * 0)
LOG2_E = math.log2(math.e)

V7X_VMEM_LIMIT_BYTES = 56 * 1024 * 1024
SUBLANES = 8
ROPE_SWAP = DIFF_HEAD_DIM // 4
BF16_ROWS_PER_VREG = 16
ACC_ROWS = HEAD_W + BF16_ROWS_PER_VREG
CACHE_CHUNK = 256
BIAS_SPLIT = 3


def _silu(x):
    return x * jax.nn.sigmoid(x)


def _dot(a, b):
    return jnp.dot(a, b, preferred_element_type=F32)


def _dot_nt(a, b):
    return lax.dot_general(a, b, (((1,), (1,)), ((), ())), preferred_element_type=F32)


def _mod_kernel(cond_ref, w_ref, b_ref, o_ref):
    s = _silu(cond_ref[...])
    o_ref[...] = _dot(s.astype(BF16), w_ref[...].astype(BF16)) + b_ref[...]


def _modulation(cond, w_ada, b_ada):
    n = cond.shape[0]
    return pl.pallas_call(
        _mod_kernel,
        grid=(3,),
        in_specs=[
            pl.BlockSpec((n, D_MODEL), lambda j: (0, 0)),
            pl.BlockSpec((D_MODEL, D_MODEL), lambda j: (0, j)),
            pl.BlockSpec((1, D_MODEL), lambda j: (0, j)),
        ],
        out_specs=pl.BlockSpec((n, D_MODEL), lambda j: (0, j)),
        out_shape=jax.ShapeDtypeStruct((n, 3 * D_MODEL), F32),
        name="modulation",
    )(cond, w_ada, b_ada)


def _rope_tables(t_len):
    half = DIFF_HEAD_DIM // 2
    nf = half // 2
    t = np.arange(t_len)
    row = (t // GRID_W).astype(np.float32)
    col = (t % GRID_W).astype(np.float32)
    inv = (ROPE_BASE ** (-np.arange(nf, dtype=np.float32) * 2.0 / half)).astype(np.float32)
    lane = np.arange(HEAD_W) % DIFF_HEAD_DIM
    use_row = lane < half
    freq = (lane % half) % nf
    first = (lane % half) < nf
    pos = np.where(use_row[None, :], row[:, None], col[:, None]).astype(np.float32)
    ang = (pos * inv[freq][None, :]).astype(np.float32).astype(np.float64)
    cos = np.cos(ang).astype(np.float32)
    sin = (np.sin(ang) * np.where(first, -1.0, 1.0)[None, :]).astype(np.float32)
    return jnp.asarray(cos), jnp.asarray(sin)


def _inproj_kernel(*refs, use_rope, emit_kv_f32):
    x_ref, scale_ref, shift_ref, gpre_ref, w_ref = refs[:5]
    refs = refs[5:]
    if use_rope:
        cos_ref, sin_ref = refs[:2]
        refs = refs[2:]
    q_ref, k_ref, v_ref, gatt_ref, xlru_ref, glru_ref = refs[:6]
    refs = refs[6:]
    if emit_kv_f32:
        kf_ref, vf_ref = refs

    x = x_ref[...]
    ms = jnp.mean(x * x, axis=-1, keepdims=True)
    y = x * lax.rsqrt(ms + EPS) * gpre_ref[...]
    h = (y * (1.0 + scale_ref[...]) + shift_ref[...]).astype(BF16)

    def proj(g):
        return _dot(h, w_ref[:, g * D_MODEL:(g + 1) * D_MODEL])

    if use_rope:
        cos = cos_ref[...]
        sin = sin_ref[...]
        lane = lax.broadcasted_iota(jnp.int32, cos.shape, 1)
        take_next = (lane % (2 * ROPE_SWAP)) < ROPE_SWAP

        def rope(p):
            outs = []
            for hd in range(N_HEADS):
                xh = p[:, hd * HEAD_W:(hd + 1) * HEAD_W]
                partner = jnp.where(take_next,
                                    pltpu.roll(xh, HEAD_W - ROPE_SWAP, 1),
                                    pltpu.roll(xh, ROPE_SWAP, 1))
                outs.append(xh * cos + partner * sin)
            return jnp.concatenate(outs, axis=-1)
    else:
        def rope(p):
            return p

    q = rope(proj(0)) * (LOG2_E / math.sqrt(DIFF_HEAD_DIM))
    q_ref[...] = q.T.astype(BF16)
    k = rope(proj(1))
    k_ref[...] = k.astype(BF16)
    v = proj(2)
    vt = v.T
    for hd in range(N_HEADS):
        v_ref[hd] = vt[hd * HEAD_W:(hd + 1) * HEAD_W].astype(BF16)
    if emit_kv_f32:
        kf_ref[...] = k
        vf_ref[...] = v
    gatt_ref[...] = proj(3)
    xlru_ref[...] = proj(4)
    glru_ref[...] = proj(5)


def _inproj(x, scale, shift, g_pre, w_in_bf16, *, t_len, use_rope, emit_kv_f32, tm=256):
    n_tok = x.shape[0]
    tiles_per_batch = t_len // tm
    nb = scale.shape[0]
    if nb == 1:
        mod_map = lambda i: (0, 0, 0)
    else:
        mod_map = lambda i: (i // tiles_per_batch, 0, 0)
    tok_spec = pl.BlockSpec((tm, D_MODEL), lambda i: (i, 0))
    in_specs = [
        tok_spec,
        pl.BlockSpec((None, 1, D_MODEL), mod_map),
        pl.BlockSpec((None, 1, D_MODEL), mod_map),
        pl.BlockSpec((1, D_MODEL), lambda i: (0, 0)),
        pl.BlockSpec((D_MODEL, N_GROUPS * D_MODEL), lambda i: (0, 0)),
    ]
    args = [x, scale, shift, g_pre, w_in_bf16]
    if use_rope:
        cos, sin = _rope_tables(t_len)
        rope_spec = pl.BlockSpec((tm, HEAD_W), lambda i: (i % tiles_per_batch, 0))
        in_specs += [rope_spec, rope_spec]
        args += [cos, sin]
    n_batch = n_tok // t_len
    kc = _key_chunk(t_len)
    tiles_per_chunk = kc // tm
    out_specs = [
        pl.BlockSpec((None, D_MODEL, tm), lambda i: (i // tiles_per_batch, 0, i % tiles_per_batch)),
        tok_spec,
        pl.BlockSpec((None, N_HEADS, None, HEAD_W, tm),
                     lambda i: (i // tiles_per_batch, 0, (i % tiles_per_batch) // tiles_per_chunk, 0,
                                i % tiles_per_chunk)),
        tok_spec, tok_spec, tok_spec,
    ]
    out_shape = [
        jax.ShapeDtypeStruct((n_batch, D_MODEL, t_len), BF16),
        jax.ShapeDtypeStruct((n_tok, D_MODEL), BF16),
        jax.ShapeDtypeStruct((n_batch, N_HEADS, t_len // kc, HEAD_W, kc), BF16),
    ] + [jax.ShapeDtypeStruct((n_tok, D_MODEL), F32)] * 3
    if emit_kv_f32:
        out_specs += [tok_spec] * 2
        out_shape += [jax.ShapeDtypeStruct((n_tok, D_MODEL), F32)] * 2
    return pl.pallas_call(
        functools.partial(_inproj_kernel, use_rope=use_rope, emit_kv_f32=emit_kv_f32),
        grid=(n_tok // tm,),
        in_specs=in_specs,
        out_specs=out_specs,
        out_shape=out_shape,
        compiler_params=pltpu.CompilerParams(
            dimension_semantics=("arbitrary",), vmem_limit_bytes=V7X_VMEM_LIMIT_BYTES),
        name="inproj_rope" if use_rope else "inproj",
    )(*args)


def _key_chunk(t_len):
    return min(512, t_len)


def _attn_kernel(*refs, has_cache, n_chunks, kc, heads_per_step):
    lam_ref, gsub_ref, qt_ref, k_ref, vt_ref, gatt_ref = refs[:6]
    refs = refs[6:]
    cache_refs = None
    if has_cache:
        cache_refs = refs[:2]
        refs = refs[2:]
    o_ref, acc_ref, s_ref, e_ref = refs

    lp = lam_ref[...]
    lam = (jnp.exp(jnp.sum(lp[0:1] * lp[1:2], axis=-1, keepdims=True))
           - jnp.exp(jnp.sum(lp[2:3] * lp[3:4], axis=-1, keepdims=True)) + LAM_INIT)
    g_subln = gsub_ref[...]

    if n_chunks == 1 and not has_cache:
        _attn_single_chunk(lam, g_subln, qt_ref, k_ref, vt_ref, gatt_ref, o_ref, s_ref, e_ref,
                           n_heads=heads_per_step)
        return
    for hh in range(heads_per_step):
        lanes = slice(hh * HEAD_W, (hh + 1) * HEAD_W)
        _attn_accumulate(qt_ref.at[lanes, :], k_ref.at[:, lanes], vt_ref.at[hh], cache_refs,
                         pl.program_id(1) * heads_per_step + hh, acc_ref, s_ref, e_ref,
                         n_chunks=n_chunks, kc=kc)
        _attn_finalize(lam, g_subln, acc_ref[0], acc_ref[1], gatt_ref.at[:, lanes], o_ref.at[:, lanes])


def _query_maps(qt):
    row = lax.broadcasted_iota(jnp.int32, qt.shape, 0)
    zero = jnp.zeros_like(qt)
    return jnp.where(row < DIFF_HEAD_DIM, qt, zero), jnp.where(row >= DIFF_HEAD_DIM, qt, zero)


def _with_ones(vt_chunk):
    ones = jnp.ones((ACC_ROWS - HEAD_W, vt_chunk.shape[1]), BF16)
    return jnp.concatenate([vt_chunk, ones], axis=0)


def _attn_single_chunk(lam, g_subln, qt_ref, k_ref, vt_ref, gatt_ref, o_ref, s_ref, e_ref, *, n_heads):
    maxes = {}
    for t in range(n_heads + 2):
        if t < n_heads:
            lanes = slice(t * HEAD_W, (t + 1) * HEAD_W)
            keys = k_ref[:, lanes]
            q_maps = _query_maps(qt_ref[lanes, :])
            head_max = []
            for idx in range(2):
                st = _dot(keys, q_maps[idx])
                s_ref[t % 2, idx] = st
                head_max.append(jnp.max(st, axis=0, keepdims=True))
            maxes[t] = head_max
        if 1 <= t <= n_heads:
            h = t - 1
            for idx in range(2):
                e_ref[h % 2, idx] = jnp.exp2(s_ref[h % 2, idx] - maxes[h][idx]).astype(BF16)
        if t >= 2:
            h = t - 2
            lanes = slice(h * HEAD_W, (h + 1) * HEAD_W)
            vt_ext = _with_ones(vt_ref[h, 0])
            acc1, acc2 = [_dot(vt_ext, e_ref[h % 2, idx]) for idx in range(2)]
            _attn_finalize(lam, g_subln, acc1, acc2, gatt_ref.at[:, lanes], o_ref.at[:, lanes])


def _attn_finalize(lam, g_subln, acc1, acc2, gatt_ref, o_ref):
    o1 = acc1[:HEAD_W] * (1.0 / acc1[HEAD_W:HEAD_W + 1])
    o2 = acc2[:HEAD_W] * (1.0 / acc2[HEAD_W:HEAD_W + 1])
    ot = o1 - lam * o2
    ms = jnp.mean(ot * ot, axis=0, keepdims=True)
    on = (ot * lax.rsqrt(ms + EPS)).T * (g_subln * (1.0 - LAM_INIT))
    o_ref[...] = (on * _silu(gatt_ref[...])).astype(BF16)


def _attn_accumulate(qt_ref, k_ref, vt_ref, cache_refs, head, acc_ref, s_ref, e_ref, *, n_chunks, kc):
    has_cache = cache_refs is not None
    tq = qt_ref.shape[1]
    q_maps = _query_maps(qt_ref[...])

    acc_ref[...] = jnp.zeros(acc_ref.shape, F32)

    n_cache = 0
    if has_cache:
        kc_ref, vc_ref = cache_refs
        cache_len = kc_ref.shape[0] // N_HEADS
        cache_chunk = min(CACHE_CHUNK, cache_len)
        n_cache = cache_len // cache_chunk
    n_total = n_chunks + n_cache

    def is_cache(c):
        return isinstance(c, int) and c < n_cache

    def cache_rows(c):
        return pl.ds(head + c * cache_chunk * N_HEADS, cache_chunk, stride=N_HEADS)

    def keys_of(c):
        if is_cache(c):
            return kc_ref[cache_rows(c), :].astype(BF16)
        start = (c - n_cache) * kc
        return k_ref[pl.ds(start if isinstance(c, int) else pl.multiple_of(start, kc), kc), :]

    def values_of(c):
        if is_cache(c):
            return vc_ref[cache_rows(c), :].T.astype(BF16)
        return vt_ref[c - n_cache]

    def scores(c, slot):
        k_chunk = keys_of(c)
        maxes = []
        for idx in range(2):
            st = _dot(k_chunk, q_maps[idx])
            s_ref[slot, idx, :k_chunk.shape[0], :] = st
            maxes.append(jnp.max(st, axis=0, keepdims=True))
        return tuple(maxes)

    def probs(n_keys, slot, ms, maxes):
        new_ms, alphas = [], []
        for idx in range(2):
            m_new = jnp.maximum(ms[idx], maxes[idx])
            alphas.append(jnp.exp2(ms[idx] - m_new))
            e_ref[slot, idx, :n_keys, :] = jnp.exp2(s_ref[slot, idx, :n_keys, :] - m_new).astype(BF16)
            new_ms.append(m_new)
        return tuple(new_ms), tuple(alphas)

    def values(c, slot, alphas):
        vt_ext = _with_ones(values_of(c))
        n_keys = vt_ext.shape[1]
        for idx in range(2):
            acc_ref[idx] = acc_ref[idx] * alphas[idx] + _dot(vt_ext, e_ref[slot, idx, :n_keys, :])

    def n_keys_of(c):
        return cache_chunk if is_cache(c) else kc

    def step(i, slot, ms, maxes_next, alphas_cur):
        maxes_after = scores(i + 2, slot)
        ms, alphas_next = probs(n_keys_of(i + 1), 1 - slot, ms, maxes_next)
        values(i, slot, alphas_cur)
        return ms, maxes_after, alphas_next

    m0 = jnp.full((1, tq), -jnp.inf, F32)
    ms = (m0, m0)
    maxes_cur = scores(0, 0)
    if n_total == 1:
        ms, alphas_cur = probs(n_keys_of(0), 0, ms, maxes_cur)
        values(0, 0, alphas_cur)
    else:
        maxes_next = scores(1, 1)
        ms, alphas_cur = probs(n_keys_of(0), 0, ms, maxes_cur)
        n_steps = n_total - 2
        first_loop = n_cache
        for i in range(min(first_loop, n_steps)):
            ms, maxes_next, alphas_cur = step(i, i % 2, ms, maxes_next, alphas_cur)
        n_pairs = max(n_steps - first_loop, 0) // 2

        def pair(t, carry):
            ms, maxes_next, alphas_cur = carry
            i = first_loop + 2 * t
            ms, maxes_next, alphas_cur = step(i, first_loop % 2, ms, maxes_next, alphas_cur)
            return step(i + 1, (first_loop + 1) % 2, ms, maxes_next, alphas_cur)

        ms, maxes_next, alphas_cur = lax.fori_loop(0, n_pairs, pair, (ms, maxes_next, alphas_cur))
        for i in range(first_loop + 2 * n_pairs, n_steps):
            ms, maxes_next, alphas_cur = step(i, i % 2, ms, maxes_next, alphas_cur)
        last = n_total - 1
        ms, alphas_last = probs(n_keys_of(last), last % 2, ms, maxes_next)
        values(last - 1, (last - 1) % 2, alphas_cur)
        values(last, last % 2, alphas_last)


def _attention(lam_params, g_subln, qt, k, vt, g_att, cache_k=None, cache_v=None, *, tq=2048):
    b, t_len, _ = k.shape
    has_cache = cache_k is not None
    n_chunks, kc = vt.shape[2], vt.shape[4]
    tq = min(tq, t_len)
    heads_per_step = N_HEADS if n_chunks == 1 and not has_cache else 1
    width = heads_per_step * HEAD_W
    s_rows = max(kc, cache_k.shape[1] // N_HEADS) if has_cache else kc
    tok_spec = pl.BlockSpec((None, tq, width), lambda bi, h, qi: (bi, qi, h))
    in_specs = [
        pl.BlockSpec(lam_params.shape, lambda bi, h, qi: (0, 0)),
        pl.BlockSpec((1, HEAD_W), lambda bi, h, qi: (0, 0)),
        pl.BlockSpec((None, width, tq), lambda bi, h, qi: (bi, h, qi)),
        pl.BlockSpec((None, t_len, width), lambda bi, h, qi: (bi, 0, h)),
        pl.BlockSpec((None, heads_per_step, n_chunks, HEAD_W, kc), lambda bi, h, qi: (bi, h, 0, 0, 0)),
        tok_spec,
    ]
    args = [lam_params, g_subln, qt, k, vt, g_att]
    if has_cache:
        c_spec = pl.BlockSpec((None,) + cache_k.shape[1:], lambda bi, h, qi: (bi, 0, 0))
        in_specs += [c_spec, c_spec]
        args += [cache_k, cache_v]
    return pl.pallas_call(
        functools.partial(_attn_kernel, has_cache=has_cache, n_chunks=n_chunks, kc=kc,
                          heads_per_step=heads_per_step),
        grid=(b, N_HEADS // heads_per_step, t_len // tq),
        in_specs=in_specs,
        out_specs=tok_spec,
        out_shape=jax.ShapeDtypeStruct((b, t_len, D_ATT), BF16),
        scratch_shapes=[pltpu.VMEM((2, ACC_ROWS, tq), F32),
                        pltpu.VMEM((2, 2, s_rows, tq), F32),
                        pltpu.VMEM((2, 2, s_rows, tq), BF16)],
        compiler_params=pltpu.CompilerParams(
            dimension_semantics=("arbitrary",) * 3, vmem_limit_bytes=V7X_VMEM_LIMIT_BYTES),
        name="diff_attn_cache" if has_cache else "diff_attn",
    )(*args)


def _lru_kernel(x_ref, g_ref, cw_ref, cb_ref, w_ref, lam_ref, h0_ref, o_ref, st_ref, *scratch,
                blocks_per_step, **static):
    for j in range(blocks_per_step):
        lanes = slice(j * LRU_BLOCK, (j + 1) * LRU_BLOCK)
        _lru_block(x_ref.at[:, lanes], g_ref.at[:, lanes], cw_ref.at[:, lanes], cb_ref.at[:, lanes],
                   w_ref.at[j], lam_ref.at[:, lanes], h0_ref.at[:, lanes],
                   o_ref.at[:, lanes], st_ref.at[:, lanes], *scratch, **static)


def _lru_block(x_ref, g_ref, cw_ref, cb_ref, w_ref, lam_ref, h0_ref,
               o_ref, st_ref,
               xp, a_f, b_f, a_b, b_b, hin_f, hin_b,
               *, t_len, chunk, pitch, rows):
    g_f, p_f, g_b, p_b = b_f, a_f, b_b, a_b
    n_chunks = t_len // chunk
    chunks_per_tile = rows // chunk

    pad = jnp.zeros((SUBLANES, LRU_BLOCK), F32)
    xp[pl.ds(0, SUBLANES), :] = pad
    xp[pl.ds(SUBLANES, t_len), :] = x_ref[...]
    xp[pl.ds(SUBLANES + t_len, SUBLANES), :] = pad

    lam = lam_ref[...]
    sp = jnp.maximum(-lam, 0.0) + jnp.log1p(jnp.exp(-jnp.abs(lam)))
    half_scale = (-0.5 * LRU_C) * sp
    cw = cw_ref[...]
    cb = cb_ref[...]
    w_ext = w_ref[...]
    lane = lax.broadcasted_iota(jnp.int32, (rows, LRU_BLOCK), 1)
    bias_taps = jnp.where(lane < BIAS_SPLIT, 1.0, 0.0).astype(BF16)

    def gate_tile(r, carry):
        t0 = pl.multiple_of(r * rows, SUBLANES)
        u = cb
        for j in range(4):
            u = u + xp[pl.ds(t0 + SUBLANES - 1 + j, rows), :] * cw[j:j + 1]
        zh = _dot(jnp.concatenate([u.astype(BF16), bias_taps], axis=1), w_ext)
        u_half = 0.5 * u
        for d, (a_s, b_s) in enumerate(((a_f, b_f), (a_b, b_b))):
            tr = jnp.tanh(zh[:, (2 * d) * LRU_BLOCK:(2 * d + 1) * LRU_BLOCK])
            ti = jnp.tanh(zh[:, (2 * d + 1) * LRU_BLOCK:(2 * d + 2) * LRU_BLOCK])
            log_a = half_scale[d:d + 1] + half_scale[d:d + 1] * tr
            a = jnp.exp(log_a)
            gain_sq = jnp.tanh(log_a) * (-1.0 - a * a)
            gain = jnp.where(gain_sq > 0.0, gain_sq * lax.rsqrt(gain_sq), 0.0)
            bb = gain * (u_half + u_half * ti)
            for cc in range(chunks_per_tile):
                dst = pl.multiple_of(r * (chunks_per_tile * pitch), SUBLANES) + cc * pitch
                a_s[pl.ds(dst, chunk), :] = a[cc * chunk:(cc + 1) * chunk]
                b_s[pl.ds(dst, chunk), :] = bb[cc * chunk:(cc + 1) * chunk]
        return carry

    lax.fori_loop(0, t_len // rows, gate_tile, 0)

    def local_step(l, carry):
        hf, pf, hb, pb = carry
        rf = pl.ds(l, n_chunks, stride=pitch)
        a = a_f[rf, :]
        hf = a * hf + b_f[rf, :]
        pf = a * pf
        g_f[rf, :] = hf
        p_f[rf, :] = pf
        rb = pl.ds(chunk - 1 - l, n_chunks, stride=pitch)
        a = a_b[rb, :]
        hb = a * hb + b_b[rb, :]
        pb = a * pb
        g_b[rb, :] = hb
        p_b[rb, :] = pb
        return hf, pf, hb, pb

    z0 = jnp.zeros((n_chunks, LRU_BLOCK), F32)
    o0 = jnp.ones((n_chunks, LRU_BLOCK), F32)
    end_f, decay_f, end_b, decay_b = lax.fori_loop(0, chunk, local_step, (z0, o0, z0, o0), unroll=4)

    h0 = h0_ref[...]
    chunk_id = lax.broadcasted_iota(jnp.int32, (n_chunks, LRU_BLOCK), 0)

    def compose(end, decay, towards_higher):
        shift = 1
        while shift < n_chunks:
            if towards_higher:
                valid = chunk_id >= shift
                amount = shift
            else:
                valid = chunk_id < n_chunks - shift
                amount = n_chunks - shift
            prev_end = jnp.where(valid, pltpu.roll(end, amount, 0), 0.0)
            prev_decay = jnp.where(valid, pltpu.roll(decay, amount, 0), 1.0)
            end = end + decay * prev_end
            decay = decay * prev_decay
            shift *= 2
        return end, decay

    end_f, decay_f = compose(end_f, decay_f, True)
    after_f = end_f + decay_f * h0[0:1]
    hin_f[...] = jnp.where(chunk_id == 0, h0[0:1], pltpu.roll(after_f, 1, 0))
    st_ref[0:1, :] = after_f[n_chunks - 1:n_chunks]
    end_b, decay_b = compose(end_b, decay_b, False)
    after_b = end_b + decay_b * h0[1:2]
    hin_b[...] = jnp.where(chunk_id == n_chunks - 1, h0[1:2], pltpu.roll(after_b, n_chunks - 1, 0))
    st_ref[1:2, :] = after_b[0:1]

    def out_pair(cp, carry):
        for par in range(2):
            c = 2 * cp + par
            src = pl.ds(pl.multiple_of(cp * (2 * pitch), SUBLANES) + par * pitch, chunk)
            hf = g_f[src, :] + p_f[src, :] * hin_f[pl.ds(c, 1), :]
            hb = g_b[src, :] + p_b[src, :] * hin_b[pl.ds(c, 1), :]
            dst = pl.ds(pl.multiple_of(c * chunk, SUBLANES), chunk)
            o_ref[dst, :] = ((hf + hb) * _silu(g_ref[dst, :])).astype(BF16)
        return carry

    lax.fori_loop(0, n_chunks // 2, out_pair, 0, unroll=2)


def _lru_gate_operand(w_r, w_i, b_r, b_i):
    weights = 0.5 * jnp.concatenate([w_r[0], w_i[0], w_r[1], w_i[1]], axis=-1)
    bias = 0.5 * jnp.concatenate(
        [bb.reshape(N_LRU_BLOCKS, 1, LRU_BLOCK) for bb in (b_r[0], b_i[0], b_r[1], b_i[1])], axis=-1)
    parts, rest = [], bias
    for _ in range(BIAS_SPLIT):
        gamma = rest * (2.0 ** 16 + 1.0)
        part = gamma - (gamma - rest)
        parts.append(part)
        rest = rest - part
    bias_rows = jnp.pad(jnp.concatenate(parts, axis=1), ((0, 0), (0, LRU_BLOCK - BIAS_SPLIT), (0, 0)))
    return jnp.concatenate([weights, bias_rows], axis=1).astype(BF16)


def _lru(x_lru, g_lru, conv_w, conv_b, w_ext, lru_lam, h0):
    b, t_len, _ = x_lru.shape
    chunk = 64 if t_len >= 2048 else 32
    pitch = chunk + SUBLANES // 2
    n_chunks = t_len // chunk
    rows = min(2048, t_len)
    blocks_per_step = N_LRU_BLOCKS if t_len <= 512 else 1
    width = blocks_per_step * LRU_BLOCK
    seq_spec = pl.BlockSpec((None, t_len, width), lambda bi, n: (bi, 0, n))
    st_spec = pl.BlockSpec((None, 2, width), lambda bi, n: (bi, 0, n))
    scan_buf = pltpu.VMEM((n_chunks * pitch, LRU_BLOCK), F32)
    return pl.pallas_call(
        functools.partial(_lru_kernel, blocks_per_step=blocks_per_step,
                          t_len=t_len, chunk=chunk, pitch=pitch, rows=rows),
        grid=(b, N_LRU_BLOCKS // blocks_per_step),
        in_specs=[
            seq_spec, seq_spec,
            pl.BlockSpec((4, width), lambda bi, n: (0, n)),
            pl.BlockSpec((1, width), lambda bi, n: (0, n)),
            pl.BlockSpec((blocks_per_step, 2 * LRU_BLOCK, 4 * LRU_BLOCK), lambda bi, n: (n, 0, 0)),
            pl.BlockSpec((2, width), lambda bi, n: (0, n)),
            st_spec,
        ],
        out_specs=[seq_spec, st_spec],
        out_shape=[jax.ShapeDtypeStruct((b, t_len, D_LRU), BF16),
                   jax.ShapeDtypeStruct((b, 2, D_LRU), F32)],
        scratch_shapes=[
            pltpu.VMEM((t_len + 2 * SUBLANES, LRU_BLOCK), F32),
            scan_buf, scan_buf, scan_buf, scan_buf,
            pltpu.VMEM((n_chunks, LRU_BLOCK), F32),
            pltpu.VMEM((n_chunks, LRU_BLOCK), F32),
        ],
        compiler_params=pltpu.CompilerParams(
            dimension_semantics=("arbitrary",) * 2, vmem_limit_bytes=V7X_VMEM_LIMIT_BYTES),
        name="rglru",
    )(x_lru, g_lru, conv_w, conv_b, w_ext, lru_lam, h0)


def _outproj_kernel(att_ref, lru_ref, x_ref, gate_ref, wa_ref, wl_ref, gpost_ref, y_ref):
    o = _dot(att_ref[...], wa_ref[...]) + _dot(lru_ref[...], wl_ref[...])
    ms = jnp.mean(o * o, axis=-1, keepdims=True)
    n = o * lax.rsqrt(ms + EPS) * gpost_ref[...]
    y_ref[...] = x_ref[...] + gate_ref[...] * n


def _outproj(att, lru, x, gate, w_att, w_lru, g_post, *, t_len, tm=512):
    n_tok = x.shape[0]
    tm = min(tm, t_len)
    tiles_per_batch = t_len // tm
    nb = gate.shape[0]
    if nb == 1:
        mod_map = lambda i: (0, 0, 0)
    else:
        mod_map = lambda i: (i // tiles_per_batch, 0, 0)
    tok_spec = pl.BlockSpec((tm, D_MODEL), lambda i: (i, 0))
    w_spec = pl.BlockSpec((D_MODEL, D_MODEL), lambda i: (0, 0))
    return pl.pallas_call(
        _outproj_kernel,
        grid=(n_tok // tm,),
        in_specs=[tok_spec, tok_spec, tok_spec,
                  pl.BlockSpec((None, 1, D_MODEL), mod_map),
                  w_spec, w_spec,
                  pl.BlockSpec((1, D_MODEL), lambda i: (0, 0))],
        out_specs=tok_spec,
        out_shape=jax.ShapeDtypeStruct((n_tok, D_MODEL), F32),
        compiler_params=pltpu.CompilerParams(
            dimension_semantics=("arbitrary",), vmem_limit_bytes=V7X_VMEM_LIMIT_BYTES),
        name="outproj",
    )(att, lru, x, gate, w_att, w_lru, g_post)


def _sublayer(x, scale, shift, gate, cache_k, cache_v, h0, use_rope, w):
    b, t_len, _ = x.shape
    xf = x.reshape(b * t_len, D_MODEL)
    emit_kv = cache_k is None
    outs = _inproj(xf, scale, shift, w["g_pre"], w["w_in"], t_len=t_len,
                   use_rope=use_rope, emit_kv_f32=emit_kv)
    qt, vt = outs[0], outs[2]
    k, g_att, x_lru, g_lru = [outs[i].reshape(b, t_len, D_MODEL) for i in (1, 3, 4, 5)]
    att = _attention(w["lam"], w["g_subln"], qt, k, vt, g_att, cache_k, cache_v)
    lru, state = _lru(x_lru, g_lru, w["conv_w"], w["conv_b"], w["w_ext"], w["lru_lam"], h0)
    y = _outproj(att.reshape(b * t_len, D_ATT), lru.reshape(b * t_len, D_LRU), xf, gate,
                 w["w_out_att"], w["w_out_lru"], w["g_post"], t_len=t_len)
    y = y.reshape(b, t_len, D_MODEL)
    if emit_kv:
        return y, outs[6], outs[7], state
    return y, None, None, state


def kernel(x_prompt, x_sample, cache_k, cache_v, state_lru, c, c_ctx, w_ada, b_ada, g_pre, w_in, lambda_q1, lambda_k1, lambda_q2, lambda_k2, g_subln, conv_w, conv_b, w_rgate, b_rgate, w_igate, b_igate, lru_lambda, w_out, g_post):
    bp, seq, _ = x_prompt.shape
    bd, kc = cache_k.shape[0], cache_k.shape[2]
    assert cache_k.shape[1] == 1 and w_in.shape[0] == 1, "single-layer step only"
    l = 0

    cond = jnp.concatenate([c, c_ctx[None, :], jnp.zeros((SUBLANES - bd - 1, D_MODEL), F32)], axis=0)
    mod = _modulation(cond, w_ada[l], b_ada[l][None, :])
    shift, scale, gate = [mod[:, i * D_MODEL:(i + 1) * D_MODEL] for i in range(3)]

    def rows(a, lo, hi):
        return a[lo:hi][:, None, :]

    w = {
        "g_pre": g_pre[l][None, :],
        "w_in": w_in[l].astype(BF16),
        "lam": jnp.stack([lambda_q1[l], lambda_k1[l], lambda_q2[l], lambda_k2[l]], axis=0),
        "g_subln": g_subln[l][None, :],
        "conv_w": conv_w[l],
        "conv_b": conv_b[l][None, :],
        "w_ext": _lru_gate_operand(w_rgate[l], w_igate[l], b_rgate[l], b_igate[l]),
        "lru_lam": lru_lambda[l],
        "w_out_att": w_out[l, :D_ATT].astype(BF16),
        "w_out_lru": w_out[l, D_ATT:].astype(BF16),
        "g_post": g_post[l][None, :],
    }

    y_p, new_k, new_v, st_p = _sublayer(
        x_prompt, rows(scale, bd, bd + 1), rows(shift, bd, bd + 1), rows(gate, bd, bd + 1),
        None, None, jnp.zeros((bp, 2, D_LRU), F32), False, w)

    y_s, _, _, _ = _sublayer(
        x_sample, rows(scale, 0, bd), rows(shift, 0, bd), rows(gate, 0, bd),
        cache_k.reshape(bd, kc * N_HEADS, HEAD_W), cache_v.reshape(bd, kc * N_HEADS, HEAD_W),
        state_lru.reshape(bd, 2, D_LRU), True, w)

    new_k = new_k.reshape(bp, 1, seq, N_HEADS, HEAD_W)
    new_v = new_v.reshape(bp, 1, seq, N_HEADS, HEAD_W)
    return (y_p, y_s, new_k, new_v, st_p[:, None])
```

```python
import functools
import math

import jax
import jax.numpy as jnp
import numpy as np
from jax import lax
from jax.experimental import pallas as pl
from jax.experimental.pallas import tpu as pltpu

F32 = jnp.float32
BF16 = jnp.bfloat16

D_MODEL = 1024
GRID_W = 64
N_HEADS = 8
DIFF_HEAD_DIM = 64
HEAD_W = 2 * DIFF_HEAD_DIM
D_ATT = N_HEADS * HEAD_W
D_LRU = 1024
N_LRU_BLOCKS = 8
LRU_BLOCK = D_LRU // N_LRU_BLOCKS
LRU_C = 8.0
N_GROUPS = 6
ROPE_BASE = 10000.0
EPS = 1e-6
LAM_INIT = 0.8 - 0.6 * math.exp(-0.3 * 0)
LOG2_E = math.log2(math.e)

V7X_VMEM_LIMIT_BYTES = 56 * 1024 * 1024
SUBLANES = 8
ROPE_SWAP = DIFF_HEAD_DIM // 4
BF16_ROWS_PER_VREG = 16
ACC_ROWS = HEAD_W + BF16_ROWS_PER_VREG
CACHE_CHUNK = 256
BIAS_SPLIT = 3


def _silu(x):
    return x * jax.nn.sigmoid(x)


def _dot(a, b):
    return jnp.dot(a, b, preferred_element_type=F32)


def _dot_nt(a, b):
    return lax.dot_general(a, b, (((1,), (1,)), ((), ())), preferred_element_type=F32)


def _mod_kernel(cond_ref, w_ref, b_ref, o_ref):
    s = _silu(cond_ref[...])
    o_ref[...] = _dot(s.astype(BF16), w_ref[...].astype(BF16)) + b_ref[...]


def _modulation(cond, w_ada, b_ada):
    n = cond.shape[0]
    return pl.pallas_call(
        _mod_kernel,
        grid=(3,),
        in_specs=[
            pl.BlockSpec((n, D_MODEL), lambda j: (0, 0)),
            pl.BlockSpec((D_MODEL, D_MODEL), lambda j: (0, j)),
            pl.BlockSpec((1, D_MODEL), lambda j: (0, j)),
        ],
        out_specs=pl.BlockSpec((n, D_MODEL), lambda j: (0, j)),
        out_shape=jax.ShapeDtypeStruct((n, 3 * D_MODEL), F32),
        name="modulation",
    )(cond, w_ada, b_ada)


def _rope_tables(t_len):
    half = DIFF_HEAD_DIM // 2
    nf = half // 2
    t = np.arange(t_len)
    row = (t // GRID_W).astype(np.float32)
    col = (t % GRID_W).astype(np.float32)
    inv = (ROPE_BASE ** (-np.arange(nf, dtype=np.float32) * 2.0 / half)).astype(np.float32)
    lane = np.arange(HEAD_W) % DIFF_HEAD_DIM
    use_row = lane < half
    freq = (lane % half) % nf
    first = (lane % half) < nf
    pos = np.where(use_row[None, :], row[:, None], col[:, None]).astype(np.float32)
    ang = (pos * inv[freq][None, :]).astype(np.float32).astype(np.float64)
    cos = np.cos(ang).astype(np.float32)
    sin = (np.sin(ang) * np.where(first, -1.0, 1.0)[None, :]).astype(np.float32)
    return jnp.asarray(cos), jnp.asarray(sin)


def _inproj_kernel(*refs, use_rope, emit_kv_f32):
    x_ref, scale_ref, shift_ref, gpre_ref, w_ref = refs[:5]
    refs = refs[5:]
    if use_rope:
        cos_ref, sin_ref = refs[:2]
        refs = refs[2:]
    q_ref, k_ref, v_ref, gatt_ref, xlru_ref, glru_ref = refs[:6]
    refs = refs[6:]
    if emit_kv_f32:
        kf_ref, vf_ref = refs

    x = x_ref[...]
    ms = jnp.mean(x * x, axis=-1, keepdims=True)
    y = x * lax.rsqrt(ms + EPS) * gpre_ref[...]
    h = (y * (1.0 + scale_ref[...]) + shift_ref[...]).astype(BF16)

    def proj(g):
        return _dot(h, w_ref[:, g * D_MODEL:(g + 1) * D_MODEL])

    if use_rope:
        cos = cos_ref[...]
        sin = sin_ref[...]
        lane = lax.broadcasted_iota(jnp.int32, cos.shape, 1)
        take_next = (lane % (2 * ROPE_SWAP)) < ROPE_SWAP

        def rope(p):
            outs = []
            for hd in range(N_HEADS):
                xh = p[:, hd * HEAD_W:(hd + 1) * HEAD_W]
                partner = jnp.where(take_next,
                                    pltpu.roll(xh, HEAD_W - ROPE_SWAP, 1),
                                    pltpu.roll(xh, ROPE_SWAP, 1))
                outs.append(xh * cos + partner * sin)
            return jnp.concatenate(outs, axis=-1)
    else:
        def rope(p):
            return p

    q = rope(proj(0)) * (LOG2_E / math.sqrt(DIFF_HEAD_DIM))
    q_ref[...] = q.T.astype(BF16)
    k = rope(proj(1))
    k_ref[...] = k.astype(BF16)
    v = proj(2)
    vt = v.T
    for hd in range(N_HEADS):
        v_ref[hd] = vt[hd * HEAD_W:(hd + 1) * HEAD_W].astype(BF16)
    if emit_kv_f32:
        kf_ref[...] = k
        vf_ref[...] = v
    gatt_ref[...] = proj(3)
    xlru_ref[...] = proj(4)
    glru_ref[...] = proj(5)


def _inproj(x, scale, shift, g_pre, w_in_bf16, *, t_len, use_rope, emit_kv_f32, tm=256):
    n_tok = x.shape[0]
    tiles_per_batch = t_len // tm
    nb = scale.shape[0]
    if nb == 1:
        mod_map = lambda i: (0, 0, 0)
    else:
        mod_map = lambda i: (i // tiles_per_batch, 0, 0)
    tok_spec = pl.BlockSpec((tm, D_MODEL), lambda i: (i, 0))
    in_specs = [
        tok_spec,
        pl.BlockSpec((None, 1, D_MODEL), mod_map),
        pl.BlockSpec((None, 1, D_MODEL), mod_map),
        pl.BlockSpec((1, D_MODEL), lambda i: (0, 0)),
        pl.BlockSpec((D_MODEL, N_GROUPS * D_MODEL), lambda i: (0, 0)),
    ]
    args = [x, scale, shift, g_pre, w_in_bf16]
    if use_rope:
        cos, sin = _rope_tables(t_len)
        rope_spec = pl.BlockSpec((tm, HEAD_W), lambda i: (i % tiles_per_batch, 0))
        in_specs += [rope_spec, rope_spec]
        args += [cos, sin]
    n_batch = n_tok // t_len
    kc = _key_chunk(t_len)
    tiles_per_chunk = kc // tm
    out_specs = [
        pl.BlockSpec((None, D_MODEL, tm), lambda i: (i // tiles_per_batch, 0, i % tiles_per_batch)),
        tok_spec,
        pl.BlockSpec((None, N_HEADS, None, HEAD_W, tm),
                     lambda i: (i // tiles_per_batch, 0, (i % tiles_per_batch) // tiles_per_chunk, 0,
                                i % tiles_per_chunk)),
        tok_spec, tok_spec, tok_spec,
    ]
    out_shape = [
        jax.ShapeDtypeStruct((n_batch, D_MODEL, t_len), BF16),
        jax.ShapeDtypeStruct((n_tok, D_MODEL), BF16),
        jax.ShapeDtypeStruct((n_batch, N_HEADS, t_len // kc, HEAD_W, kc), BF16),
    ] + [jax.ShapeDtypeStruct((n_tok, D_MODEL), F32)] * 3
    if emit_kv_f32:
        out_specs += [tok_spec] * 2
        out_shape += [jax.ShapeDtypeStruct((n_tok, D_MODEL), F32)] * 2
    return pl.pallas_call(
        functools.partial(_inproj_kernel, use_rope=use_rope, emit_kv_f32=emit_kv_f32),
        grid=(n_tok // tm,),
        in_specs=in_specs,
        out_specs=out_specs,
        out_shape=out_shape,
        compiler_params=pltpu.CompilerParams(
            dimension_semantics=("arbitrary",), vmem_limit_bytes=V7X_VMEM_LIMIT_BYTES),
        name="inproj_rope" if use_rope else "inproj",
    )(*args)


def _key_chunk(t_len):
    return min(512, t_len)


def _attn_kernel(*refs, has_cache, n_chunks, kc, heads_per_step):
    lam_ref, gsub_ref, qt_ref, k_ref, vt_ref, gatt_ref = refs[:6]
    refs = refs[6:]
    cache_refs = None
    if has_cache:
        cache_refs = refs[:2]
        refs = refs[2:]
    o_ref, acc_ref, s_ref, e_ref = refs

    lp = lam_ref[...]
    lam = (jnp.exp(jnp.sum(lp[0:1] * lp[1:2], axis=-1, keepdims=True))
           - jnp.exp(jnp.sum(lp[2:3] * lp[3:4], axis=-1, keepdims=True)) + LAM_INIT)
    g_subln = gsub_ref[...]

    if n_chunks == 1 and not has_cache:
        _attn_single_chunk(lam, g_subln, qt_ref, k_ref, vt_ref, gatt_ref, o_ref, s_ref, e_ref,
                           n_heads=heads_per_step)
        return
    for hh in range(heads_per_step):
        lanes = slice(hh * HEAD_W, (hh + 1) * HEAD_W)
        _attn_accumulate(qt_ref.at[lanes, :], k_ref.at[:, lanes], vt_ref.at[hh], cache_refs,
                         pl.program_id(1) * heads_per_step + hh, acc_ref, s_ref, e_ref,
                         n_chunks=n_chunks, kc=kc)
        _attn_finalize(lam, g_subln, acc_ref[0], acc_ref[1], gatt_ref.at[:, lanes], o_ref.at[:, lanes])


def _query_maps(qt):
    row = lax.broadcasted_iota(jnp.int32, qt.shape, 0)
    zero = jnp.zeros_like(qt)
    return jnp.where(row < DIFF_HEAD_DIM, qt, zero), jnp.where(row >= DIFF_HEAD_DIM, qt, zero)


def _with_ones(vt_chunk):
    ones = jnp.ones((ACC_ROWS - HEAD_W, vt_chunk.shape[1]), BF16)
    return jnp.concatenate([vt_chunk, ones], axis=0)


def _attn_single_chunk(lam, g_subln, qt_ref, k_ref, vt_ref, gatt_ref, o_ref, s_ref, e_ref, *, n_heads):
    maxes = {}
    for t in range(n_heads + 2):
        if t < n_heads:
            lanes = slice(t * HEAD_W, (t + 1) * HEAD_W)
            keys = k_ref[:, lanes]
            q_maps = _query_maps(qt_ref[lanes, :])
            head_max = []
            for idx in range(2):
                st = _dot(keys, q_maps[idx])
                s_ref[t % 2, idx] = st
                head_max.append(jnp.max(st, axis=0, keepdims=True))
            maxes[t] = head_max
        if 1 <= t <= n_heads:
            h = t - 1
            for idx in range(2):
                e_ref[h % 2, idx] = jnp.exp2(s_ref[h % 2, idx] - maxes[h][idx]).astype(BF16)
        if t >= 2:
            h = t - 2
            lanes = slice(h * HEAD_W, (h + 1) * HEAD_W)
            vt_ext = _with_ones(vt_ref[h, 0])
            acc1, acc2 = [_dot(vt_ext, e_ref[h % 2, idx]) for idx in range(2)]
            _attn_finalize(lam, g_subln, acc1, acc2, gatt_ref.at[:, lanes], o_ref.at[:, lanes])


def _attn_finalize(lam, g_subln, acc1, acc2, gatt_ref, o_ref):
    o1 = acc1[:HEAD_W] * (1.0 / acc1[HEAD_W:HEAD_W + 1])
    o2 = acc2[:HEAD_W] * (1.0 / acc2[HEAD_W:HEAD_W + 1])
    ot = o1 - lam * o2
    ms = jnp.mean(ot * ot, axis=0, keepdims=True)
    on = (ot * lax.rsqrt(ms + EPS)).T * (g_subln * (1.0 - LAM_INIT))
    o_ref[...] = (on * _silu(gatt_ref[...])).astype(BF16)


def _attn_accumulate(qt_ref, k_ref, vt_ref, cache_refs, head, acc_ref, s_ref, e_ref, *, n_chunks, kc):
    has_cache = cache_refs is not None
    tq = qt_ref.shape[1]
    q_maps = _query_maps(qt_ref[...])

    acc_ref[...] = jnp.zeros(acc_ref.shape, F32)

    n_cache = 0
    if has_cache:
        kc_ref, vc_ref = cache_refs
        cache_len = kc_ref.shape[0] // N_HEADS
        cache_chunk = min(CACHE_CHUNK, cache_len)
        n_cache = cache_len // cache_chunk
    n_total = n_chunks + n_cache

    def is_cache(c):
        return isinstance(c, int) and c < n_cache

    def cache_rows(c):
        return pl.ds(head + c * cache_chunk * N_HEADS, cache_chunk, stride=N_HEADS)

    def keys_of(c):
        if is_cache(c):
            return kc_ref[cache_rows(c), :].astype(BF16)
        start = (c - n_cache) * kc
        return k_ref[pl.ds(start if isinstance(c, int) else pl.multiple_of(start, kc), kc), :]

    def values_of(c):
        if is_cache(c):
            return vc_ref[cache_rows(c), :].T.astype(BF16)
        return vt_ref[c - n_cache]

    def scores(c, slot):
        k_chunk = keys_of(c)
        maxes = []
        for idx in range(2):
            st = _dot(k_chunk, q_maps[idx])
            s_ref[slot, idx, :k_chunk.shape[0], :] = st
            maxes.append(jnp.max(st, axis=0, keepdims=True))
        return tuple(maxes)

    def probs(n_keys, slot, ms, maxes):
        new_ms, alphas = [], []
        for idx in range(2):
            m_new = jnp.maximum(ms[idx], maxes[idx])
            alphas.append(jnp.exp2(ms[idx] - m_new))
            e_ref[slot, idx, :n_keys, :] = jnp.exp2(s_ref[slot, idx, :n_keys, :] - m_new).astype(BF16)
            new_ms.append(m_new)
        return tuple(new_ms), tuple(alphas)

    def values(c, slot, alphas):
        vt_ext = _with_ones(values_of(c))
        n_keys = vt_ext.shape[1]
        for idx in range(2):
            acc_ref[idx] = acc_ref[idx] * alphas[idx] + _dot(vt_ext, e_ref[slot, idx, :n_keys, :])

    def n_keys_of(c):
        return cache_chunk if is_cache(c) else kc

    def step(i, slot, ms, maxes_next, alphas_cur):
        maxes_after = scores(i + 2, slot)
        ms, alphas_next = probs(n_keys_of(i + 1), 1 - slot, ms, maxes_next)
        values(i, slot, alphas_cur)
        return ms, maxes_after, alphas_next

    m0 = jnp.full((1, tq), -jnp.inf, F32)
    ms = (m0, m0)
    maxes_cur = scores(0, 0)
    if n_total == 1:
        ms, alphas_cur = probs(n_keys_of(0), 0, ms, maxes_cur)
        values(0, 0, alphas_cur)
    else:
        maxes_next = scores(1, 1)
        ms, alphas_cur = probs(n_keys_of(0), 0, ms, maxes_cur)
        n_steps = n_total - 2
        first_loop = n_cache
        for i in range(min(first_loop, n_steps)):
            ms, maxes_next, alphas_cur = step(i, i % 2, ms, maxes_next, alphas_cur)
        n_pairs = max(n_steps - first_loop, 0) // 2

        def pair(t, carry):
            ms, maxes_next, alphas_cur = carry
            i = first_loop + 2 * t
            ms, maxes_next, alphas_cur = step(i, first_loop % 2, ms, maxes_next, alphas_cur)
            return step(i + 1, (first_loop + 1) % 2, ms, maxes_next, alphas_cur)

        ms, maxes_next, alphas_cur = lax.fori_loop(0, n_pairs, pair, (ms, maxes_next, alphas_cur))
        for i in range(first_loop + 2 * n_pairs, n_steps):
            ms, maxes_next, alphas_cur = step(i, i % 2, ms, maxes_next, alphas_cur)
        last = n_total - 1
        ms, alphas_last = probs(n_keys_of(last), last % 2, ms, maxes_next)
        values(last - 1, (last - 1) % 2, alphas_cur)
        values(last, last % 2, alphas_last)


def _attention(lam_params, g_subln, qt, k, vt, g_att, cache_k=None, cache_v=None, *, tq=2048):
    b, t_len, _ = k.shape
    has_cache = cache_k is not None
    n_chunks, kc = vt.shape[2], vt.shape[4]
    tq = min(tq, t_len)
    heads_per_step = N_HEADS if n_chunks == 1 and not has_cache else 1
    width = heads_per_step * HEAD_W
    s_rows = max(kc, cache_k.shape[1] // N_HEADS) if has_cache else kc
    tok_spec = pl.BlockSpec((None, tq, width), lambda bi, h, qi: (bi, qi, h))
    in_specs = [
        pl.BlockSpec(lam_params.shape, lambda bi, h, qi: (0, 0)),
        pl.BlockSpec((1, HEAD_W), lambda bi, h, qi: (0, 0)),
        pl.BlockSpec((None, width, tq), lambda bi, h, qi: (bi, h, qi)),
        pl.BlockSpec((None, t_len, width), lambda bi, h, qi: (bi, 0, h)),
        pl.BlockSpec((None, heads_per_step, n_chunks, HEAD_W, kc), lambda bi, h, qi: (bi, h, 0, 0, 0)),
        tok_spec,
    ]
    args = [lam_params, g_subln, qt, k, vt, g_att]
    if has_cache:
        c_spec = pl.BlockSpec((None,) + cache_k.shape[1:], lambda bi, h, qi: (bi, 0, 0))
        in_specs += [c_spec, c_spec]
        args += [cache_k, cache_v]
    return pl.pallas_call(
        functools.partial(_attn_kernel, has_cache=has_cache, n_chunks=n_chunks, kc=kc,
                          heads_per_step=heads_per_step),
        grid=(b, N_HEADS // heads_per_step, t_len // tq),
        in_specs=in_specs,
        out_specs=tok_spec,
        out_shape=jax.ShapeDtypeStruct((b, t_len, D_ATT), BF16),
        scratch_shapes=[pltpu.VMEM((2, ACC_ROWS, tq), F32),
                        pltpu.VMEM((2, 2, s_rows, tq), F32),
                        pltpu.VMEM((2, 2, s_rows, tq), BF16)],
        compiler_params=pltpu.CompilerParams(
            dimension_semantics=("arbitrary",) * 3, vmem_limit_bytes=V7X_VMEM_LIMIT_BYTES),
        name="diff_attn_cache" if has_cache else "diff_attn",
    )(*args)


def _lru_kernel(x_ref, g_ref, cw_ref, cb_ref, w_ref, lam_ref, h0_ref,
                o_ref, st_ref,
                xp, a_f, b_f, a_b, b_b, hin_f, hin_b,
                *, blocks_per_step, t_len, chunk, pitch, rows):
    blocks = range(blocks_per_step)
    lanes = [slice(j * LRU_BLOCK, (j + 1) * LRU_BLOCK) for j in blocks]
    g_f, p_f, g_b, p_b = b_f, a_f, b_b, a_b
    n_chunks = t_len // chunk
    chunks_per_tile = rows // chunk

    pad = jnp.zeros((SUBLANES, LRU_BLOCK), F32)
    for j in blocks:
        xp[j, pl.ds(0, SUBLANES), :] = pad
        xp[j, pl.ds(SUBLANES, t_len), :] = x_ref[:, lanes[j]]
        xp[j, pl.ds(SUBLANES + t_len, SUBLANES), :] = pad

    lam = lam_ref[...]
    sp = jnp.maximum(-lam, 0.0) + jnp.log1p(jnp.exp(-jnp.abs(lam)))
    half_scale = (-0.5 * LRU_C) * sp
    cw = cw_ref[...]
    cb = cb_ref[...]
    lane = lax.broadcasted_iota(jnp.int32, (rows, LRU_BLOCK), 1)
    bias_taps = jnp.where(lane < BIAS_SPLIT, 1.0, 0.0).astype(BF16)

    def gate_tile(r, carry):
        t0 = pl.multiple_of(r * rows, SUBLANES)
        for j in blocks:
            u = cb[:, lanes[j]]
            for tap in range(4):
                u = u + xp[j, pl.ds(t0 + SUBLANES - 1 + tap, rows), :] * cw[tap:tap + 1, lanes[j]]
            zh = _dot(jnp.concatenate([u.astype(BF16), bias_taps], axis=1), w_ref[j])
            u_half = 0.5 * u
            for d, (a_s, b_s) in enumerate(((a_f, b_f), (a_b, b_b))):
                scale = half_scale[d:d + 1, lanes[j]]
                tr = jnp.tanh(zh[:, (2 * d) * LRU_BLOCK:(2 * d + 1) * LRU_BLOCK])
                ti = jnp.tanh(zh[:, (2 * d + 1) * LRU_BLOCK:(2 * d + 2) * LRU_BLOCK])
                log_a = scale + scale * tr
                a = jnp.exp(log_a)
                gain_sq = jnp.tanh(log_a) * (-1.0 - a * a)
                gain = jnp.where(gain_sq > 0.0, gain_sq * lax.rsqrt(gain_sq), 0.0)
                bb = gain * (u_half + u_half * ti)
                for cc in range(chunks_per_tile):
                    dst = pl.multiple_of(r * (chunks_per_tile * pitch), SUBLANES) + cc * pitch
                    a_s[j, pl.ds(dst, chunk), :] = a[cc * chunk:(cc + 1) * chunk]
                    b_s[j, pl.ds(dst, chunk), :] = bb[cc * chunk:(cc + 1) * chunk]
        return carry

    lax.fori_loop(0, t_len // rows, gate_tile, 0)

    def local_step(l, carry):
        out = []
        for j in blocks:
            hf, pf, hb, pb = carry[4 * j:4 * j + 4]
            rf = pl.ds(l, n_chunks, stride=pitch)
            a = a_f[j, rf, :]
            hf = a * hf + b_f[j, rf, :]
            pf = a * pf
            g_f[j, rf, :] = hf
            p_f[j, rf, :] = pf
            rb = pl.ds(chunk - 1 - l, n_chunks, stride=pitch)
            a = a_b[j, rb, :]
            hb = a * hb + b_b[j, rb, :]
            pb = a * pb
            g_b[j, rb, :] = hb
            p_b[j, rb, :] = pb
            out += [hf, pf, hb, pb]
        return tuple(out)

    z0 = jnp.zeros((n_chunks, LRU_BLOCK), F32)
    o0 = jnp.ones((n_chunks, LRU_BLOCK), F32)
    chunk_maps = lax.fori_loop(0, chunk, local_step, (z0, o0, z0, o0) * blocks_per_step, unroll=4)

    h0 = h0_ref[...]
    chunk_id = lax.broadcasted_iota(jnp.int32, (n_chunks, LRU_BLOCK), 0)

    def compose(end, decay, towards_higher):
        shift = 1
        while shift < n_chunks:
            if towards_higher:
                valid = chunk_id >= shift
                amount = shift
            else:
                valid = chunk_id < n_chunks - shift
                amount = n_chunks - shift
            prev_end = jnp.where(valid, pltpu.roll(end, amount, 0), 0.0)
            prev_decay = jnp.where(valid, pltpu.roll(decay, amount, 0), 1.0)
            end = end + decay * prev_end
            decay = decay * prev_decay
            shift *= 2
        return end, decay

    for j in blocks:
        end_f, decay_f, end_b, decay_b = chunk_maps[4 * j:4 * j + 4]
        h0_f = h0[0:1, lanes[j]]
        h0_b = h0[1:2, lanes[j]]
        end_f, decay_f = compose(end_f, decay_f, True)
        after_f = end_f + decay_f * h0_f
        hin_f[j] = jnp.where(chunk_id == 0, h0_f, pltpu.roll(after_f, 1, 0))
        st_ref[0:1, lanes[j]] = after_f[n_chunks - 1:n_chunks]
        end_b, decay_b = compose(end_b, decay_b, False)
        after_b = end_b + decay_b * h0_b
        hin_b[j] = jnp.where(chunk_id == n_chunks - 1, h0_b, pltpu.roll(after_b, n_chunks - 1, 0))
        st_ref[1:2, lanes[j]] = after_b[0:1]

    def out_pair(cp, carry):
        for par in range(2):
            c = 2 * cp + par
            src = pl.ds(pl.multiple_of(cp * (2 * pitch), SUBLANES) + par * pitch, chunk)
            dst = pl.ds(pl.multiple_of(c * chunk, SUBLANES), chunk)
            for j in blocks:
                hf = g_f[j, src, :] + p_f[j, src, :] * hin_f[j, pl.ds(c, 1), :]
                hb = g_b[j, src, :] + p_b[j, src, :] * hin_b[j, pl.ds(c, 1), :]
                o_ref[dst, lanes[j]] = ((hf + hb) * _silu(g_ref[dst, lanes[j]])).astype(BF16)
        return carry

    lax.fori_loop(0, n_chunks // 2, out_pair, 0, unroll=2)


def _lru_gate_operand(w_r, w_i, b_r, b_i):
    weights = 0.5 * jnp.concatenate([w_r[0], w_i[0], w_r[1], w_i[1]], axis=-1)
    bias = 0.5 * jnp.concatenate(
        [bb.reshape(N_LRU_BLOCKS, 1, LRU_BLOCK) for bb in (b_r[0], b_i[0], b_r[1], b_i[1])], axis=-1)
    parts, rest = [], bias
    for _ in range(BIAS_SPLIT):
        gamma = rest * (2.0 ** 16 + 1.0)
        part = gamma - (gamma - rest)
        parts.append(part)
        rest = rest - part
    bias_rows = jnp.pad(jnp.concatenate(parts, axis=1), ((0, 0), (0, LRU_BLOCK - BIAS_SPLIT), (0, 0)))
    return jnp.concatenate([weights, bias_rows], axis=1).astype(BF16)


def _lru(x_lru, g_lru, conv_w, conv_b, w_ext, lru_lam, h0):
    b, t_len, _ = x_lru.shape
    chunk = 64 if t_len >= 2048 else 32
    pitch = chunk + SUBLANES // 2
    n_chunks = t_len // chunk
    rows = min(2048, t_len)
    blocks_per_step = N_LRU_BLOCKS if t_len <= 512 else 1
    width = blocks_per_step * LRU_BLOCK
    seq_spec = pl.BlockSpec((None, t_len, width), lambda bi, n: (bi, 0, n))
    st_spec = pl.BlockSpec((None, 2, width), lambda bi, n: (bi, 0, n))
    scan_buf = pltpu.VMEM((blocks_per_step, n_chunks * pitch, LRU_BLOCK), F32)
    return pl.pallas_call(
        functools.partial(_lru_kernel, blocks_per_step=blocks_per_step,
                          t_len=t_len, chunk=chunk, pitch=pitch, rows=rows),
        grid=(b, N_LRU_BLOCKS // blocks_per_step),
        in_specs=[
            seq_spec, seq_spec,
            pl.BlockSpec((4, width), lambda bi, n: (0, n)),
            pl.BlockSpec((1, width), lambda bi, n: (0, n)),
            pl.BlockSpec((blocks_per_step, 2 * LRU_BLOCK, 4 * LRU_BLOCK), lambda bi, n: (n, 0, 0)),
            pl.BlockSpec((2, width), lambda bi, n: (0, n)),
            st_spec,
        ],
        out_specs=[seq_spec, st_spec],
        out_shape=[jax.ShapeDtypeStruct((b, t_len, D_LRU), BF16),
                   jax.ShapeDtypeStruct((b, 2, D_LRU), F32)],
        scratch_shapes=[
            pltpu.VMEM((blocks_per_step, t_len + 2 * SUBLANES, LRU_BLOCK), F32),
            scan_buf, scan_buf, scan_buf, scan_buf,
            pltpu.VMEM((blocks_per_step, n_chunks, LRU_BLOCK), F32),
            pltpu.VMEM((blocks_per_step, n_chunks, LRU_BLOCK), F32),
        ],
        compiler_params=pltpu.CompilerParams(
            dimension_semantics=("arbitrary",) * 2, vmem_limit_bytes=V7X_VMEM_LIMIT_BYTES),
        name="rglru",
    )(x_lru, g_lru, conv_w, conv_b, w_ext, lru_lam, h0)


def _outproj_kernel(att_ref, lru_ref, x_ref, gate_ref, wa_ref, wl_ref, gpost_ref, y_ref):
    o = _dot(att_ref[...], wa_ref[...]) + _dot(lru_ref[...], wl_ref[...])
    ms = jnp.mean(o * o, axis=-1, keepdims=True)
    n = o * lax.rsqrt(ms + EPS) * gpost_ref[...]
    y_ref[...] = x_ref[...] + gate_ref[...] * n


def _outproj(att, lru, x, gate, w_att, w_lru, g_post, *, t_len, tm=512):
    n_tok = x.shape[0]
    tm = min(tm, t_len)
    tiles_per_batch = t_len // tm
    nb = gate.shape[0]
    if nb == 1:
        mod_map = lambda i: (0, 0, 0)
    else:
        mod_map = lambda i: (i // tiles_per_batch, 0, 0)
    tok_spec = pl.BlockSpec((tm, D_MODEL), lambda i: (i, 0))
    w_spec = pl.BlockSpec((D_MODEL, D_MODEL), lambda i: (0, 0))
    return pl.pallas_call(
        _outproj_kernel,
        grid=(n_tok // tm,),
        in_specs=[tok_spec, tok_spec, tok_spec,
                  pl.BlockSpec((None, 1, D_MODEL), mod_map),
                  w_spec, w_spec,
                  pl.BlockSpec((1, D_MODEL), lambda i: (0, 0))],
        out_specs=tok_spec,
        out_shape=jax.ShapeDtypeStruct((n_tok, D_MODEL), F32),
        compiler_params=pltpu.CompilerParams(
            dimension_semantics=("arbitrary",), vmem_limit_bytes=V7X_VMEM_LIMIT_BYTES),
        name="outproj",
    )(att, lru, x, gate, w_att, w_lru, g_post)


def _sublayer(x, scale, shift, gate, cache_k, cache_v, h0, use_rope, w):
    b, t_len, _ = x.shape
    xf = x.reshape(b * t_len, D_MODEL)
    emit_kv = cache_k is None
    outs = _inproj(xf, scale, shift, w["g_pre"], w["w_in"], t_len=t_len,
                   use_rope=use_rope, emit_kv_f32=emit_kv)
    qt, vt = outs[0], outs[2]
    k, g_att, x_lru, g_lru = [outs[i].reshape(b, t_len, D_MODEL) for i in (1, 3, 4, 5)]
    att = _attention(w["lam"], w["g_subln"], qt, k, vt, g_att, cache_k, cache_v)
    lru, state = _lru(x_lru, g_lru, w["conv_w"], w["conv_b"], w["w_ext"], w["lru_lam"], h0)
    y = _outproj(att.reshape(b * t_len, D_ATT), lru.reshape(b * t_len, D_LRU), xf, gate,
                 w["w_out_att"], w["w_out_lru"], w["g_post"], t_len=t_len)
    y = y.reshape(b, t_len, D_MODEL)
    if emit_kv:
        return y, outs[6], outs[7], state
    return y, None, None, state


def kernel(x_prompt, x_sample, cache_k, cache_v, state_lru, c, c_ctx, w_ada, b_ada, g_pre, w_in, lambda_q1, lambda_k1, lambda_q2, lambda_k2, g_subln, conv_w, conv_b, w_rgate, b_rgate, w_igate, b_igate, lru_lambda, w_out, g_post):
    bp, seq, _ = x_prompt.shape
    bd, kc = cache_k.shape[0], cache_k.shape[2]
    assert cache_k.shape[1] == 1 and w_in.shape[0] == 1, "single-layer step only"
    l = 0

    cond = jnp.concatenate([c, c_ctx[None, :], jnp.zeros((SUBLANES - bd - 1, D_MODEL), F32)], axis=0)
    mod = _modulation(cond, w_ada[l], b_ada[l][None, :])
    shift, scale, gate = [mod[:, i * D_MODEL:(i + 1) * D_MODEL] for i in range(3)]

    def rows(a, lo, hi):
        return a[lo:hi][:, None, :]

    w = {
        "g_pre": g_pre[l][None, :],
        "w_in": w_in[l].astype(BF16),
        "lam": jnp.stack([lambda_q1[l], lambda_k1[l], lambda_q2[l], lambda_k2[l]], axis=0),
        "g_subln": g_subln[l][None, :],
        "conv_w": conv_w[l],
        "conv_b": conv_b[l][None, :],
        "w_ext": _lru_gate_operand(w_rgate[l], w_igate[l], b_rgate[l], b_igate[l]),
        "lru_lam": lru_lambda[l],
        "w_out_att": w_out[l, :D_ATT].astype(BF16),
        "w_out_lru": w_out[l, D_ATT:].astype(BF16),
        "g_post": g_post[l][None, :],
    }

    y_p, new_k, new_v, st_p = _sublayer(
        x_prompt, rows(scale, bd, bd + 1), rows(shift, bd, bd + 1), rows(gate, bd, bd + 1),
        None, None, jnp.zeros((bp, 2, D_LRU), F32), False, w)

    y_s, _, _, _ = _sublayer(
        x_sample, rows(scale, 0, bd), rows(shift, 0, bd), rows(gate, 0, bd),
        cache_k.reshape(bd, kc * N_HEADS, HEAD_W), cache_v.reshape(bd, kc * N_HEADS, HEAD_W),
        state_lru.reshape(bd, 2, D_LRU), True, w)

    new_k = new_k.reshape(bp, 1, seq, N_HEADS, HEAD_W)
    new_v = new_v.reshape(bp, 1, seq, N_HEADS, HEAD_W)
    return (y_p, y_s, new_k, new_v, st_p[:, None])
```

```python
import functools
import math

import jax
import jax.numpy as jnp
import numpy as np
from jax import lax
from jax.experimental import pallas as pl
from jax.experimental.pallas import tpu as pltpu

F32 = jnp.float32
BF16 = jnp.bfloat16

D_MODEL = 1024
GRID_W = 64
N_HEADS = 8
DIFF_HEAD_DIM = 64
HEAD_W = 2 * DIFF_HEAD_DIM
D_ATT = N_HEADS * HEAD_W
D_LRU = 1024
N_LRU_BLOCKS = 8
LRU_BLOCK = D_LRU // N_LRU_BLOCKS
LRU_C = 8.0
N_GROUPS = 6
ROPE_BASE = 10000.0
EPS = 1e-6
LAM_INIT = 0.8 - 0.6 * math.exp(-0.3 * 0)
LOG2_E = math.log2(math.e)

V7X_VMEM_LIMIT_BYTES = 56 * 1024 * 1024
SUBLANES = 8
ROPE_SWAP = DIFF_HEAD_DIM // 4
BF16_ROWS_PER_VREG = 16
ACC_ROWS = HEAD_W + BF16_ROWS_PER_VREG
CACHE_CHUNK = 256
BIAS_SPLIT = 3


def _silu(x):
    return x * jax.nn.sigmoid(x)


def _dot(a, b):
    return jnp.dot(a, b, preferred_element_type=F32)


def _dot_nt(a, b):
    return lax.dot_general(a, b, (((1,), (1,)), ((), ())), preferred_element_type=F32)


def _mod_kernel(cond_ref, w_ref, b_ref, o_ref):
    s = _silu(cond_ref[...])
    o_ref[...] = _dot(s.astype(BF16), w_ref[...].astype(BF16)) + b_ref[...]


def _modulation(cond, w_ada, b_ada):
    n = cond.shape[0]
    return pl.pallas_call(
        _mod_kernel,
        grid=(3,),
        in_specs=[
            pl.BlockSpec((n, D_MODEL), lambda j: (0, 0)),
            pl.BlockSpec((D_MODEL, D_MODEL), lambda j: (0, j)),
            pl.BlockSpec((1, D_MODEL), lambda j: (0, j)),
        ],
        out_specs=pl.BlockSpec((n, D_MODEL), lambda j: (0, j)),
        out_shape=jax.ShapeDtypeStruct((n, 3 * D_MODEL), F32),
        name="modulation",
    )(cond, w_ada, b_ada)


def _rope_tables(t_len):
    half = DIFF_HEAD_DIM // 2
    nf = half // 2
    t = np.arange(t_len)
    row = (t // GRID_W).astype(np.float32)
    col = (t % GRID_W).astype(np.float32)
    inv = (ROPE_BASE ** (-np.arange(nf, dtype=np.float32) * 2.0 / half)).astype(np.float32)
    lane = np.arange(HEAD_W) % DIFF_HEAD_DIM
    use_row = lane < half
    freq = (lane % half) % nf
    first = (lane % half) < nf
    pos = np.where(use_row[None, :], row[:, None], col[:, None]).astype(np.float32)
    ang = (pos * inv[freq][None, :]).astype(np.float32).astype(np.float64)
    cos = np.cos(ang).astype(np.float32)
    sin = (np.sin(ang) * np.where(first, -1.0, 1.0)[None, :]).astype(np.float32)
    return jnp.asarray(cos), jnp.asarray(sin)


def _inproj_kernel(*refs, use_rope, emit_kv_f32):
    x_ref, scale_ref, shift_ref, gpre_ref, w_ref = refs[:5]
    refs = refs[5:]
    if use_rope:
        cos_ref, sin_ref = refs[:2]
        refs = refs[2:]
    q_ref, k_ref, v_ref, gatt_ref, xlru_ref, glru_ref = refs[:6]
    refs = refs[6:]
    if emit_kv_f32:
        kf_ref, vf_ref = refs

    x = x_ref[...]
    ms = jnp.mean(x * x, axis=-1, keepdims=True)
    y = x * lax.rsqrt(ms + EPS) * gpre_ref[...]
    h = (y * (1.0 + scale_ref[...]) + shift_ref[...]).astype(BF16)

    def proj(g):
        return _dot(h, w_ref[:, g * D_MODEL:(g + 1) * D_MODEL])

    if use_rope:
        cos = cos_ref[...]
        sin = sin_ref[...]
        lane = lax.broadcasted_iota(jnp.int32, cos.shape, 1)
        take_next = (lane % (2 * ROPE_SWAP)) < ROPE_SWAP

        def rope(p):
            outs = []
            for hd in range(N_HEADS):
                xh = p[:, hd * HEAD_W:(hd + 1) * HEAD_W]
                partner = jnp.where(take_next,
                                    pltpu.roll(xh, HEAD_W - ROPE_SWAP, 1),
                                    pltpu.roll(xh, ROPE_SWAP, 1))
                outs.append(xh * cos + partner * sin)
            return jnp.concatenate(outs, axis=-1)
    else:
        def rope(p):
            return p

    q = rope(proj(0)) * (LOG2_E / math.sqrt(DIFF_HEAD_DIM))
    q_ref[...] = q.T.astype(BF16)
    k = rope(proj(1))
    k_ref[...] = k.astype(BF16)
    v = proj(2)
    vt = v.T
    for hd in range(N_HEADS):
        v_ref[hd] = vt[hd * HEAD_W:(hd + 1) * HEAD_W].astype(BF16)
    if emit_kv_f32:
        kf_ref[...] = k
        vf_ref[...] = v
    gatt_ref[...] = proj(3)
    xlru_ref[...] = proj(4)
    glru_ref[...] = proj(5)


def _inproj(x, scale, shift, g_pre, w_in_bf16, *, t_len, use_rope, emit_kv_f32, tm=512):
    n_tok = x.shape[0]
    tm = min(tm, t_len)
    tiles_per_batch = t_len // tm
    nb = scale.shape[0]
    if nb == 1:
        mod_map = lambda i: (0, 0, 0)
    else:
        mod_map = lambda i: (i // tiles_per_batch, 0, 0)
    tok_spec = pl.BlockSpec((tm, D_MODEL), lambda i: (i, 0))
    in_specs = [
        tok_spec,
        pl.BlockSpec((None, 1, D_MODEL), mod_map),
        pl.BlockSpec((None, 1, D_MODEL), mod_map),
        pl.BlockSpec((1, D_MODEL), lambda i: (0, 0)),
        pl.BlockSpec((D_MODEL, N_GROUPS * D_MODEL), lambda i: (0, 0), pipeline_mode=pl.Buffered(1)),
    ]
    args = [x, scale, shift, g_pre, w_in_bf16]
    if use_rope:
        cos, sin = _rope_tables(t_len)
        rope_spec = pl.BlockSpec((tm, HEAD_W), lambda i: (i % tiles_per_batch, 0))
        in_specs += [rope_spec, rope_spec]
        args += [cos, sin]
    n_batch = n_tok // t_len
    kc = _key_chunk(t_len)
    tiles_per_chunk = kc // tm
    out_specs = [
        pl.BlockSpec((None, D_MODEL, tm), lambda i: (i // tiles_per_batch, 0, i % tiles_per_batch)),
        tok_spec,
        pl.BlockSpec((None, N_HEADS, None, HEAD_W, tm),
                     lambda i: (i // tiles_per_batch, 0, (i % tiles_per_batch) // tiles_per_chunk, 0,
                                i % tiles_per_chunk)),
        tok_spec, tok_spec, tok_spec,
    ]
    out_shape = [
        jax.ShapeDtypeStruct((n_batch, D_MODEL, t_len), BF16),
        jax.ShapeDtypeStruct((n_tok, D_MODEL), BF16),
        jax.ShapeDtypeStruct((n_batch, N_HEADS, t_len // kc, HEAD_W, kc), BF16),
    ] + [jax.ShapeDtypeStruct((n_tok, D_MODEL), F32)] * 3
    if emit_kv_f32:
        out_specs += [tok_spec] * 2
        out_shape += [jax.ShapeDtypeStruct((n_tok, D_MODEL), F32)] * 2
    return pl.pallas_call(
        functools.partial(_inproj_kernel, use_rope=use_rope, emit_kv_f32=emit_kv_f32),
        grid=(n_tok // tm,),
        in_specs=in_specs,
        out_specs=out_specs,
        out_shape=out_shape,
        compiler_params=pltpu.CompilerParams(
            dimension_semantics=("arbitrary",), vmem_limit_bytes=V7X_VMEM_LIMIT_BYTES),
        name="inproj_rope" if use_rope else "inproj",
    )(*args)


def _key_chunk(t_len):
    return min(512, t_len)


def _attn_kernel(*refs, has_cache, n_chunks, kc, heads_per_step):
    lam_ref, gsub_ref, qt_ref, k_ref, vt_ref, gatt_ref = refs[:6]
    refs = refs[6:]
    cache_refs = None
    if has_cache:
        cache_refs = refs[:2]
        refs = refs[2:]
    o_ref, acc_ref, s_ref, e_ref = refs

    lp = lam_ref[...]
    lam = (jnp.exp(jnp.sum(lp[0:1] * lp[1:2], axis=-1, keepdims=True))
           - jnp.exp(jnp.sum(lp[2:3] * lp[3:4], axis=-1, keepdims=True)) + LAM_INIT)
    g_subln = gsub_ref[...]

    if n_chunks == 1 and not has_cache:
        _attn_single_chunk(lam, g_subln, qt_ref, k_ref, vt_ref, gatt_ref, o_ref, s_ref, e_ref,
                           n_heads=heads_per_step)
        return
    for hh in range(heads_per_step):
        lanes = slice(hh * HEAD_W, (hh + 1) * HEAD_W)
        _attn_accumulate(qt_ref.at[lanes, :], k_ref.at[:, lanes], vt_ref.at[hh], cache_refs,
                         pl.program_id(1) * heads_per_step + hh, acc_ref, s_ref, e_ref,
                         n_chunks=n_chunks, kc=kc)
        _attn_finalize(lam, g_subln, acc_ref[0], acc_ref[1], gatt_ref.at[:, lanes], o_ref.at[:, lanes])


def _query_maps(qt):
    row = lax.broadcasted_iota(jnp.int32, qt.shape, 0)
    zero = jnp.zeros_like(qt)
    return jnp.where(row < DIFF_HEAD_DIM, qt, zero), jnp.where(row >= DIFF_HEAD_DIM, qt, zero)


def _with_ones(vt_chunk):
    ones = jnp.ones((ACC_ROWS - HEAD_W, vt_chunk.shape[1]), BF16)
    return jnp.concatenate([vt_chunk, ones], axis=0)


def _attn_single_chunk(lam, g_subln, qt_ref, k_ref, vt_ref, gatt_ref, o_ref, s_ref, e_ref, *, n_heads):
    maxes = {}
    for t in range(n_heads + 2):
        if t < n_heads:
            lanes = slice(t * HEAD_W, (t + 1) * HEAD_W)
            keys = k_ref[:, lanes]
            q_maps = _query_maps(qt_ref[lanes, :])
            head_max = []
            for idx in range(2):
                st = _dot(keys, q_maps[idx])
                s_ref[t % 2, idx] = st
                head_max.append(jnp.max(st, axis=0, keepdims=True))
            maxes[t] = head_max
        if 1 <= t <= n_heads:
            h = t - 1
            for idx in range(2):
                e_ref[h % 2, idx] = jnp.exp2(s_ref[h % 2, idx] - maxes[h][idx]).astype(BF16)
        if t >= 2:
            h = t - 2
            lanes = slice(h * HEAD_W, (h + 1) * HEAD_W)
            vt_ext = _with_ones(vt_ref[h, 0])
            acc1, acc2 = [_dot(vt_ext, e_ref[h % 2, idx]) for idx in range(2)]
            _attn_finalize(lam, g_subln, acc1, acc2, gatt_ref.at[:, lanes], o_ref.at[:, lanes])


def _attn_finalize(lam, g_subln, acc1, acc2, gatt_ref, o_ref):
    o1 = acc1[:HEAD_W] * (1.0 / acc1[HEAD_W:HEAD_W + 1])
    o2 = acc2[:HEAD_W] * (1.0 / acc2[HEAD_W:HEAD_W + 1])
    ot = o1 - lam * o2
    ms = jnp.mean(ot * ot, axis=0, keepdims=True)
    on = (ot * lax.rsqrt(ms + EPS)).T * (g_subln * (1.0 - LAM_INIT))
    o_ref[...] = (on * _silu(gatt_ref[...])).astype(BF16)


def _attn_accumulate(qt_ref, k_ref, vt_ref, cache_refs, head, acc_ref, s_ref, e_ref, *, n_chunks, kc):
    has_cache = cache_refs is not None
    tq = qt_ref.shape[1]
    q_maps = _query_maps(qt_ref[...])

    acc_ref[...] = jnp.zeros(acc_ref.shape, F32)

    n_cache = 0
    if has_cache:
        kc_ref, vc_ref = cache_refs
        cache_len = kc_ref.shape[0] // N_HEADS
        cache_chunk = min(CACHE_CHUNK, cache_len)
        n_cache = cache_len // cache_chunk
    n_total = n_chunks + n_cache

    def is_cache(c):
        return isinstance(c, int) and c < n_cache

    def cache_rows(c):
        return pl.ds(head + c * cache_chunk * N_HEADS, cache_chunk, stride=N_HEADS)

    def keys_of(c):
        if is_cache(c):
            return kc_ref[cache_rows(c), :].astype(BF16)
        start = (c - n_cache) * kc
        return k_ref[pl.ds(start if isinstance(c, int) else pl.multiple_of(start, kc), kc), :]

    def values_of(c):
        if is_cache(c):
            return vc_ref[cache_rows(c), :].T.astype(BF16)
        return vt_ref[c - n_cache]

    def scores(c, slot):
        k_chunk = keys_of(c)
        maxes = []
        for idx in range(2):
            st = _dot(k_chunk, q_maps[idx])
            s_ref[slot, idx, :k_chunk.shape[0], :] = st
            maxes.append(jnp.max(st, axis=0, keepdims=True))
        return tuple(maxes)

    def probs(n_keys, slot, ms, maxes):
        new_ms, alphas = [], []
        for idx in range(2):
            m_new = jnp.maximum(ms[idx], maxes[idx])
            alphas.append(jnp.exp2(ms[idx] - m_new))
            e_ref[slot, idx, :n_keys, :] = jnp.exp2(s_ref[slot, idx, :n_keys, :] - m_new).astype(BF16)
            new_ms.append(m_new)
        return tuple(new_ms), tuple(alphas)

    def values(c, slot, alphas):
        vt_ext = _with_ones(values_of(c))
        n_keys = vt_ext.shape[1]
        for idx in range(2):
            acc_ref[idx] = acc_ref[idx] * alphas[idx] + _dot(vt_ext, e_ref[slot, idx, :n_keys, :])

    def n_keys_of(c):
        return cache_chunk if is_cache(c) else kc

    def step(i, slot, ms, maxes_next, alphas_cur):
        maxes_after = scores(i + 2, slot)
        ms, alphas_next = probs(n_keys_of(i + 1), 1 - slot, ms, maxes_next)
        values(i, slot, alphas_cur)
        return ms, maxes_after, alphas_next

    m0 = jnp.full((1, tq), -jnp.inf, F32)
    ms = (m0, m0)
    maxes_cur = scores(0, 0)
    if n_total == 1:
        ms, alphas_cur = probs(n_keys_of(0), 0, ms, maxes_cur)
        values(0, 0, alphas_cur)
    else:
        maxes_next = scores(1, 1)
        ms, alphas_cur = probs(n_keys_of(0), 0, ms, maxes_cur)
        n_steps = n_total - 2
        first_loop = n_cache
        for i in range(min(first_loop, n_steps)):
            ms, maxes_next, alphas_cur = step(i, i % 2, ms, maxes_next, alphas_cur)
        n_pairs = max(n_steps - first_loop, 0) // 2

        def pair(t, carry):
            ms, maxes_next, alphas_cur = carry
            i = first_loop + 2 * t
            ms, maxes_next, alphas_cur = step(i, first_loop % 2, ms, maxes_next, alphas_cur)
            return step(i + 1, (first_loop + 1) % 2, ms, maxes_next, alphas_cur)

        ms, maxes_next, alphas_cur = lax.fori_loop(0, n_pairs, pair, (ms, maxes_next, alphas_cur))
        for i in range(first_loop + 2 * n_pairs, n_steps):
            ms, maxes_next, alphas_cur = step(i, i % 2, ms, maxes_next, alphas_cur)
        last = n_total - 1
        ms, alphas_last = probs(n_keys_of(last), last % 2, ms, maxes_next)
        values(last - 1, (last - 1) % 2, alphas_cur)
        values(last, last % 2, alphas_last)


def _attention(lam_params, g_subln, qt, k, vt, g_att, cache_k=None, cache_v=None, *, tq=2048):
    b, t_len, _ = k.shape
    has_cache = cache_k is not None
    n_chunks, kc = vt.shape[2], vt.shape[4]
    tq = min(tq, t_len)
    heads_per_step = N_HEADS if n_chunks == 1 and not has_cache else 1
    width = heads_per_step * HEAD_W
    s_rows = max(kc, cache_k.shape[1] // N_HEADS) if has_cache else kc
    tok_spec = pl.BlockSpec((None, tq, width), lambda bi, h, qi: (bi, qi, h))
    in_specs = [
        pl.BlockSpec(lam_params.shape, lambda bi, h, qi: (0, 0)),
        pl.BlockSpec((1, HEAD_W), lambda bi, h, qi: (0, 0)),
        pl.BlockSpec((None, width, tq), lambda bi, h, qi: (bi, h, qi)),
        pl.BlockSpec((None, t_len, width), lambda bi, h, qi: (bi, 0, h)),
        pl.BlockSpec((None, heads_per_step, n_chunks, HEAD_W, kc), lambda bi, h, qi: (bi, h, 0, 0, 0)),
        tok_spec,
    ]
    args = [lam_params, g_subln, qt, k, vt, g_att]
    if has_cache:
        c_spec = pl.BlockSpec((None,) + cache_k.shape[1:], lambda bi, h, qi: (bi, 0, 0))
        in_specs += [c_spec, c_spec]
        args += [cache_k, cache_v]
    return pl.pallas_call(
        functools.partial(_attn_kernel, has_cache=has_cache, n_chunks=n_chunks, kc=kc,
                          heads_per_step=heads_per_step),
        grid=(b, N_HEADS // heads_per_step, t_len // tq),
        in_specs=in_specs,
        out_specs=tok_spec,
        out_shape=jax.ShapeDtypeStruct((b, t_len, D_ATT), BF16),
        scratch_shapes=[pltpu.VMEM((2, ACC_ROWS, tq), F32),
                        pltpu.VMEM((2, 2, s_rows, tq), F32),
                        pltpu.VMEM((2, 2, s_rows, tq), BF16)],
        compiler_params=pltpu.CompilerParams(
            dimension_semantics=("arbitrary",) * 3, vmem_limit_bytes=V7X_VMEM_LIMIT_BYTES),
        name="diff_attn_cache" if has_cache else "diff_attn",
    )(*args)


def _lru_kernel(x_ref, g_ref, cw_ref, cb_ref, w_ref, lam_ref, h0_ref,
                o_ref, st_ref,
                xp, a_f, b_f, a_b, b_b, hin_f, hin_b,
                *, blocks_per_step, t_len, chunk, pitch, rows):
    blocks = range(blocks_per_step)
    lanes = [slice(j * LRU_BLOCK, (j + 1) * LRU_BLOCK) for j in blocks]
    g_f, p_f, g_b, p_b = b_f, a_f, b_b, a_b
    n_chunks = t_len // chunk
    chunks_per_tile = rows // chunk

    pad = jnp.zeros((SUBLANES, LRU_BLOCK), F32)
    for j in blocks:
        xp[j, pl.ds(0, SUBLANES), :] = pad
        xp[j, pl.ds(SUBLANES, t_len), :] = x_ref[:, lanes[j]]
        xp[j, pl.ds(SUBLANES + t_len, SUBLANES), :] = pad

    lam = lam_ref[...]
    sp = jnp.maximum(-lam, 0.0) + jnp.log1p(jnp.exp(-jnp.abs(lam)))
    half_scale = (-0.5 * LRU_C) * sp
    cw = cw_ref[...]
    cb = cb_ref[...]
    lane = lax.broadcasted_iota(jnp.int32, (rows, LRU_BLOCK), 1)
    bias_taps = jnp.where(lane < BIAS_SPLIT, 1.0, 0.0).astype(BF16)

    def gate_tile(r, carry):
        t0 = pl.multiple_of(r * rows, SUBLANES)
        for j in blocks:
            u = cb[:, lanes[j]]
            for tap in range(4):
                u = u + xp[j, pl.ds(t0 + SUBLANES - 1 + tap, rows), :] * cw[tap:tap + 1, lanes[j]]
            zh = _dot(jnp.concatenate([u.astype(BF16), bias_taps], axis=1), w_ref[j])
            u_half = 0.5 * u
            for d, (a_s, b_s) in enumerate(((a_f, b_f), (a_b, b_b))):
                scale = half_scale[d:d + 1, lanes[j]]
                tr = jnp.tanh(zh[:, (2 * d) * LRU_BLOCK:(2 * d + 1) * LRU_BLOCK])
                ti = jnp.tanh(zh[:, (2 * d + 1) * LRU_BLOCK:(2 * d + 2) * LRU_BLOCK])
                log_a = scale + scale * tr
                a = jnp.exp(log_a)
                gain_sq = jnp.tanh(log_a) * (-1.0 - a * a)
                gain = jnp.where(gain_sq > 0.0, gain_sq * lax.rsqrt(gain_sq), 0.0)
                bb = gain * (u_half + u_half * ti)
                for cc in range(chunks_per_tile):
                    dst = pl.multiple_of(r * (chunks_per_tile * pitch), SUBLANES) + cc * pitch
                    a_s[j, pl.ds(dst, chunk), :] = a[cc * chunk:(cc + 1) * chunk]
                    b_s[j, pl.ds(dst, chunk), :] = bb[cc * chunk:(cc + 1) * chunk]
        return carry

    lax.fori_loop(0, t_len // rows, gate_tile, 0)

    def local_step(l, carry):
        out = []
        for j in blocks:
            hf, pf, hb, pb = carry[4 * j:4 * j + 4]
            rf = pl.ds(l, n_chunks, stride=pitch)
            a = a_f[j, rf, :]
            hf = a * hf + b_f[j, rf, :]
            pf = a * pf
            g_f[j, rf, :] = hf
            p_f[j, rf, :] = pf
            rb = pl.ds(chunk - 1 - l, n_chunks, stride=pitch)
            a = a_b[j, rb, :]
            hb = a * hb + b_b[j, rb, :]
            pb = a * pb
            g_b[j, rb, :] = hb
            p_b[j, rb, :] = pb
            out += [hf, pf, hb, pb]
        return tuple(out)

    z0 = jnp.zeros((n_chunks, LRU_BLOCK), F32)
    o0 = jnp.ones((n_chunks, LRU_BLOCK), F32)
    chunk_maps = lax.fori_loop(0, chunk, local_step, (z0, o0, z0, o0) * blocks_per_step, unroll=4)

    h0 = h0_ref[...]
    chunk_id = lax.broadcasted_iota(jnp.int32, (n_chunks, LRU_BLOCK), 0)

    def compose(end, decay, towards_higher):
        shift = 1
        while shift < n_chunks:
            if towards_higher:
                valid = chunk_id >= shift
                amount = shift
            else:
                valid = chunk_id < n_chunks - shift
                amount = n_chunks - shift
            prev_end = jnp.where(valid, pltpu.roll(end, amount, 0), 0.0)
            prev_decay = jnp.where(valid, pltpu.roll(decay, amount, 0), 1.0)
            end = end + decay * prev_end
            decay = decay * prev_decay
            shift *= 2
        return end, decay

    for j in blocks:
        end_f, decay_f, end_b, decay_b = chunk_maps[4 * j:4 * j + 4]
        h0_f = h0[0:1, lanes[j]]
        h0_b = h0[1:2, lanes[j]]
        end_f, decay_f = compose(end_f, decay_f, True)
        after_f = end_f + decay_f * h0_f
        hin_f[j] = jnp.where(chunk_id == 0, h0_f, pltpu.roll(after_f, 1, 0))
        st_ref[0:1, lanes[j]] = after_f[n_chunks - 1:n_chunks]
        end_b, decay_b = compose(end_b, decay_b, False)
        after_b = end_b + decay_b * h0_b
        hin_b[j] = jnp.where(chunk_id == n_chunks - 1, h0_b, pltpu.roll(after_b, n_chunks - 1, 0))
        st_ref[1:2, lanes[j]] = after_b[0:1]

    def out_pair(cp, carry):
        for par in range(2):
            c = 2 * cp + par
            src = pl.ds(pl.multiple_of(cp * (2 * pitch), SUBLANES) + par * pitch, chunk)
            dst = pl.ds(pl.multiple_of(c * chunk, SUBLANES), chunk)
            for j in blocks:
                hf = g_f[j, src, :] + p_f[j, src, :] * hin_f[j, pl.ds(c, 1), :]
                hb = g_b[j, src, :] + p_b[j, src, :] * hin_b[j, pl.ds(c, 1), :]
                o_ref[dst, lanes[j]] = ((hf + hb) * _silu(g_ref[dst, lanes[j]])).astype(BF16)
        return carry

    lax.fori_loop(0, n_chunks // 2, out_pair, 0, unroll=2)


def _lru_gate_operand(w_r, w_i, b_r, b_i):
    weights = 0.5 * jnp.concatenate([w_r[0], w_i[0], w_r[1], w_i[1]], axis=-1)
    bias = 0.5 * jnp.concatenate(
        [bb.reshape(N_LRU_BLOCKS, 1, LRU_BLOCK) for bb in (b_r[0], b_i[0], b_r[1], b_i[1])], axis=-1)
    parts, rest = [], bias
    for _ in range(BIAS_SPLIT):
        gamma = rest * (2.0 ** 16 + 1.0)
        part = gamma - (gamma - rest)
        parts.append(part)
        rest = rest - part
    bias_rows = jnp.pad(jnp.concatenate(parts, axis=1), ((0, 0), (0, LRU_BLOCK - BIAS_SPLIT), (0, 0)))
    return jnp.concatenate([weights, bias_rows], axis=1).astype(BF16)


def _lru(x_lru, g_lru, conv_w, conv_b, w_ext, lru_lam, h0):
    b, t_len, _ = x_lru.shape
    chunk = 64 if t_len >= 2048 else 32
    pitch = chunk + SUBLANES // 2
    n_chunks = t_len // chunk
    rows = min(2048, t_len)
    blocks_per_step = N_LRU_BLOCKS if t_len <= 512 else 1
    width = blocks_per_step * LRU_BLOCK
    seq_spec = pl.BlockSpec((None, t_len, width), lambda bi, n: (bi, 0, n))
    st_spec = pl.BlockSpec((None, 2, width), lambda bi, n: (bi, 0, n))
    scan_buf = pltpu.VMEM((blocks_per_step, n_chunks * pitch, LRU_BLOCK), F32)
    return pl.pallas_call(
        functools.partial(_lru_kernel, blocks_per_step=blocks_per_step,
                          t_len=t_len, chunk=chunk, pitch=pitch, rows=rows),
        grid=(b, N_LRU_BLOCKS // blocks_per_step),
        in_specs=[
            seq_spec, seq_spec,
            pl.BlockSpec((4, width), lambda bi, n: (0, n)),
            pl.BlockSpec((1, width), lambda bi, n: (0, n)),
            pl.BlockSpec((blocks_per_step, 2 * LRU_BLOCK, 4 * LRU_BLOCK), lambda bi, n: (n, 0, 0)),
            pl.BlockSpec((2, width), lambda bi, n: (0, n)),
            st_spec,
        ],
        out_specs=[seq_spec, st_spec],
        out_shape=[jax.ShapeDtypeStruct((b, t_len, D_LRU), BF16),
                   jax.ShapeDtypeStruct((b, 2, D_LRU), F32)],
        scratch_shapes=[
            pltpu.VMEM((blocks_per_step, t_len + 2 * SUBLANES, LRU_BLOCK), F32),
            scan_buf, scan_buf, scan_buf, scan_buf,
            pltpu.VMEM((blocks_per_step, n_chunks, LRU_BLOCK), F32),
            pltpu.VMEM((blocks_per_step, n_chunks, LRU_BLOCK), F32),
        ],
        compiler_params=pltpu.CompilerParams(
            dimension_semantics=("arbitrary",) * 2, vmem_limit_bytes=V7X_VMEM_LIMIT_BYTES),
        name="rglru",
    )(x_lru, g_lru, conv_w, conv_b, w_ext, lru_lam, h0)


def _outproj_kernel(att_ref, lru_ref, x_ref, gate_ref, wa_ref, wl_ref, gpost_ref, y_ref):
    o = _dot(att_ref[...], wa_ref[...]) + _dot(lru_ref[...], wl_ref[...])
    ms = jnp.mean(o * o, axis=-1, keepdims=True)
    n = o * lax.rsqrt(ms + EPS) * gpost_ref[...]
    y_ref[...] = x_ref[...] + gate_ref[...] * n


def _outproj(att, lru, x, gate, w_att, w_lru, g_post, *, t_len, tm=1024):
    n_tok = x.shape[0]
    nb = gate.shape[0]
    if nb == 1:
        tm = min(tm, n_tok)
        mod_map = lambda i: (0, 0, 0)
    else:
        tm = min(tm, t_len)
        tiles_per_batch = t_len // tm
        mod_map = lambda i: (i // tiles_per_batch, 0, 0)
    tok_spec = pl.BlockSpec((tm, D_MODEL), lambda i: (i, 0))
    w_spec = pl.BlockSpec((D_MODEL, D_MODEL), lambda i: (0, 0))
    return pl.pallas_call(
        _outproj_kernel,
        grid=(n_tok // tm,),
        in_specs=[tok_spec, tok_spec, tok_spec,
                  pl.BlockSpec((None, 1, D_MODEL), mod_map),
                  w_spec, w_spec,
                  pl.BlockSpec((1, D_MODEL), lambda i: (0, 0))],
        out_specs=tok_spec,
        out_shape=jax.ShapeDtypeStruct((n_tok, D_MODEL), F32),
        compiler_params=pltpu.CompilerParams(
            dimension_semantics=("arbitrary",), vmem_limit_bytes=V7X_VMEM_LIMIT_BYTES),
        name="outproj",
    )(att, lru, x, gate, w_att, w_lru, g_post)


def _sublayer(x, scale, shift, gate, cache_k, cache_v, h0, use_rope, w):
    b, t_len, _ = x.shape
    xf = x.reshape(b * t_len, D_MODEL)
    emit_kv = cache_k is None
    outs = _inproj(xf, scale, shift, w["g_pre"], w["w_in"], t_len=t_len,
                   use_rope=use_rope, emit_kv_f32=emit_kv)
    qt, vt = outs[0], outs[2]
    k, g_att, x_lru, g_lru = [outs[i].reshape(b, t_len, D_MODEL) for i in (1, 3, 4, 5)]
    att = _attention(w["lam"], w["g_subln"], qt, k, vt, g_att, cache_k, cache_v)
    lru, state = _lru(x_lru, g_lru, w["conv_w"], w["conv_b"], w["w_ext"], w["lru_lam"], h0)
    y = _outproj(att.reshape(b * t_len, D_ATT), lru.reshape(b * t_len, D_LRU), xf, gate,
                 w["w_out_att"], w["w_out_lru"], w["g_post"], t_len=t_len)
    y = y.reshape(b, t_len, D_MODEL)
    if emit_kv:
        return y, outs[6], outs[7], state
    return y, None, None, state


def kernel(x_prompt, x_sample, cache_k, cache_v, state_lru, c, c_ctx, w_ada, b_ada, g_pre, w_in, lambda_q1, lambda_k1, lambda_q2, lambda_k2, g_subln, conv_w, conv_b, w_rgate, b_rgate, w_igate, b_igate, lru_lambda, w_out, g_post):
    bp, seq, _ = x_prompt.shape
    bd, kc = cache_k.shape[0], cache_k.shape[2]
    assert cache_k.shape[1] == 1 and w_in.shape[0] == 1, "single-layer step only"
    l = 0

    cond = jnp.concatenate([c, c_ctx[None, :], jnp.zeros((SUBLANES - bd - 1, D_MODEL), F32)], axis=0)
    mod = _modulation(cond, w_ada[l], b_ada[l][None, :])
    shift, scale, gate = [mod[:, i * D_MODEL:(i + 1) * D_MODEL] for i in range(3)]

    def rows(a, lo, hi):
        return a[lo:hi][:, None, :]

    w = {
        "g_pre": g_pre[l][None, :],
        "w_in": w_in[l].astype(BF16),
        "lam": jnp.stack([lambda_q1[l], lambda_k1[l], lambda_q2[l], lambda_k2[l]], axis=0),
        "g_subln": g_subln[l][None, :],
        "conv_w": conv_w[l],
        "conv_b": conv_b[l][None, :],
        "w_ext": _lru_gate_operand(w_rgate[l], w_igate[l], b_rgate[l], b_igate[l]),
        "lru_lam": lru_lambda[l],
        "w_out_att": w_out[l, :D_ATT].astype(BF16),
        "w_out_lru": w_out[l, D_ATT:].astype(BF16),
        "g_post": g_post[l][None, :],
    }

    y_p, new_k, new_v, st_p = _sublayer(
        x_prompt, rows(scale, bd, bd + 1), rows(shift, bd, bd + 1), rows(gate, bd, bd + 1),
        None, None, jnp.zeros((bp, 2, D_LRU), F32), False, w)

    y_s, _, _, _ = _sublayer(
        x_sample, rows(scale, 0, bd), rows(shift, 0, bd), rows(gate, 0, bd),
        cache_k.reshape(bd, kc * N_HEADS, HEAD_W), cache_v.reshape(bd, kc * N_HEADS, HEAD_W),
        state_lru.reshape(bd, 2, D_LRU), True, w)

    new_k = new_k.reshape(bp, 1, seq, N_HEADS, HEAD_W)
    new_v = new_v.reshape(bp, 1, seq, N_HEADS, HEAD_W)
    return (y_p, y_s, new_k, new_v, st_p[:, None])
```

```python
import functools
import math

import jax
import jax.numpy as jnp
import numpy as np
from jax import lax
from jax.experimental import pallas as pl
from jax.experimental.pallas import tpu as pltpu

F32 = jnp.float32
BF16 = jnp.bfloat16

D_MODEL = 1024
GRID_W = 64
N_HEADS = 8
DIFF_HEAD_DIM = 64
HEAD_W = 2 * DIFF_HEAD_DIM
D_ATT = N_HEADS * HEAD_W
D_LRU = 1024
N_LRU_BLOCKS = 8
LRU_BLOCK = D_LRU // N_LRU_BLOCKS
LRU_C = 8.0
N_GROUPS = 6
ROPE_BASE = 10000.0
EPS = 1e-6
LAM_INIT = 0.8 - 0.6 * math.exp(-0.3 * 0)
LOG2_E = math.log2(math.e)

V7X_VMEM_LIMIT_BYTES = 56 * 1024 * 1024
SUBLANES = 8
ROPE_SWAP = DIFF_HEAD_DIM // 4
BF16_ROWS_PER_VREG = 16
ACC_ROWS = HEAD_W + BF16_ROWS_PER_VREG
CACHE_CHUNK = 256
BIAS_SPLIT = 3


def _silu(x):
    return x * jax.nn.sigmoid(x)


def _dot(a, b):
    return jnp.dot(a, b, preferred_element_type=F32)


def _dot_nt(a, b):
    return lax.dot_general(a, b, (((1,), (1,)), ((), ())), preferred_element_type=F32)


def _mod_kernel(cond_ref, w_ref, b_ref, o_ref):
    s = _silu(cond_ref[...])
    o_ref[...] = _dot(s.astype(BF16), w_ref[...].astype(BF16)) + b_ref[...]


def _modulation(cond, w_ada, b_ada):
    n = cond.shape[0]
    return pl.pallas_call(
        _mod_kernel,
        grid=(3,),
        in_specs=[
            pl.BlockSpec((n, D_MODEL), lambda j: (0, 0)),
            pl.BlockSpec((D_MODEL, D_MODEL), lambda j: (0, j)),
            pl.BlockSpec((1, D_MODEL), lambda j: (0, j)),
        ],
        out_specs=pl.BlockSpec((n, D_MODEL), lambda j: (0, j)),
        out_shape=jax.ShapeDtypeStruct((n, 3 * D_MODEL), F32),
        name="modulation",
    )(cond, w_ada, b_ada)


def _rope_tables(t_len):
    half = DIFF_HEAD_DIM // 2
    nf = half // 2
    t = np.arange(t_len)
    row = (t // GRID_W).astype(np.float32)
    col = (t % GRID_W).astype(np.float32)
    inv = (ROPE_BASE ** (-np.arange(nf, dtype=np.float32) * 2.0 / half)).astype(np.float32)
    lane = np.arange(HEAD_W) % DIFF_HEAD_DIM
    use_row = lane < half
    freq = (lane % half) % nf
    first = (lane % half) < nf
    pos = np.where(use_row[None, :], row[:, None], col[:, None]).astype(np.float32)
    ang = (pos * inv[freq][None, :]).astype(np.float32).astype(np.float64)
    cos = np.cos(ang).astype(np.float32)
    sin = (np.sin(ang) * np.where(first, -1.0, 1.0)[None, :]).astype(np.float32)
    return jnp.asarray(cos), jnp.asarray(sin)


def _inproj_kernel(*refs, use_rope, emit_kv_f32):
    x_ref, scale_ref, shift_ref, gpre_ref, w_ref = refs[:5]
    refs = refs[5:]
    if use_rope:
        cos_ref, sin_ref = refs[:2]
        refs = refs[2:]
    q_ref, k_ref, v_ref, gatt_ref, xlru_ref, glru_ref = refs[:6]
    refs = refs[6:]
    if emit_kv_f32:
        kf_ref, vf_ref = refs

    x = x_ref[...]
    ms = jnp.mean(x * x, axis=-1, keepdims=True)
    y = x * lax.rsqrt(ms + EPS) * gpre_ref[...]
    h = (y * (1.0 + scale_ref[...]) + shift_ref[...]).astype(BF16)

    def proj(g):
        return _dot(h, w_ref[:, g * D_MODEL:(g + 1) * D_MODEL])

    if use_rope:
        cos = cos_ref[...]
        sin = sin_ref[...]
        lane = lax.broadcasted_iota(jnp.int32, cos.shape, 1)
        take_next = (lane % (2 * ROPE_SWAP)) < ROPE_SWAP

        def rope(p):
            outs = []
            for hd in range(N_HEADS):
                xh = p[:, hd * HEAD_W:(hd + 1) * HEAD_W]
                partner = jnp.where(take_next,
                                    pltpu.roll(xh, HEAD_W - ROPE_SWAP, 1),
                                    pltpu.roll(xh, ROPE_SWAP, 1))
                outs.append(xh * cos + partner * sin)
            return jnp.concatenate(outs, axis=-1)
    else:
        def rope(p):
            return p

    q = rope(proj(0)) * (LOG2_E / math.sqrt(DIFF_HEAD_DIM))
    q_ref[...] = q.T.astype(BF16)
    k = rope(proj(1))
    k_ref[...] = k.astype(BF16)
    v = proj(2)
    vt = v.T
    for hd in range(N_HEADS):
        v_ref[hd] = vt[hd * HEAD_W:(hd + 1) * HEAD_W].astype(BF16)
    if emit_kv_f32:
        kf_ref[...] = k
        vf_ref[...] = v
    gatt_ref[...] = proj(3)
    xlru_ref[...] = proj(4)
    glru_ref[...] = proj(5)


def _inproj(x, scale, shift, g_pre, w_in_bf16, *, t_len, use_rope, emit_kv_f32, tm=512):
    n_tok = x.shape[0]
    tm = min(tm, t_len)
    tiles_per_batch = t_len // tm
    nb = scale.shape[0]
    if nb == 1:
        mod_map = lambda i: (0, 0, 0)
    else:
        mod_map = lambda i: (i // tiles_per_batch, 0, 0)
    tok_spec = pl.BlockSpec((tm, D_MODEL), lambda i: (i, 0))
    in_specs = [
        tok_spec,
        pl.BlockSpec((None, 1, D_MODEL), mod_map),
        pl.BlockSpec((None, 1, D_MODEL), mod_map),
        pl.BlockSpec((1, D_MODEL), lambda i: (0, 0)),
        pl.BlockSpec((D_MODEL, N_GROUPS * D_MODEL), lambda i: (0, 0), pipeline_mode=pl.Buffered(1)),
    ]
    args = [x, scale, shift, g_pre, w_in_bf16]
    if use_rope:
        cos, sin = _rope_tables(t_len)
        rope_spec = pl.BlockSpec((tm, HEAD_W), lambda i: (i % tiles_per_batch, 0))
        in_specs += [rope_spec, rope_spec]
        args += [cos, sin]
    n_batch = n_tok // t_len
    kc = _key_chunk(t_len)
    tiles_per_chunk = kc // tm
    out_specs = [
        pl.BlockSpec((None, D_MODEL, tm), lambda i: (i // tiles_per_batch, 0, i % tiles_per_batch)),
        tok_spec,
        pl.BlockSpec((None, N_HEADS, None, HEAD_W, tm),
                     lambda i: (i // tiles_per_batch, 0, (i % tiles_per_batch) // tiles_per_chunk, 0,
                                i % tiles_per_chunk)),
        tok_spec, tok_spec, tok_spec,
    ]
    out_shape = [
        jax.ShapeDtypeStruct((n_batch, D_MODEL, t_len), BF16),
        jax.ShapeDtypeStruct((n_tok, D_MODEL), BF16),
        jax.ShapeDtypeStruct((n_batch, N_HEADS, t_len // kc, HEAD_W, kc), BF16),
    ] + [jax.ShapeDtypeStruct((n_tok, D_MODEL), F32)] * 3
    if emit_kv_f32:
        out_specs += [tok_spec] * 2
        out_shape += [jax.ShapeDtypeStruct((n_tok, D_MODEL), F32)] * 2
    return pl.pallas_call(
        functools.partial(_inproj_kernel, use_rope=use_rope, emit_kv_f32=emit_kv_f32),
        grid=(n_tok // tm,),
        in_specs=in_specs,
        out_specs=out_specs,
        out_shape=out_shape,
        compiler_params=pltpu.CompilerParams(
            dimension_semantics=("arbitrary",), vmem_limit_bytes=V7X_VMEM_LIMIT_BYTES),
        name="inproj_rope" if use_rope else "inproj",
    )(*args)


def _key_chunk(t_len):
    return min(512, t_len)


def _attn_kernel(*refs, has_cache, n_chunks, kc, heads_per_step):
    lam_ref, qt_ref, k_ref, vt_ref = refs[:4]
    refs = refs[4:]
    cache_refs = None
    if has_cache:
        cache_refs = refs[:2]
        refs = refs[2:]
    o_ref, acc_ref, s_ref, e_ref = refs

    lp = lam_ref[...]
    lam = (jnp.exp(jnp.sum(lp[0:1] * lp[1:2], axis=-1, keepdims=True))
           - jnp.exp(jnp.sum(lp[2:3] * lp[3:4], axis=-1, keepdims=True)) + LAM_INIT)

    if n_chunks == 1 and not has_cache:
        _attn_single_chunk(lam, qt_ref, k_ref, vt_ref, o_ref, s_ref, e_ref, n_heads=heads_per_step)
        return
    for hh in range(heads_per_step):
        lanes = slice(hh * HEAD_W, (hh + 1) * HEAD_W)
        _attn_accumulate(qt_ref.at[lanes, :], k_ref.at[:, lanes], vt_ref.at[hh], cache_refs,
                         pl.program_id(1) * heads_per_step + hh, acc_ref, s_ref, e_ref,
                         n_chunks=n_chunks, kc=kc)
        _attn_finalize(lam, acc_ref[0], acc_ref[1], o_ref.at[:, lanes])


def _query_maps(qt):
    row = lax.broadcasted_iota(jnp.int32, qt.shape, 0)
    zero = jnp.zeros_like(qt)
    return jnp.where(row < DIFF_HEAD_DIM, qt, zero), jnp.where(row >= DIFF_HEAD_DIM, qt, zero)


def _with_ones(vt_chunk):
    ones = jnp.ones((ACC_ROWS - HEAD_W, vt_chunk.shape[1]), BF16)
    return jnp.concatenate([vt_chunk, ones], axis=0)


def _attn_single_chunk(lam, qt_ref, k_ref, vt_ref, o_ref, s_ref, e_ref, *, n_heads):
    maxes = {}
    for t in range(n_heads + 2):
        if t < n_heads:
            lanes = slice(t * HEAD_W, (t + 1) * HEAD_W)
            keys = k_ref[:, lanes]
            q_maps = _query_maps(qt_ref[lanes, :])
            head_max = []
            for idx in range(2):
                st = _dot(keys, q_maps[idx])
                s_ref[t % 2, idx] = st
                head_max.append(jnp.max(st, axis=0, keepdims=True))
            maxes[t] = head_max
        if 1 <= t <= n_heads:
            h = t - 1
            for idx in range(2):
                e_ref[h % 2, idx] = jnp.exp2(s_ref[h % 2, idx] - maxes[h][idx]).astype(BF16)
        if t >= 2:
            h = t - 2
            lanes = slice(h * HEAD_W, (h + 1) * HEAD_W)
            vt_ext = _with_ones(vt_ref[h, 0])
            acc1, acc2 = [_dot(vt_ext, e_ref[h % 2, idx]) for idx in range(2)]
            _attn_finalize(lam, acc1, acc2, o_ref.at[:, lanes])


def _attn_finalize(lam, acc1, acc2, o_ref):
    o1 = acc1[:HEAD_W] * (1.0 / acc1[HEAD_W:HEAD_W + 1])
    o2 = acc2[:HEAD_W] * (1.0 / acc2[HEAD_W:HEAD_W + 1])
    ot = o1 - lam * o2
    ms = jnp.mean(ot * ot, axis=0, keepdims=True)
    o_ref[...] = (ot * lax.rsqrt(ms + EPS)).T.astype(BF16)


def _attn_accumulate(qt_ref, k_ref, vt_ref, cache_refs, head, acc_ref, s_ref, e_ref, *, n_chunks, kc):
    has_cache = cache_refs is not None
    tq = qt_ref.shape[1]
    q_maps = _query_maps(qt_ref[...])

    acc_ref[...] = jnp.zeros(acc_ref.shape, F32)

    n_cache = 0
    if has_cache:
        kc_ref, vc_ref = cache_refs
        cache_len = kc_ref.shape[0] // N_HEADS
        cache_chunk = min(CACHE_CHUNK, cache_len)
        n_cache = cache_len // cache_chunk
    n_total = n_chunks + n_cache

    def is_cache(c):
        return isinstance(c, int) and c < n_cache

    def cache_rows(c):
        return pl.ds(head + c * cache_chunk * N_HEADS, cache_chunk, stride=N_HEADS)

    def keys_of(c):
        if is_cache(c):
            return kc_ref[cache_rows(c), :].astype(BF16)
        start = (c - n_cache) * kc
        return k_ref[pl.ds(start if isinstance(c, int) else pl.multiple_of(start, kc), kc), :]

    def values_of(c):
        if is_cache(c):
            return vc_ref[cache_rows(c), :].T.astype(BF16)
        return vt_ref[c - n_cache]

    def scores(c, slot):
        k_chunk = keys_of(c)
        maxes = []
        for idx in range(2):
            st = _dot(k_chunk, q_maps[idx])
            s_ref[slot, idx, :k_chunk.shape[0], :] = st
            maxes.append(jnp.max(st, axis=0, keepdims=True))
        return tuple(maxes)

    def probs(n_keys, slot, ms, maxes):
        new_ms, alphas = [], []
        for idx in range(2):
            m_new = jnp.maximum(ms[idx], maxes[idx])
            alphas.append(jnp.exp2(ms[idx] - m_new))
            e_ref[slot, idx, :n_keys, :] = jnp.exp2(s_ref[slot, idx, :n_keys, :] - m_new).astype(BF16)
            new_ms.append(m_new)
        return tuple(new_ms), tuple(alphas)

    def values(c, slot, alphas):
        vt_ext = _with_ones(values_of(c))
        n_keys = vt_ext.shape[1]
        for idx in range(2):
            acc_ref[idx] = acc_ref[idx] * alphas[idx] + _dot(vt_ext, e_ref[slot, idx, :n_keys, :])

    def n_keys_of(c):
        return cache_chunk if is_cache(c) else kc

    def step(i, slot, ms, maxes_next, alphas_cur):
        maxes_after = scores(i + 2, slot)
        ms, alphas_next = probs(n_keys_of(i + 1), 1 - slot, ms, maxes_next)
        values(i, slot, alphas_cur)
        return ms, maxes_after, alphas_next

    m0 = jnp.full((1, tq), -jnp.inf, F32)
    ms = (m0, m0)
    maxes_cur = scores(0, 0)
    if n_total == 1:
        ms, alphas_cur = probs(n_keys_of(0), 0, ms, maxes_cur)
        values(0, 0, alphas_cur)
    else:
        maxes_next = scores(1, 1)
        ms, alphas_cur = probs(n_keys_of(0), 0, ms, maxes_cur)
        n_steps = n_total - 2
        first_loop = n_cache
        for i in range(min(first_loop, n_steps)):
            ms, maxes_next, alphas_cur = step(i, i % 2, ms, maxes_next, alphas_cur)
        n_pairs = max(n_steps - first_loop, 0) // 2

        def pair(t, carry):
            ms, maxes_next, alphas_cur = carry
            i = first_loop + 2 * t
            ms, maxes_next, alphas_cur = step(i, first_loop % 2, ms, maxes_next, alphas_cur)
            return step(i + 1, (first_loop + 1) % 2, ms, maxes_next, alphas_cur)

        ms, maxes_next, alphas_cur = lax.fori_loop(0, n_pairs, pair, (ms, maxes_next, alphas_cur))
        for i in range(first_loop + 2 * n_pairs, n_steps):
            ms, maxes_next, alphas_cur = step(i, i % 2, ms, maxes_next, alphas_cur)
        last = n_total - 1
        ms, alphas_last = probs(n_keys_of(last), last % 2, ms, maxes_next)
        values(last - 1, (last - 1) % 2, alphas_cur)
        values(last, last % 2, alphas_last)


def _attention(lam_params, qt, k, vt, cache_k=None, cache_v=None, *, tq=2048):
    b, t_len, _ = k.shape
    has_cache = cache_k is not None
    n_chunks, kc = vt.shape[2], vt.shape[4]
    tq = min(tq, t_len)
    heads_per_step = N_HEADS if n_chunks == 1 and not has_cache else 1
    width = heads_per_step * HEAD_W
    s_rows = max(kc, cache_k.shape[1] // N_HEADS) if has_cache else kc
    tok_spec = pl.BlockSpec((None, tq, width), lambda bi, h, qi: (bi, qi, h))
    in_specs = [
        pl.BlockSpec(lam_params.shape, lambda bi, h, qi: (0, 0)),
        pl.BlockSpec((None, width, tq), lambda bi, h, qi: (bi, h, qi)),
        pl.BlockSpec((None, t_len, width), lambda bi, h, qi: (bi, 0, h)),
        pl.BlockSpec((None, heads_per_step, n_chunks, HEAD_W, kc), lambda bi, h, qi: (bi, h, 0, 0, 0)),
    ]
    args = [lam_params, qt, k, vt]
    if has_cache:
        c_spec = pl.BlockSpec((None,) + cache_k.shape[1:], lambda bi, h, qi: (bi, 0, 0))
        in_specs += [c_spec, c_spec]
        args += [cache_k, cache_v]
    return pl.pallas_call(
        functools.partial(_attn_kernel, has_cache=has_cache, n_chunks=n_chunks, kc=kc,
                          heads_per_step=heads_per_step),
        grid=(b, N_HEADS // heads_per_step, t_len // tq),
        in_specs=in_specs,
        out_specs=tok_spec,
        out_shape=jax.ShapeDtypeStruct((b, t_len, D_ATT), BF16),
        scratch_shapes=[pltpu.VMEM((2, ACC_ROWS, tq), F32),
                        pltpu.VMEM((2, 2, s_rows, tq), F32),
                        pltpu.VMEM((2, 2, s_rows, tq), BF16)],
        compiler_params=pltpu.CompilerParams(
            dimension_semantics=("arbitrary",) * 3, vmem_limit_bytes=V7X_VMEM_LIMIT_BYTES),
        name="diff_attn_cache" if has_cache else "diff_attn",
    )(*args)


def _lru_kernel(x_ref, g_ref, cw_ref, cb_ref, w_ref, lam_ref, h0_ref,
                o_ref, st_ref,
                xp, a_f, b_f, a_b, b_b, hin_f, hin_b,
                *, blocks_per_step, t_len, chunk, pitch, rows):
    blocks = range(blocks_per_step)
    lanes = [slice(j * LRU_BLOCK, (j + 1) * LRU_BLOCK) for j in blocks]
    g_f, p_f, g_b, p_b = b_f, a_f, b_b, a_b
    n_chunks = t_len // chunk
    chunks_per_tile = rows // chunk

    pad = jnp.zeros((SUBLANES, LRU_BLOCK), F32)
    for j in blocks:
        xp[j, pl.ds(0, SUBLANES), :] = pad
        xp[j, pl.ds(SUBLANES, t_len), :] = x_ref[:, lanes[j]]
        xp[j, pl.ds(SUBLANES + t_len, SUBLANES), :] = pad

    lam = lam_ref[...]
    sp = jnp.maximum(-lam, 0.0) + jnp.log1p(jnp.exp(-jnp.abs(lam)))
    half_scale = (-0.5 * LRU_C) * sp
    cw = cw_ref[...]
    cb = cb_ref[...]
    lane = lax.broadcasted_iota(jnp.int32, (rows, LRU_BLOCK), 1)
    bias_taps = jnp.where(lane < BIAS_SPLIT, 1.0, 0.0).astype(BF16)

    def gate_tile(r, carry):
        t0 = pl.multiple_of(r * rows, SUBLANES)
        for j in blocks:
            u = cb[:, lanes[j]]
            for tap in range(4):
                u = u + xp[j, pl.ds(t0 + SUBLANES - 1 + tap, rows), :] * cw[tap:tap + 1, lanes[j]]
            zh = _dot(jnp.concatenate([u.astype(BF16), bias_taps], axis=1), w_ref[j])
            u_half = 0.5 * u
            for d, (a_s, b_s) in enumerate(((a_f, b_f), (a_b, b_b))):
                scale = half_scale[d:d + 1, lanes[j]]
                tr = jnp.tanh(zh[:, (2 * d) * LRU_BLOCK:(2 * d + 1) * LRU_BLOCK])
                ti = jnp.tanh(zh[:, (2 * d + 1) * LRU_BLOCK:(2 * d + 2) * LRU_BLOCK])
                log_a = scale + scale * tr
                a = jnp.exp(log_a)
                gain_sq = jnp.tanh(log_a) * (-1.0 - a * a)
                gain = jnp.where(gain_sq > 0.0, gain_sq * lax.rsqrt(gain_sq), 0.0)
                bb = gain * (u_half + u_half * ti)
                for cc in range(chunks_per_tile):
                    dst = pl.multiple_of(r * (chunks_per_tile * pitch), SUBLANES) + cc * pitch
                    a_s[j, pl.ds(dst, chunk), :] = a[cc * chunk:(cc + 1) * chunk]
                    b_s[j, pl.ds(dst, chunk), :] = bb[cc * chunk:(cc + 1) * chunk]
        return carry

    lax.fori_loop(0, t_len // rows, gate_tile, 0)

    def local_step(l, carry):
        out = []
        for j in blocks:
            hf, pf, hb, pb = carry[4 * j:4 * j + 4]
            rf = pl.ds(l, n_chunks, stride=pitch)
            a = a_f[j, rf, :]
            hf = a * hf + b_f[j, rf, :]
            pf = a * pf
            g_f[j, rf, :] = hf
            p_f[j, rf, :] = pf
            rb = pl.ds(chunk - 1 - l, n_chunks, stride=pitch)
            a = a_b[j, rb, :]
            hb = a * hb + b_b[j, rb, :]
            pb = a * pb
            g_b[j, rb, :] = hb
            p_b[j, rb, :] = pb
            out += [hf, pf, hb, pb]
        return tuple(out)

    z0 = jnp.zeros((n_chunks, LRU_BLOCK), F32)
    o0 = jnp.ones((n_chunks, LRU_BLOCK), F32)
    chunk_maps = lax.fori_loop(0, chunk, local_step, (z0, o0, z0, o0) * blocks_per_step, unroll=4)

    h0 = h0_ref[...]
    chunk_id = lax.broadcasted_iota(jnp.int32, (n_chunks, LRU_BLOCK), 0)

    def compose(end, decay, towards_higher):
        shift = 1
        while shift < n_chunks:
            if towards_higher:
                valid = chunk_id >= shift
                amount = shift
            else:
                valid = chunk_id < n_chunks - shift
                amount = n_chunks - shift
            prev_end = jnp.where(valid, pltpu.roll(end, amount, 0), 0.0)
            prev_decay = jnp.where(valid, pltpu.roll(decay, amount, 0), 1.0)
            end = end + decay * prev_end
            decay = decay * prev_decay
            shift *= 2
        return end, decay

    for j in blocks:
        end_f, decay_f, end_b, decay_b = chunk_maps[4 * j:4 * j + 4]
        h0_f = h0[0:1, lanes[j]]
        h0_b = h0[1:2, lanes[j]]
        end_f, decay_f = compose(end_f, decay_f, True)
        after_f = end_f + decay_f * h0_f
        hin_f[j] = jnp.where(chunk_id == 0, h0_f, pltpu.roll(after_f, 1, 0))
        st_ref[0:1, lanes[j]] = after_f[n_chunks - 1:n_chunks]
        end_b, decay_b = compose(end_b, decay_b, False)
        after_b = end_b + decay_b * h0_b
        hin_b[j] = jnp.where(chunk_id == n_chunks - 1, h0_b, pltpu.roll(after_b, n_chunks - 1, 0))
        st_ref[1:2, lanes[j]] = after_b[0:1]

    def out_pair(cp, carry):
        for par in range(2):
            c = 2 * cp + par
            src = pl.ds(pl.multiple_of(cp * (2 * pitch), SUBLANES) + par * pitch, chunk)
            dst = pl.ds(pl.multiple_of(c * chunk, SUBLANES), chunk)
            for j in blocks:
                hf = g_f[j, src, :] + p_f[j, src, :] * hin_f[j, pl.ds(c, 1), :]
                hb = g_b[j, src, :] + p_b[j, src, :] * hin_b[j, pl.ds(c, 1), :]
                o_ref[dst, lanes[j]] = ((hf + hb) * _silu(g_ref[dst, lanes[j]])).astype(BF16)
        return carry

    lax.fori_loop(0, n_chunks // 2, out_pair, 0, unroll=2)


def _lru_gate_operand(w_r, w_i, b_r, b_i):
    weights = 0.5 * jnp.concatenate([w_r[0], w_i[0], w_r[1], w_i[1]], axis=-1)
    bias = 0.5 * jnp.concatenate(
        [bb.reshape(N_LRU_BLOCKS, 1, LRU_BLOCK) for bb in (b_r[0], b_i[0], b_r[1], b_i[1])], axis=-1)
    parts, rest = [], bias
    for _ in range(BIAS_SPLIT):
        gamma = rest * (2.0 ** 16 + 1.0)
        part = gamma - (gamma - rest)
        parts.append(part)
        rest = rest - part
    bias_rows = jnp.pad(jnp.concatenate(parts, axis=1), ((0, 0), (0, LRU_BLOCK - BIAS_SPLIT), (0, 0)))
    return jnp.concatenate([weights, bias_rows], axis=1).astype(BF16)


def _lru(x_lru, g_lru, conv_w, conv_b, w_ext, lru_lam, h0):
    b, t_len, _ = x_lru.shape
    chunk = 64 if t_len >= 2048 else 32
    pitch = chunk + SUBLANES // 2
    n_chunks = t_len // chunk
    rows = min(2048, t_len)
    blocks_per_step = N_LRU_BLOCKS if t_len <= 512 else 1
    width = blocks_per_step * LRU_BLOCK
    seq_spec = pl.BlockSpec((None, t_len, width), lambda bi, n: (bi, 0, n))
    st_spec = pl.BlockSpec((None, 2, width), lambda bi, n: (bi, 0, n))
    scan_buf = pltpu.VMEM((blocks_per_step, n_chunks * pitch, LRU_BLOCK), F32)
    return pl.pallas_call(
        functools.partial(_lru_kernel, blocks_per_step=blocks_per_step,
                          t_len=t_len, chunk=chunk, pitch=pitch, rows=rows),
        grid=(b, N_LRU_BLOCKS // blocks_per_step),
        in_specs=[
            seq_spec, seq_spec,
            pl.BlockSpec((4, width), lambda bi, n: (0, n)),
            pl.BlockSpec((1, width), lambda bi, n: (0, n)),
            pl.BlockSpec((blocks_per_step, 2 * LRU_BLOCK, 4 * LRU_BLOCK), lambda bi, n: (n, 0, 0)),
            pl.BlockSpec((2, width), lambda bi, n: (0, n)),
            st_spec,
        ],
        out_specs=[seq_spec, st_spec],
        out_shape=[jax.ShapeDtypeStruct((b, t_len, D_LRU), BF16),
                   jax.ShapeDtypeStruct((b, 2, D_LRU), F32)],
        scratch_shapes=[
            pltpu.VMEM((blocks_per_step, t_len + 2 * SUBLANES, LRU_BLOCK), F32),
            scan_buf, scan_buf, scan_buf, scan_buf,
            pltpu.VMEM((blocks_per_step, n_chunks, LRU_BLOCK), F32),
            pltpu.VMEM((blocks_per_step, n_chunks, LRU_BLOCK), F32),
        ],
        compiler_params=pltpu.CompilerParams(
            dimension_semantics=("arbitrary",) * 2, vmem_limit_bytes=V7X_VMEM_LIMIT_BYTES),
        name="rglru",
    )(x_lru, g_lru, conv_w, conv_b, w_ext, lru_lam, h0)


def _outproj_kernel(att_ref, gatt_ref, gsub_ref, lru_ref, x_ref, gate_ref, wa_ref, wl_ref, gpost_ref, y_ref):
    att = att_ref[...].astype(F32) * gsub_ref[...] * _silu(gatt_ref[...])
    o = _dot(att.astype(BF16), wa_ref[...]) + _dot(lru_ref[...], wl_ref[...])
    ms = jnp.mean(o * o, axis=-1, keepdims=True)
    n = o * lax.rsqrt(ms + EPS) * gpost_ref[...]
    y_ref[...] = x_ref[...] + gate_ref[...] * n


def _outproj(att, g_att, g_sub, lru, x, gate, w_att, w_lru, g_post, *, t_len, tm=1024):
    n_tok = x.shape[0]
    nb = gate.shape[0]
    if nb == 1:
        tm = min(tm, n_tok)
        mod_map = lambda i: (0, 0, 0)
    else:
        tm = min(tm, t_len)
        tiles_per_batch = t_len // tm
        mod_map = lambda i: (i // tiles_per_batch, 0, 0)
    tok_spec = pl.BlockSpec((tm, D_MODEL), lambda i: (i, 0))
    w_spec = pl.BlockSpec((D_MODEL, D_MODEL), lambda i: (0, 0))
    row_spec = pl.BlockSpec((1, D_MODEL), lambda i: (0, 0))
    return pl.pallas_call(
        _outproj_kernel,
        grid=(n_tok // tm,),
        in_specs=[tok_spec, tok_spec, row_spec, tok_spec, tok_spec,
                  pl.BlockSpec((None, 1, D_MODEL), mod_map),
                  w_spec, w_spec, row_spec],
        out_specs=tok_spec,
        out_shape=jax.ShapeDtypeStruct((n_tok, D_MODEL), F32),
        compiler_params=pltpu.CompilerParams(
            dimension_semantics=("arbitrary",), vmem_limit_bytes=V7X_VMEM_LIMIT_BYTES),
        name="outproj",
    )(att, g_att, g_sub, lru, x, gate, w_att, w_lru, g_post)


def _sublayer(x, scale, shift, gate, cache_k, cache_v, h0, use_rope, w):
    b, t_len, _ = x.shape
    xf = x.reshape(b * t_len, D_MODEL)
    emit_kv = cache_k is None
    outs = _inproj(xf, scale, shift, w["g_pre"], w["w_in"], t_len=t_len,
                   use_rope=use_rope, emit_kv_f32=emit_kv)
    qt, vt = outs[0], outs[2]
    g_att = outs[3]
    k, x_lru, g_lru = [outs[i].reshape(b, t_len, D_MODEL) for i in (1, 4, 5)]
    att = _attention(w["lam"], qt, k, vt, cache_k, cache_v)
    lru, state = _lru(x_lru, g_lru, w["conv_w"], w["conv_b"], w["w_ext"], w["lru_lam"], h0)
    y = _outproj(att.reshape(b * t_len, D_ATT), g_att, w["g_sub"], lru.reshape(b * t_len, D_LRU), xf, gate,
                 w["w_out_att"], w["w_out_lru"], w["g_post"], t_len=t_len)
    y = y.reshape(b, t_len, D_MODEL)
    if emit_kv:
        return y, outs[6], outs[7], state
    return y, None, None, state


def kernel(x_prompt, x_sample, cache_k, cache_v, state_lru, c, c_ctx, w_ada, b_ada, g_pre, w_in, lambda_q1, lambda_k1, lambda_q2, lambda_k2, g_subln, conv_w, conv_b, w_rgate, b_rgate, w_igate, b_igate, lru_lambda, w_out, g_post):
    bp, seq, _ = x_prompt.shape
    bd, kc = cache_k.shape[0], cache_k.shape[2]
    assert cache_k.shape[1] == 1 and w_in.shape[0] == 1, "single-layer step only"
    l = 0

    cond = jnp.concatenate([c, c_ctx[None, :], jnp.zeros((SUBLANES - bd - 1, D_MODEL), F32)], axis=0)
    mod = _modulation(cond, w_ada[l], b_ada[l][None, :])
    shift, scale, gate = [mod[:, i * D_MODEL:(i + 1) * D_MODEL] for i in range(3)]

    def rows(a, lo, hi):
        return a[lo:hi][:, None, :]

    w = {
        "g_pre": g_pre[l][None, :],
        "w_in": w_in[l].astype(BF16),
        "lam": jnp.stack([lambda_q1[l], lambda_k1[l], lambda_q2[l], lambda_k2[l]], axis=0),
        "g_sub": jnp.tile(g_subln[l] * (1.0 - LAM_INIT), N_HEADS)[None, :],
        "conv_w": conv_w[l],
        "conv_b": conv_b[l][None, :],
        "w_ext": _lru_gate_operand(w_rgate[l], w_igate[l], b_rgate[l], b_igate[l]),
        "lru_lam": lru_lambda[l],
        "w_out_att": w_out[l, :D_ATT].astype(BF16),
        "w_out_lru": w_out[l, D_ATT:].astype(BF16),
        "g_post": g_post[l][None, :],
    }

    y_p, new_k, new_v, st_p = _sublayer(
        x_prompt, rows(scale, bd, bd + 1), rows(shift, bd, bd + 1), rows(gate, bd, bd + 1),
        None, None, jnp.zeros((bp, 2, D_LRU), F32), False, w)

    y_s, _, _, _ = _sublayer(
        x_sample, rows(scale, 0, bd), rows(shift, 0, bd), rows(gate, 0, bd),
        cache_k.reshape(bd, kc * N_HEADS, HEAD_W), cache_v.reshape(bd, kc * N_HEADS, HEAD_W),
        state_lru.reshape(bd, 2, D_LRU), True, w)

    new_k = new_k.reshape(bp, 1, seq, N_HEADS, HEAD_W)
    new_v = new_v.reshape(bp, 1, seq, N_HEADS, HEAD_W)
    return (y_p, y_s, new_k, new_v, st_p[:, None])
```

```python
import functools
import math

import jax
import jax.numpy as jnp
import numpy as np
from jax import lax
from jax.experimental import pallas as pl
from jax.experimental.pallas import tpu as pltpu

F32 = jnp.float32
BF16 = jnp.bfloat16

D_MODEL = 1024
GRID_W = 64
N_HEADS = 8
DIFF_HEAD_DIM = 64
HEAD_W = 2 * DIFF_HEAD_DIM
D_ATT = N_HEADS * HEAD_W
D_LRU = 1024
N_LRU_BLOCKS = 8
LRU_BLOCK = D_LRU // N_LRU_BLOCKS
LRU_C = 8.0
N_GROUPS = 6
ROPE_BASE = 10000.0
EPS = 1e-6
LAM_INIT = 0.8 - 0.6 * math.exp(-0.3 * 0)
LOG2_E = math.log2(math.e)

V7X_VMEM_LIMIT_BYTES = 56 * 1024 * 1024
SUBLANES = 8
ROPE_SWAP = DIFF_HEAD_DIM // 4
BF16_ROWS_PER_VREG = 16
ACC_ROWS = HEAD_W + BF16_ROWS_PER_VREG
CACHE_CHUNK = 256
BIAS_SPLIT = 3


def _silu(x):
    return x * jax.nn.sigmoid(x)


def _dot(a, b):
    return jnp.dot(a, b, preferred_element_type=F32)


def _mod_kernel(cond_ref, w_ref, b_ref, o_ref):
    s = _silu(cond_ref[...])
    o_ref[...] = _dot(s.astype(BF16), w_ref[...].astype(BF16)) + b_ref[...]


def _modulation(cond, w_ada, b_ada):
    n = cond.shape[0]
    return pl.pallas_call(
        _mod_kernel,
        grid=(3,),
        in_specs=[
            pl.BlockSpec((n, D_MODEL), lambda j: (0, 0)),
            pl.BlockSpec((D_MODEL, D_MODEL), lambda j: (0, j)),
            pl.BlockSpec((1, D_MODEL), lambda j: (0, j)),
        ],
        out_specs=pl.BlockSpec((n, D_MODEL), lambda j: (0, j)),
        out_shape=jax.ShapeDtypeStruct((n, 3 * D_MODEL), F32),
        name="modulation",
    )(cond, w_ada, b_ada)


def _rope_tables(t_len):
    half = DIFF_HEAD_DIM // 2
    nf = half // 2
    t = np.arange(t_len)
    row = (t // GRID_W).astype(np.float32)
    col = (t % GRID_W).astype(np.float32)
    inv = (ROPE_BASE ** (-np.arange(nf, dtype=np.float32) * 2.0 / half)).astype(np.float32)
    lane = np.arange(HEAD_W) % DIFF_HEAD_DIM
    use_row = lane < half
    freq = (lane % half) % nf
    first = (lane % half) < nf
    pos = np.where(use_row[None, :], row[:, None], col[:, None]).astype(np.float32)
    ang = (pos * inv[freq][None, :]).astype(np.float32).astype(np.float64)
    cos = np.cos(ang).astype(np.float32)
    sin = (np.sin(ang) * np.where(first, -1.0, 1.0)[None, :]).astype(np.float32)
    return jnp.asarray(cos), jnp.asarray(sin)


def _inproj_kernel(*refs, use_rope, emit_kv_f32):
    x_ref, scale_ref, shift_ref, gpre_ref, w_ref = refs[:5]
    refs = refs[5:]
    if use_rope:
        cos_ref, sin_ref = refs[:2]
        refs = refs[2:]
    q_ref, k_ref, v_ref, gatt_ref, xlru_ref, glru_ref = refs[:6]
    refs = refs[6:]
    if emit_kv_f32:
        kf_ref, vf_ref = refs

    x = x_ref[...]
    ms = jnp.mean(x * x, axis=-1, keepdims=True)
    y = x * lax.rsqrt(ms + EPS) * gpre_ref[...]
    h = (y * (1.0 + scale_ref[...]) + shift_ref[...]).astype(BF16)

    def proj(g):
        return _dot(h, w_ref[:, g * D_MODEL:(g + 1) * D_MODEL])

    if use_rope:
        cos = cos_ref[...]
        sin = sin_ref[...]
        lane = lax.broadcasted_iota(jnp.int32, cos.shape, 1)
        take_next = (lane % (2 * ROPE_SWAP)) < ROPE_SWAP

        def rope(p):
            outs = []
            for hd in range(N_HEADS):
                xh = p[:, hd * HEAD_W:(hd + 1) * HEAD_W]
                partner = jnp.where(take_next,
                                    pltpu.roll(xh, HEAD_W - ROPE_SWAP, 1),
                                    pltpu.roll(xh, ROPE_SWAP, 1))
                outs.append(xh * cos + partner * sin)
            return jnp.concatenate(outs, axis=-1)
    else:
        def rope(p):
            return p

    q = rope(proj(0)) * (LOG2_E / math.sqrt(DIFF_HEAD_DIM))
    q_ref[...] = q.T.astype(BF16)
    k = rope(proj(1))
    k_ref[...] = k.astype(BF16)
    v = proj(2)
    vt = v.T
    for hd in range(N_HEADS):
        v_ref[hd] = vt[hd * HEAD_W:(hd + 1) * HEAD_W].astype(BF16)
    if emit_kv_f32:
        kf_ref[...] = k
        vf_ref[...] = v
    gatt_ref[...] = proj(3)
    xlru_ref[...] = proj(4)
    glru_ref[...] = proj(5)


def _inproj(x, scale, shift, g_pre, w_in_bf16, *, t_len, use_rope, emit_kv_f32, tm=512):
    n_tok = x.shape[0]
    tm = min(tm, t_len)
    tiles_per_batch = t_len // tm
    nb = scale.shape[0]
    if nb == 1:
        mod_map = lambda i: (0, 0, 0)
    else:
        mod_map = lambda i: (i // tiles_per_batch, 0, 0)
    tok_spec = pl.BlockSpec((tm, D_MODEL), lambda i: (i, 0))
    in_specs = [
        tok_spec,
        pl.BlockSpec((None, 1, D_MODEL), mod_map),
        pl.BlockSpec((None, 1, D_MODEL), mod_map),
        pl.BlockSpec((1, D_MODEL), lambda i: (0, 0)),
        pl.BlockSpec((D_MODEL, N_GROUPS * D_MODEL), lambda i: (0, 0), pipeline_mode=pl.Buffered(1)),
    ]
    args = [x, scale, shift, g_pre, w_in_bf16]
    if use_rope:
        cos, sin = _rope_tables(t_len)
        rope_spec = pl.BlockSpec((tm, HEAD_W), lambda i: (i % tiles_per_batch, 0))
        in_specs += [rope_spec, rope_spec]
        args += [cos, sin]
    n_batch = n_tok // t_len
    kc = _key_chunk(t_len)
    tiles_per_chunk = kc // tm
    out_specs = [
        pl.BlockSpec((None, D_MODEL, tm), lambda i: (i // tiles_per_batch, 0, i % tiles_per_batch)),
        tok_spec,
        pl.BlockSpec((None, N_HEADS, None, HEAD_W, tm),
                     lambda i: (i // tiles_per_batch, 0, (i % tiles_per_batch) // tiles_per_chunk, 0,
                                i % tiles_per_chunk)),
        tok_spec, tok_spec, tok_spec,
    ]
    out_shape = [
        jax.ShapeDtypeStruct((n_batch, D_MODEL, t_len), BF16),
        jax.ShapeDtypeStruct((n_tok, D_MODEL), BF16),
        jax.ShapeDtypeStruct((n_batch, N_HEADS, t_len // kc, HEAD_W, kc), BF16),
    ] + [jax.ShapeDtypeStruct((n_tok, D_MODEL), F32)] * 3
    if emit_kv_f32:
        out_specs += [tok_spec] * 2
        out_shape += [jax.ShapeDtypeStruct((n_tok, D_MODEL), F32)] * 2
    return pl.pallas_call(
        functools.partial(_inproj_kernel, use_rope=use_rope, emit_kv_f32=emit_kv_f32),
        grid=(n_tok // tm,),
        in_specs=in_specs,
        out_specs=out_specs,
        out_shape=out_shape,
        compiler_params=pltpu.CompilerParams(
            dimension_semantics=("arbitrary",), vmem_limit_bytes=V7X_VMEM_LIMIT_BYTES),
        name="inproj_rope" if use_rope else "inproj",
    )(*args)


def _key_chunk(t_len):
    return min(512, t_len)


def _attn_kernel(*refs, has_cache, n_chunks, kc, heads_per_step):
    lam_ref, gsub_ref, qt_ref, k_ref, vt_ref, gatt_ref = refs[:6]
    refs = refs[6:]
    cache_refs = None
    if has_cache:
        cache_refs = refs[:2]
        refs = refs[2:]
    o_ref, acc_ref, s_ref, e_ref = refs

    lp = lam_ref[...]
    lam = (jnp.exp(jnp.sum(lp[0:1] * lp[1:2], axis=-1, keepdims=True))
           - jnp.exp(jnp.sum(lp[2:3] * lp[3:4], axis=-1, keepdims=True)) + LAM_INIT)
    g_subln = gsub_ref[...]

    if n_chunks == 1 and not has_cache:
        _attn_single_chunk(lam, g_subln, qt_ref, k_ref, vt_ref, gatt_ref, o_ref, s_ref, e_ref,
                           n_heads=heads_per_step)
        return
    for hh in range(heads_per_step):
        lanes = slice(hh * HEAD_W, (hh + 1) * HEAD_W)
        _attn_accumulate(qt_ref.at[lanes, :], k_ref.at[:, lanes], vt_ref.at[hh], cache_refs,
                         pl.program_id(1) * heads_per_step + hh, acc_ref, s_ref, e_ref,
                         n_chunks=n_chunks, kc=kc)
        _attn_finalize(lam, g_subln, acc_ref[0], acc_ref[1], gatt_ref.at[:, lanes], o_ref.at[:, lanes])


def _query_maps(qt):
    row = lax.broadcasted_iota(jnp.int32, qt.shape, 0)
    zero = jnp.zeros_like(qt)
    return jnp.where(row < DIFF_HEAD_DIM, qt, zero), jnp.where(row >= DIFF_HEAD_DIM, qt, zero)


def _with_ones(vt_chunk):
    ones = jnp.ones((ACC_ROWS - HEAD_W, vt_chunk.shape[1]), BF16)
    return jnp.concatenate([vt_chunk, ones], axis=0)


def _attn_single_chunk(lam, g_subln, qt_ref, k_ref, vt_ref, gatt_ref, o_ref, s_ref, e_ref, *, n_heads):
    maxes = {}
    for t in range(n_heads + 2):
        if t < n_heads:
            lanes = slice(t * HEAD_W, (t + 1) * HEAD_W)
            keys = k_ref[:, lanes]
            q_maps = _query_maps(qt_ref[lanes, :])
            head_max = []
            for idx in range(2):
                st = _dot(keys, q_maps[idx])
                s_ref[t % 2, idx] = st
                head_max.append(jnp.max(st, axis=0, keepdims=True))
            maxes[t] = head_max
        if 1 <= t <= n_heads:
            h = t - 1
            for idx in range(2):
                e_ref[h % 2, idx] = jnp.exp2(s_ref[h % 2, idx] - maxes[h][idx]).astype(BF16)
        if t >= 2:
            h = t - 2
            lanes = slice(h * HEAD_W, (h + 1) * HEAD_W)
            vt_ext = _with_ones(vt_ref[h, 0])
            acc1, acc2 = [_dot(vt_ext, e_ref[h % 2, idx]) for idx in range(2)]
            _attn_finalize(lam, g_subln, acc1, acc2, gatt_ref.at[:, lanes], o_ref.at[:, lanes])


def _attn_finalize(lam, g_subln, acc1, acc2, gatt_ref, o_ref):
    o1 = acc1[:HEAD_W] * (1.0 / acc1[HEAD_W:HEAD_W + 1])
    o2 = acc2[:HEAD_W] * (1.0 / acc2[HEAD_W:HEAD_W + 1])
    ot = o1 - lam * o2
    ms = jnp.mean(ot * ot, axis=0, keepdims=True)
    on = (ot * lax.rsqrt(ms + EPS)).T * (g_subln * (1.0 - LAM_INIT))
    o_ref[...] = (on * _silu(gatt_ref[...])).astype(BF16)


def _attn_accumulate(qt_ref, k_ref, vt_ref, cache_refs, head, acc_ref, s_ref, e_ref, *, n_chunks, kc):
    has_cache = cache_refs is not None
    tq = qt_ref.shape[1]
    q_maps = _query_maps(qt_ref[...])

    acc_ref[...] = jnp.zeros(acc_ref.shape, F32)

    n_cache = 0
    if has_cache:
        kc_ref, vc_ref = cache_refs
        cache_len = kc_ref.shape[0] // N_HEADS
        cache_chunk = min(CACHE_CHUNK, cache_len)
        n_cache = cache_len // cache_chunk
    n_total = n_chunks + n_cache

    def is_cache(c):
        return isinstance(c, int) and c < n_cache

    def cache_rows(c):
        return pl.ds(head + c * cache_chunk * N_HEADS, cache_chunk, stride=N_HEADS)

    def keys_of(c):
        if is_cache(c):
            return kc_ref[cache_rows(c), :].astype(BF16)
        start = (c - n_cache) * kc
        return k_ref[pl.ds(start if isinstance(c, int) else pl.multiple_of(start, kc), kc), :]

    def values_of(c):
        if is_cache(c):
            return vc_ref[cache_rows(c), :].T.astype(BF16)
        return vt_ref[c - n_cache]

    def scores(c, slot):
        k_chunk = keys_of(c)
        maxes = []
        for idx in range(2):
            st = _dot(k_chunk, q_maps[idx])
            s_ref[slot, idx, :k_chunk.shape[0], :] = st
            maxes.append(jnp.max(st, axis=0, keepdims=True))
        return tuple(maxes)

    def probs(n_keys, slot, ms, maxes):
        new_ms, alphas = [], []
        for idx in range(2):
            m_new = jnp.maximum(ms[idx], maxes[idx])
            alphas.append(jnp.exp2(ms[idx] - m_new))
            e_ref[slot, idx, :n_keys, :] = jnp.exp2(s_ref[slot, idx, :n_keys, :] - m_new).astype(BF16)
            new_ms.append(m_new)
        return tuple(new_ms), tuple(alphas)

    def values(c, slot, alphas):
        vt_ext = _with_ones(values_of(c))
        n_keys = vt_ext.shape[1]
        for idx in range(2):
            acc_ref[idx] = acc_ref[idx] * alphas[idx] + _dot(vt_ext, e_ref[slot, idx, :n_keys, :])

    def n_keys_of(c):
        return cache_chunk if is_cache(c) else kc

    def step(i, slot, ms, maxes_next, alphas_cur):
        maxes_after = scores(i + 2, slot)
        ms, alphas_next = probs(n_keys_of(i + 1), 1 - slot, ms, maxes_next)
        values(i, slot, alphas_cur)
        return ms, maxes_after, alphas_next

    m0 = jnp.full((1, tq), -jnp.inf, F32)
    ms = (m0, m0)
    maxes_cur = scores(0, 0)
    if n_total == 1:
        ms, alphas_cur = probs(n_keys_of(0), 0, ms, maxes_cur)
        values(0, 0, alphas_cur)
    else:
        maxes_next = scores(1, 1)
        ms, alphas_cur = probs(n_keys_of(0), 0, ms, maxes_cur)
        n_steps = n_total - 2
        first_loop = n_cache
        for i in range(min(first_loop, n_steps)):
            ms, maxes_next, alphas_cur = step(i, i % 2, ms, maxes_next, alphas_cur)
        n_pairs = max(n_steps - first_loop, 0) // 2

        def pair(t, carry):
            ms, maxes_next, alphas_cur = carry
            i = first_loop + 2 * t
            ms, maxes_next, alphas_cur = step(i, first_loop % 2, ms, maxes_next, alphas_cur)
            return step(i + 1, (first_loop + 1) % 2, ms, maxes_next, alphas_cur)

        ms, maxes_next, alphas_cur = lax.fori_loop(0, n_pairs, pair, (ms, maxes_next, alphas_cur))
        for i in range(first_loop + 2 * n_pairs, n_steps):
            ms, maxes_next, alphas_cur = step(i, i % 2, ms, maxes_next, alphas_cur)
        last = n_total - 1
        ms, alphas_last = probs(n_keys_of(last), last % 2, ms, maxes_next)
        values(last - 1, (last - 1) % 2, alphas_cur)
        values(last, last % 2, alphas_last)


def _attention(lam_params, g_subln, qt, k, vt, g_att, cache_k=None, cache_v=None, *, tq=2048):
    b, t_len, _ = k.shape
    has_cache = cache_k is not None
    n_chunks, kc = vt.shape[2], vt.shape[4]
    tq = min(tq, t_len)
    heads_per_step = N_HEADS if n_chunks == 1 and not has_cache else 1
    width = heads_per_step * HEAD_W
    s_rows = max(kc, cache_k.shape[1] // N_HEADS) if has_cache else kc
    tok_spec = pl.BlockSpec((None, tq, width), lambda bi, h, qi: (bi, qi, h))
    in_specs = [
        pl.BlockSpec(lam_params.shape, lambda bi, h, qi: (0, 0)),
        pl.BlockSpec((1, HEAD_W), lambda bi, h, qi: (0, 0)),
        pl.BlockSpec((None, width, tq), lambda bi, h, qi: (bi, h, qi)),
        pl.BlockSpec((None, t_len, width), lambda bi, h, qi: (bi, 0, h)),
        pl.BlockSpec((None, heads_per_step, n_chunks, HEAD_W, kc), lambda bi, h, qi: (bi, h, 0, 0, 0)),
        tok_spec,
    ]
    args = [lam_params, g_subln, qt, k, vt, g_att]
    if has_cache:
        c_spec = pl.BlockSpec((None,) + cache_k.shape[1:], lambda bi, h, qi: (bi, 0, 0))
        in_specs += [c_spec, c_spec]
        args += [cache_k, cache_v]
    return pl.pallas_call(
        functools.partial(_attn_kernel, has_cache=has_cache, n_chunks=n_chunks, kc=kc,
                          heads_per_step=heads_per_step),
        grid=(b, N_HEADS // heads_per_step, t_len // tq),
        in_specs=in_specs,
        out_specs=tok_spec,
        out_shape=jax.ShapeDtypeStruct((b, t_len, D_ATT), BF16),
        scratch_shapes=[pltpu.VMEM((2, ACC_ROWS, tq), F32),
                        pltpu.VMEM((2, 2, s_rows, tq), F32),
                        pltpu.VMEM((2, 2, s_rows, tq), BF16)],
        compiler_params=pltpu.CompilerParams(
            dimension_semantics=("arbitrary",) * 3, vmem_limit_bytes=V7X_VMEM_LIMIT_BYTES),
        name="diff_attn_cache" if has_cache else "diff_attn",
    )(*args)


def _lru_kernel(x_ref, g_ref, cw_ref, cb_ref, w_ref, lam_ref, h0_ref,
                o_ref, st_ref,
                xp, a_f, b_f, a_b, b_b, hin_f, hin_b,
                *, blocks_per_step, t_len, chunk, pitch, rows):
    blocks = range(blocks_per_step)
    lanes = [slice(j * LRU_BLOCK, (j + 1) * LRU_BLOCK) for j in blocks]
    g_f, p_f, g_b, p_b = b_f, a_f, b_b, a_b
    n_chunks = t_len // chunk
    chunks_per_tile = rows // chunk

    pad = jnp.zeros((SUBLANES, LRU_BLOCK), F32)
    for j in blocks:
        xp[j, pl.ds(0, SUBLANES), :] = pad
        xp[j, pl.ds(SUBLANES, t_len), :] = x_ref[:, lanes[j]]
        xp[j, pl.ds(SUBLANES + t_len, SUBLANES), :] = pad

    lam = lam_ref[...]
    sp = jnp.maximum(-lam, 0.0) + jnp.log1p(jnp.exp(-jnp.abs(lam)))
    half_scale = (-0.5 * LRU_C) * sp
    cw = cw_ref[...]
    cb = cb_ref[...]
    lane = lax.broadcasted_iota(jnp.int32, (rows, LRU_BLOCK), 1)
    bias_taps = jnp.where(lane < BIAS_SPLIT, 1.0, 0.0).astype(BF16)

    def gate_tile(r, carry):
        t0 = pl.multiple_of(r * rows, SUBLANES)
        for j in blocks:
            u = cb[:, lanes[j]]
            for tap in range(4):
                u = u + xp[j, pl.ds(t0 + SUBLANES - 1 + tap, rows), :] * cw[tap:tap + 1, lanes[j]]
            zh = _dot(jnp.concatenate([u.astype(BF16), bias_taps], axis=1), w_ref[j])
            u_half = 0.5 * u
            for d, (a_s, b_s) in enumerate(((a_f, b_f), (a_b, b_b))):
                scale = half_scale[d:d + 1, lanes[j]]
                tr = jnp.tanh(zh[:, (2 * d) * LRU_BLOCK:(2 * d + 1) * LRU_BLOCK])
                ti = jnp.tanh(zh[:, (2 * d + 1) * LRU_BLOCK:(2 * d + 2) * LRU_BLOCK])
                log_a = scale + scale * tr
                a = jnp.exp(log_a)
                gain_sq = jnp.tanh(log_a) * (-1.0 - a * a)
                gain = jnp.where(gain_sq > 0.0, gain_sq * lax.rsqrt(gain_sq), 0.0)
                bb = gain * (u_half + u_half * ti)
                for cc in range(chunks_per_tile):
                    dst = pl.multiple_of(r * (chunks_per_tile * pitch), SUBLANES) + cc * pitch
                    a_s[j, pl.ds(dst, chunk), :] = a[cc * chunk:(cc + 1) * chunk]
                    b_s[j, pl.ds(dst, chunk), :] = bb[cc * chunk:(cc + 1) * chunk]
        return carry

    lax.fori_loop(0, t_len // rows, gate_tile, 0)

    def local_step(l, carry):
        out = []
        for j in blocks:
            hf, pf, hb, pb = carry[4 * j:4 * j + 4]
            rf = pl.ds(l, n_chunks, stride=pitch)
            a = a_f[j, rf, :]
            hf = a * hf + b_f[j, rf, :]
            pf = a * pf
            g_f[j, rf, :] = hf
            p_f[j, rf, :] = pf
            rb = pl.ds(chunk - 1 - l, n_chunks, stride=pitch)
            a = a_b[j, rb, :]
            hb = a * hb + b_b[j, rb, :]
            pb = a * pb
            g_b[j, rb, :] = hb
            p_b[j, rb, :] = pb
            out += [hf, pf, hb, pb]
        return tuple(out)

    z0 = jnp.zeros((n_chunks, LRU_BLOCK), F32)
    o0 = jnp.ones((n_chunks, LRU_BLOCK), F32)
    chunk_maps = lax.fori_loop(0, chunk, local_step, (z0, o0, z0, o0) * blocks_per_step, unroll=4)

    h0 = h0_ref[...]
    chunk_id = lax.broadcasted_iota(jnp.int32, (n_chunks, LRU_BLOCK), 0)

    def compose(end, decay, towards_higher):
        shift = 1
        while shift < n_chunks:
            if towards_higher:
                valid = chunk_id >= shift
                amount = shift
            else:
                valid = chunk_id < n_chunks - shift
                amount = n_chunks - shift
            prev_end = jnp.where(valid, pltpu.roll(end, amount, 0), 0.0)
            prev_decay = jnp.where(valid, pltpu.roll(decay, amount, 0), 1.0)
            end = end + decay * prev_end
            decay = decay * prev_decay
            shift *= 2
        return end, decay

    for j in blocks:
        end_f, decay_f, end_b, decay_b = chunk_maps[4 * j:4 * j + 4]
        h0_f = h0[0:1, lanes[j]]
        h0_b = h0[1:2, lanes[j]]
        end_f, decay_f = compose(end_f, decay_f, True)
        after_f = end_f + decay_f * h0_f
        hin_f[j] = jnp.where(chunk_id == 0, h0_f, pltpu.roll(after_f, 1, 0))
        st_ref[0:1, lanes[j]] = after_f[n_chunks - 1:n_chunks]
        end_b, decay_b = compose(end_b, decay_b, False)
        after_b = end_b + decay_b * h0_b
        hin_b[j] = jnp.where(chunk_id == n_chunks - 1, h0_b, pltpu.roll(after_b, n_chunks - 1, 0))
        st_ref[1:2, lanes[j]] = after_b[0:1]

    def out_pair(cp, carry):
        for par in range(2):
            c = 2 * cp + par
            src = pl.ds(pl.multiple_of(cp * (2 * pitch), SUBLANES) + par * pitch, chunk)
            dst = pl.ds(pl.multiple_of(c * chunk, SUBLANES), chunk)
            for j in blocks:
                hf = g_f[j, src, :] + p_f[j, src, :] * hin_f[j, pl.ds(c, 1), :]
                hb = g_b[j, src, :] + p_b[j, src, :] * hin_b[j, pl.ds(c, 1), :]
                o_ref[dst, lanes[j]] = ((hf + hb) * _silu(g_ref[dst, lanes[j]])).astype(BF16)
        return carry

    lax.fori_loop(0, n_chunks // 2, out_pair, 0, unroll=2)


def _lru_gate_operand(w_r, w_i, b_r, b_i):
    weights = 0.5 * jnp.concatenate([w_r[0], w_i[0], w_r[1], w_i[1]], axis=-1)
    bias = 0.5 * jnp.concatenate(
        [bb.reshape(N_LRU_BLOCKS, 1, LRU_BLOCK) for bb in (b_r[0], b_i[0], b_r[1], b_i[1])], axis=-1)
    parts, rest = [], bias
    for _ in range(BIAS_SPLIT):
        gamma = rest * (2.0 ** 16 + 1.0)
        part = gamma - (gamma - rest)
        parts.append(part)
        rest = rest - part
    bias_rows = jnp.pad(jnp.concatenate(parts, axis=1), ((0, 0), (0, LRU_BLOCK - BIAS_SPLIT), (0, 0)))
    return jnp.concatenate([weights, bias_rows], axis=1).astype(BF16)


def _lru(x_lru, g_lru, conv_w, conv_b, w_ext, lru_lam, h0):
    b, t_len, _ = x_lru.shape
    chunk = 64 if t_len >= 2048 else 32
    pitch = chunk + SUBLANES // 2
    n_chunks = t_len // chunk
    rows = min(2048, t_len)
    blocks_per_step = N_LRU_BLOCKS if t_len <= 512 else 1
    width = blocks_per_step * LRU_BLOCK
    seq_spec = pl.BlockSpec((None, t_len, width), lambda bi, n: (bi, 0, n))
    st_spec = pl.BlockSpec((None, 2, width), lambda bi, n: (bi, 0, n))
    scan_buf = pltpu.VMEM((blocks_per_step, n_chunks * pitch, LRU_BLOCK), F32)
    return pl.pallas_call(
        functools.partial(_lru_kernel, blocks_per_step=blocks_per_step,
                          t_len=t_len, chunk=chunk, pitch=pitch, rows=rows),
        grid=(b, N_LRU_BLOCKS // blocks_per_step),
        in_specs=[
            seq_spec, seq_spec,
            pl.BlockSpec((4, width), lambda bi, n: (0, n)),
            pl.BlockSpec((1, width), lambda bi, n: (0, n)),
            pl.BlockSpec((blocks_per_step, 2 * LRU_BLOCK, 4 * LRU_BLOCK), lambda bi, n: (n, 0, 0)),
            pl.BlockSpec((2, width), lambda bi, n: (0, n)),
            st_spec,
        ],
        out_specs=[seq_spec, st_spec],
        out_shape=[jax.ShapeDtypeStruct((b, t_len, D_LRU), BF16),
                   jax.ShapeDtypeStruct((b, 2, D_LRU), F32)],
        scratch_shapes=[
            pltpu.VMEM((blocks_per_step, t_len + 2 * SUBLANES, LRU_BLOCK), F32),
            scan_buf, scan_buf, scan_buf, scan_buf,
            pltpu.VMEM((blocks_per_step, n_chunks, LRU_BLOCK), F32),
            pltpu.VMEM((blocks_per_step, n_chunks, LRU_BLOCK), F32),
        ],
        compiler_params=pltpu.CompilerParams(
            dimension_semantics=("arbitrary",) * 2, vmem_limit_bytes=V7X_VMEM_LIMIT_BYTES),
        name="rglru",
    )(x_lru, g_lru, conv_w, conv_b, w_ext, lru_lam, h0)


def _outproj_kernel(att_ref, lru_ref, x_ref, gate_ref, wa_ref, wl_ref, gpost_ref, y_ref):
    o = _dot(att_ref[...], wa_ref[...]) + _dot(lru_ref[...], wl_ref[...])
    ms = jnp.mean(o * o, axis=-1, keepdims=True)
    n = o * lax.rsqrt(ms + EPS) * gpost_ref[...]
    y_ref[...] = x_ref[...] + gate_ref[...] * n


def _outproj(att, lru, x, gate, w_att, w_lru, g_post, *, t_len, tm=1024):
    n_tok = x.shape[0]
    nb = gate.shape[0]
    if nb == 1:
        tm = min(tm, n_tok)
        mod_map = lambda i: (0, 0, 0)
    else:
        tm = min(tm, t_len)
        tiles_per_batch = t_len // tm
        mod_map = lambda i: (i // tiles_per_batch, 0, 0)
    tok_spec = pl.BlockSpec((tm, D_MODEL), lambda i: (i, 0))
    w_spec = pl.BlockSpec((D_MODEL, D_MODEL), lambda i: (0, 0))
    return pl.pallas_call(
        _outproj_kernel,
        grid=(n_tok // tm,),
        in_specs=[tok_spec, tok_spec, tok_spec,
                  pl.BlockSpec((None, 1, D_MODEL), mod_map),
                  w_spec, w_spec,
                  pl.BlockSpec((1, D_MODEL), lambda i: (0, 0))],
        out_specs=tok_spec,
        out_shape=jax.ShapeDtypeStruct((n_tok, D_MODEL), F32),
        compiler_params=pltpu.CompilerParams(
            dimension_semantics=("arbitrary",), vmem_limit_bytes=V7X_VMEM_LIMIT_BYTES),
        name="outproj",
    )(att, lru, x, gate, w_att, w_lru, g_post)


def _sublayer(x, scale, shift, gate, cache_k, cache_v, h0, use_rope, w):
    b, t_len, _ = x.shape
    xf = x.reshape(b * t_len, D_MODEL)
    emit_kv = cache_k is None
    outs = _inproj(xf, scale, shift, w["g_pre"], w["w_in"], t_len=t_len,
                   use_rope=use_rope, emit_kv_f32=emit_kv)
    qt, vt = outs[0], outs[2]
    k, g_att, x_lru, g_lru = [outs[i].reshape(b, t_len, D_MODEL) for i in (1, 3, 4, 5)]
    att = _attention(w["lam"], w["g_subln"], qt, k, vt, g_att, cache_k, cache_v)
    lru, state = _lru(x_lru, g_lru, w["conv_w"], w["conv_b"], w["w_ext"], w["lru_lam"], h0)
    y = _outproj(att.reshape(b * t_len, D_ATT), lru.reshape(b * t_len, D_LRU), xf, gate,
                 w["w_out_att"], w["w_out_lru"], w["g_post"], t_len=t_len)
    y = y.reshape(b, t_len, D_MODEL)
    if emit_kv:
        return y, outs[6], outs[7], state
    return y, None, None, state


def kernel(x_prompt, x_sample, cache_k, cache_v, state_lru, c, c_ctx, w_ada, b_ada, g_pre, w_in, lambda_q1, lambda_k1, lambda_q2, lambda_k2, g_subln, conv_w, conv_b, w_rgate, b_rgate, w_igate, b_igate, lru_lambda, w_out, g_post):
    bp, seq, _ = x_prompt.shape
    bd, kc = cache_k.shape[0], cache_k.shape[2]
    assert cache_k.shape[1] == 1 and w_in.shape[0] == 1, "single-layer step only"
    l = 0

    cond = jnp.concatenate([c, c_ctx[None, :], jnp.zeros((SUBLANES - bd - 1, D_MODEL), F32)], axis=0)
    mod = _modulation(cond, w_ada[l], b_ada[l][None, :])
    shift, scale, gate = [mod[:, i * D_MODEL:(i + 1) * D_MODEL] for i in range(3)]

    def rows(a, lo, hi):
        return a[lo:hi][:, None, :]

    w = {
        "g_pre": g_pre[l][None, :],
        "w_in": w_in[l].astype(BF16),
        "lam": jnp.stack([lambda_q1[l], lambda_k1[l], lambda_q2[l], lambda_k2[l]], axis=0),
        "g_subln": g_subln[l][None, :],
        "conv_w": conv_w[l],
        "conv_b": conv_b[l][None, :],
        "w_ext": _lru_gate_operand(w_rgate[l], w_igate[l], b_rgate[l], b_igate[l]),
        "lru_lam": lru_lambda[l],
        "w_out_att": w_out[l, :D_ATT].astype(BF16),
        "w_out_lru": w_out[l, D_ATT:].astype(BF16),
        "g_post": g_post[l][None, :],
    }

    y_p, new_k, new_v, st_p = _sublayer(
        x_prompt, rows(scale, bd, bd + 1), rows(shift, bd, bd + 1), rows(gate, bd, bd + 1),
        None, None, jnp.zeros((bp, 2, D_LRU), F32), False, w)

    y_s, _, _, _ = _sublayer(
        x_sample, rows(scale, 0, bd), rows(shift, 0, bd), rows(gate, 0, bd),
        cache_k.reshape(bd, kc * N_HEADS, HEAD_W), cache_v.reshape(bd, kc * N_HEADS, HEAD_W),
        state_lru.reshape(bd, 2, D_LRU), True, w)

    new_k = new_k.reshape(bp, 1, seq, N_HEADS, HEAD_W)
    new_v = new_v.reshape(bp, 1, seq, N_HEADS, HEAD_W)
    return (y_p, y_s, new_k, new_v, st_p[:, None])
```

```python
import functools
import math

import jax
import jax.numpy as jnp
import numpy as np
from jax import lax
from jax.experimental import pallas as pl
from jax.experimental.pallas import tpu as pltpu

F32 = jnp.float32
BF16 = jnp.bfloat16

D_MODEL = 1024
GRID_W = 64
N_HEADS = 8
DIFF_HEAD_DIM = 64
HEAD_W = 2 * DIFF_HEAD_DIM
D_ATT = N_HEADS * HEAD_W
D_LRU = 1024
N_LRU_BLOCKS = 8
LRU_BLOCK = D_LRU // N_LRU_BLOCKS
LRU_C = 8.0
N_GROUPS = 6
ROPE_BASE = 10000.0
EPS = 1e-6
LAM_INIT = 0.8 - 0.6 * math.exp(-0.3 * 0)
LOG2_E = math.log2(math.e)

V7X_VMEM_LIMIT_BYTES = 56 * 1024 * 1024
SUBLANES = 8
ROPE_SWAP = DIFF_HEAD_DIM // 4
BF16_ROWS_PER_VREG = 16
ACC_ROWS = HEAD_W + BF16_ROWS_PER_VREG
CACHE_CHUNK = 256
BIAS_SPLIT = 3


def _silu(x):
    return x * jax.nn.sigmoid(x)


def _dot(a, b):
    return jnp.dot(a, b, preferred_element_type=F32)


def _mod_kernel(cond_ref, w_ref, b_ref, o_ref):
    s = _silu(cond_ref[...])
    o_ref[...] = _dot(s.astype(BF16), w_ref[...].astype(BF16)) + b_ref[...]


def _modulation(cond, w_ada, b_ada):
    n = cond.shape[0]
    return pl.pallas_call(
        _mod_kernel,
        grid=(3,),
        in_specs=[
            pl.BlockSpec((n, D_MODEL), lambda j: (0, 0)),
            pl.BlockSpec((D_MODEL, D_MODEL), lambda j: (0, j)),
            pl.BlockSpec((1, D_MODEL), lambda j: (0, j)),
        ],
        out_specs=pl.BlockSpec((n, D_MODEL), lambda j: (0, j)),
        out_shape=jax.ShapeDtypeStruct((n, 3 * D_MODEL), F32),
        name="modulation",
    )(cond, w_ada, b_ada)


def _rope_tables(t_len):
    half = DIFF_HEAD_DIM // 2
    nf = half // 2
    t = np.arange(t_len)
    row = (t // GRID_W).astype(np.float32)
    col = (t % GRID_W).astype(np.float32)
    inv = (ROPE_BASE ** (-np.arange(nf, dtype=np.float32) * 2.0 / half)).astype(np.float32)
    lane = np.arange(HEAD_W) % DIFF_HEAD_DIM
    use_row = lane < half
    freq = (lane % half) % nf
    first = (lane % half) < nf
    pos = np.where(use_row[None, :], row[:, None], col[:, None]).astype(np.float32)
    ang = (pos * inv[freq][None, :]).astype(np.float32).astype(np.float64)
    cos = np.cos(ang).astype(np.float32)
    sin = (np.sin(ang) * np.where(first, -1.0, 1.0)[None, :]).astype(np.float32)
    return jnp.asarray(cos), jnp.asarray(sin)


def _inproj_kernel(*refs, use_rope, emit_kv_f32):
    x_ref, scale_ref, shift_ref, gpre_ref, w_ref = refs[:5]
    refs = refs[5:]
    if use_rope:
        cos_ref, sin_ref = refs[:2]
        refs = refs[2:]
    q_ref, k_ref, v_ref, gatt_ref, xlru_ref, glru_ref = refs[:6]
    refs = refs[6:]
    if emit_kv_f32:
        kf_ref, vf_ref = refs

    x = x_ref[...]
    ms = jnp.mean(x * x, axis=-1, keepdims=True)
    y = x * lax.rsqrt(ms + EPS) * gpre_ref[...]
    h = (y * (1.0 + scale_ref[...]) + shift_ref[...]).astype(BF16)

    def proj(g):
        return _dot(h, w_ref[:, g * D_MODEL:(g + 1) * D_MODEL])

    if use_rope:
        cos = cos_ref[...]
        sin = sin_ref[...]
        lane = lax.broadcasted_iota(jnp.int32, cos.shape, 1)
        take_next = (lane % (2 * ROPE_SWAP)) < ROPE_SWAP

        def rope(p):
            outs = []
            for hd in range(N_HEADS):
                xh = p[:, hd * HEAD_W:(hd + 1) * HEAD_W]
                partner = jnp.where(take_next,
                                    pltpu.roll(xh, HEAD_W - ROPE_SWAP, 1),
                                    pltpu.roll(xh, ROPE_SWAP, 1))
                outs.append(xh * cos + partner * sin)
            return jnp.concatenate(outs, axis=-1)
    else:
        def rope(p):
            return p

    q = rope(proj(0)) * (LOG2_E / math.sqrt(DIFF_HEAD_DIM))
    q_ref[...] = q.T.astype(BF16)
    k = rope(proj(1))
    k_ref[...] = k.astype(BF16)
    v = proj(2)
    vt = v.T
    for hd in range(N_HEADS):
        v_ref[hd] = vt[hd * HEAD_W:(hd + 1) * HEAD_W].astype(BF16)
    if emit_kv_f32:
        kf_ref[...] = k
        vf_ref[...] = v
    gatt_ref[...] = proj(3)
    xlru_ref[...] = proj(4)
    glru_ref[...] = proj(5)


def _inproj(x, scale, shift, g_pre, w_in_bf16, *, t_len, use_rope, emit_kv_f32, tm=512):
    n_tok = x.shape[0]
    tm = min(tm, t_len)
    tiles_per_batch = t_len // tm
    nb = scale.shape[0]
    if nb == 1:
        mod_map = lambda i: (0, 0, 0)
    else:
        mod_map = lambda i: (i // tiles_per_batch, 0, 0)
    tok_spec = pl.BlockSpec((tm, D_MODEL), lambda i: (i, 0))
    in_specs = [
        tok_spec,
        pl.BlockSpec((None, 1, D_MODEL), mod_map),
        pl.BlockSpec((None, 1, D_MODEL), mod_map),
        pl.BlockSpec((1, D_MODEL), lambda i: (0, 0)),
        pl.BlockSpec((D_MODEL, N_GROUPS * D_MODEL), lambda i: (0, 0), pipeline_mode=pl.Buffered(1)),
    ]
    args = [x, scale, shift, g_pre, w_in_bf16]
    if use_rope:
        cos, sin = _rope_tables(t_len)
        rope_spec = pl.BlockSpec((tm, HEAD_W), lambda i: (i % tiles_per_batch, 0))
        in_specs += [rope_spec, rope_spec]
        args += [cos, sin]
    n_batch = n_tok // t_len
    kc = _key_chunk(t_len)
    tiles_per_chunk = kc // tm
    out_specs = [
        pl.BlockSpec((None, D_MODEL, tm), lambda i: (i // tiles_per_batch, 0, i % tiles_per_batch)),
        tok_spec,
        pl.BlockSpec((None, N_HEADS, None, HEAD_W, tm),
                     lambda i: (i // tiles_per_batch, 0, (i % tiles_per_batch) // tiles_per_chunk, 0,
                                i % tiles_per_chunk)),
        tok_spec, tok_spec, tok_spec,
    ]
    out_shape = [
        jax.ShapeDtypeStruct((n_batch, D_MODEL, t_len), BF16),
        jax.ShapeDtypeStruct((n_tok, D_MODEL), BF16),
        jax.ShapeDtypeStruct((n_batch, N_HEADS, t_len // kc, HEAD_W, kc), BF16),
    ] + [jax.ShapeDtypeStruct((n_tok, D_MODEL), F32)] * 3
    if emit_kv_f32:
        out_specs += [tok_spec] * 2
        out_shape += [jax.ShapeDtypeStruct((n_tok, D_MODEL), F32)] * 2
    return pl.pallas_call(
        functools.partial(_inproj_kernel, use_rope=use_rope, emit_kv_f32=emit_kv_f32),
        grid=(n_tok // tm,),
        in_specs=in_specs,
        out_specs=out_specs,
        out_shape=out_shape,
        compiler_params=pltpu.CompilerParams(
            dimension_semantics=("arbitrary",), vmem_limit_bytes=V7X_VMEM_LIMIT_BYTES),
        name="inproj_rope" if use_rope else "inproj",
    )(*args)


def _key_chunk(t_len):
    return min(512, t_len)


def _attn_kernel(*refs, has_cache, n_chunks, kc, heads_per_step):
    lam_ref, gsub_ref, qt_ref, k_ref, vt_ref, gatt_ref = refs[:6]
    refs = refs[6:]
    cache_refs = None
    if has_cache:
        cache_refs = refs[:2]
        refs = refs[2:]
    o_ref, acc_ref, s_ref, e_ref = refs

    lp = lam_ref[...]
    lam = (jnp.exp(jnp.sum(lp[0:1] * lp[1:2], axis=-1, keepdims=True))
           - jnp.exp(jnp.sum(lp[2:3] * lp[3:4], axis=-1, keepdims=True)) + LAM_INIT)
    g_subln = gsub_ref[...]

    if n_chunks == 1 and not has_cache:
        _attn_single_chunk(lam, g_subln, qt_ref, k_ref, vt_ref, gatt_ref, o_ref, s_ref, e_ref,
                           n_heads=heads_per_step)
        return
    for hh in range(heads_per_step):
        lanes = slice(hh * HEAD_W, (hh + 1) * HEAD_W)
        _attn_accumulate(qt_ref.at[lanes, :], k_ref.at[:, lanes], vt_ref.at[hh], cache_refs,
                         pl.program_id(1) * heads_per_step + hh, acc_ref, s_ref, e_ref,
                         n_chunks=n_chunks, kc=kc)
        _attn_finalize(lam, g_subln, acc_ref[0], acc_ref[1], gatt_ref.at[:, lanes], o_ref.at[:, lanes])


def _query_maps(qt):
    row = lax.broadcasted_iota(jnp.int32, qt.shape, 0)
    zero = jnp.zeros_like(qt)
    return jnp.where(row < DIFF_HEAD_DIM, qt, zero), jnp.where(row >= DIFF_HEAD_DIM, qt, zero)


def _with_ones(vt_chunk):
    ones = jnp.ones((ACC_ROWS - HEAD_W, vt_chunk.shape[1]), BF16)
    return jnp.concatenate([vt_chunk, ones], axis=0)


def _attn_single_chunk(lam, g_subln, qt_ref, k_ref, vt_ref, gatt_ref, o_ref, s_ref, e_ref, *, n_heads):
    maxes = {}
    for t in range(n_heads + 2):
        if t < n_heads:
            lanes = slice(t * HEAD_W, (t + 1) * HEAD_W)
            keys = k_ref[:, lanes]
            q_maps = _query_maps(qt_ref[lanes, :])
            head_max = []
            for idx in range(2):
                st = _dot(keys, q_maps[idx])
                s_ref[t % 2, idx] = st
                head_max.append(jnp.max(st, axis=0, keepdims=True))
            maxes[t] = head_max
        if 1 <= t <= n_heads:
            h = t - 1
            for idx in range(2):
                e_ref[h % 2, idx] = jnp.exp2(s_ref[h % 2, idx] - maxes[h][idx]).astype(BF16)
        if t >= 2:
            h = t - 2
            lanes = slice(h * HEAD_W, (h + 1) * HEAD_W)
            vt_ext = _with_ones(vt_ref[h, 0])
            acc1, acc2 = [_dot(vt_ext, e_ref[h % 2, idx]) for idx in range(2)]
            _attn_finalize(lam, g_subln, acc1, acc2, gatt_ref.at[:, lanes], o_ref.at[:, lanes])


def _attn_finalize(lam, g_subln, acc1, acc2, gatt_ref, o_ref):
    o1 = acc1[:HEAD_W] * (1.0 / acc1[HEAD_W:HEAD_W + 1])
    o2 = acc2[:HEAD_W] * (1.0 / acc2[HEAD_W:HEAD_W + 1])
    ot = o1 - lam * o2
    ms = jnp.mean(ot * ot, axis=0, keepdims=True)
    on = (ot * lax.rsqrt(ms + EPS)).T * (g_subln * (1.0 - LAM_INIT))
    o_ref[...] = (on * _silu(gatt_ref[...])).astype(BF16)


def _attn_accumulate(qt_ref, k_ref, vt_ref, cache_refs, head, acc_ref, s_ref, e_ref, *, n_chunks, kc):
    has_cache = cache_refs is not None
    tq = qt_ref.shape[1]
    q_maps = _query_maps(qt_ref[...])

    acc_ref[...] = jnp.zeros(acc_ref.shape, F32)

    n_cache = 0
    if has_cache:
        kc_ref, vc_ref = cache_refs
        cache_len = kc_ref.shape[0] // N_HEADS
        cache_chunk = min(CACHE_CHUNK, cache_len)
        n_cache = cache_len // cache_chunk
    n_total = n_chunks + n_cache

    def is_cache(c):
        return isinstance(c, int) and c < n_cache

    def cache_rows(c):
        return pl.ds(head + c * cache_chunk * N_HEADS, cache_chunk, stride=N_HEADS)

    def keys_of(c):
        if is_cache(c):
            return kc_ref[cache_rows(c), :].astype(BF16)
        start = (c - n_cache) * kc
        return k_ref[pl.ds(start if isinstance(c, int) else pl.multiple_of(start, kc), kc), :]

    def values_of(c):
        if is_cache(c):
            return vc_ref[cache_rows(c), :].T.astype(BF16)
        return vt_ref[c - n_cache]

    def scores(c, slot):
        k_chunk = keys_of(c)
        maxes = []
        for idx in range(2):
            st = _dot(k_chunk, q_maps[idx])
            s_ref[slot, idx, :k_chunk.shape[0], :] = st
            maxes.append(jnp.max(st, axis=0, keepdims=True))
        return tuple(maxes)

    def probs(n_keys, slot, ms, maxes):
        new_ms, alphas = [], []
        for idx in range(2):
            m_new = jnp.maximum(ms[idx], maxes[idx])
            alphas.append(jnp.exp2(ms[idx] - m_new))
            e_ref[slot, idx, :n_keys, :] = jnp.exp2(s_ref[slot, idx, :n_keys, :] - m_new).astype(BF16)
            new_ms.append(m_new)
        return tuple(new_ms), tuple(alphas)

    def values(c, slot, alphas):
        vt_ext = _with_ones(values_of(c))
        n_keys = vt_ext.shape[1]
        for idx in range(2):
            acc_ref[idx] = acc_ref[idx] * alphas[idx] + _dot(vt_ext, e_ref[slot, idx, :n_keys, :])

    def n_keys_of(c):
        return cache_chunk if is_cache(c) else kc

    def step(i, slot, ms, maxes_next, alphas_cur):
        maxes_after = scores(i + 2, slot)
        ms, alphas_next = probs(n_keys_of(i + 1), 1 - slot, ms, maxes_next)
        values(i, slot, alphas_cur)
        return ms, maxes_after, alphas_next

    m0 = jnp.full((1, tq), -jnp.inf, F32)
    ms = (m0, m0)
    maxes_cur = scores(0, 0)
    if n_total == 1:
        ms, alphas_cur = probs(n_keys_of(0), 0, ms, maxes_cur)
        values(0, 0, alphas_cur)
    else:
        maxes_next = scores(1, 1)
        ms, alphas_cur = probs(n_keys_of(0), 0, ms, maxes_cur)
        n_steps = n_total - 2
        first_loop = n_cache
        for i in range(min(first_loop, n_steps)):
            ms, maxes_next, alphas_cur = step(i, i % 2, ms, maxes_next, alphas_cur)
        n_pairs = max(n_steps - first_loop, 0) // 2

        def pair(t, carry):
            ms, maxes_next, alphas_cur = carry
            i = first_loop + 2 * t
            ms, maxes_next, alphas_cur = step(i, first_loop % 2, ms, maxes_next, alphas_cur)
            return step(i + 1, (first_loop + 1) % 2, ms, maxes_next, alphas_cur)

        ms, maxes_next, alphas_cur = lax.fori_loop(0, n_pairs, pair, (ms, maxes_next, alphas_cur))
        for i in range(first_loop + 2 * n_pairs, n_steps):
            ms, maxes_next, alphas_cur = step(i, i % 2, ms, maxes_next, alphas_cur)
        last = n_total - 1
        ms, alphas_last = probs(n_keys_of(last), last % 2, ms, maxes_next)
        values(last - 1, (last - 1) % 2, alphas_cur)
        values(last, last % 2, alphas_last)


def _attention(lam_params, g_subln, qt, k, vt, g_att, cache_k=None, cache_v=None, *, tq=2048):
    b, t_len, _ = k.shape
    has_cache = cache_k is not None
    n_chunks, kc = vt.shape[2], vt.shape[4]
    tq = min(tq, t_len)
    heads_per_step = N_HEADS if n_chunks == 1 and not has_cache else 1
    width = heads_per_step * HEAD_W
    s_rows = max(kc, cache_k.shape[1] // N_HEADS) if has_cache else kc
    tok_spec = pl.BlockSpec((None, tq, width), lambda bi, h, qi: (bi, qi, h))
    in_specs = [
        pl.BlockSpec(lam_params.shape, lambda bi, h, qi: (0, 0)),
        pl.BlockSpec((1, HEAD_W), lambda bi, h, qi: (0, 0)),
        pl.BlockSpec((None, width, tq), lambda bi, h, qi: (bi, h, qi)),
        pl.BlockSpec((None, t_len, width), lambda bi, h, qi: (bi, 0, h)),
        pl.BlockSpec((None, heads_per_step, n_chunks, HEAD_W, kc), lambda bi, h, qi: (bi, h, 0, 0, 0)),
        tok_spec,
    ]
    args = [lam_params, g_subln, qt, k, vt, g_att]
    if has_cache:
        c_spec = pl.BlockSpec((None,) + cache_k.shape[1:], lambda bi, h, qi: (bi, 0, 0))
        in_specs += [c_spec, c_spec]
        args += [cache_k, cache_v]
    return pl.pallas_call(
        functools.partial(_attn_kernel, has_cache=has_cache, n_chunks=n_chunks, kc=kc,
                          heads_per_step=heads_per_step),
        grid=(b, N_HEADS // heads_per_step, t_len // tq),
        in_specs=in_specs,
        out_specs=tok_spec,
        out_shape=jax.ShapeDtypeStruct((b, t_len, D_ATT), BF16),
        scratch_shapes=[pltpu.VMEM((2, ACC_ROWS, tq), F32),
                        pltpu.VMEM((2, 2, s_rows, tq), F32),
                        pltpu.VMEM((2, 2, s_rows, tq), BF16)],
        compiler_params=pltpu.CompilerParams(
            dimension_semantics=("arbitrary",) * 3, vmem_limit_bytes=V7X_VMEM_LIMIT_BYTES),
        name="diff_attn_cache" if has_cache else "diff_attn",
    )(*args)


def _lru_kernel(x_ref, g_ref, cw_ref, cb_ref, w_ref, lam_ref, h0_ref,
                o_ref, st_ref,
                xp, a_f, b_f, a_b, b_b, hin_f, hin_b,
                *, blocks_per_step, t_len, chunk, pitch, rows):
    blocks = range(blocks_per_step)
    lanes = [slice(j * LRU_BLOCK, (j + 1) * LRU_BLOCK) for j in blocks]
    g_f, p_f, g_b, p_b = b_f, a_f, b_b, a_b
    n_chunks = t_len // chunk
    chunks_per_tile = rows // chunk

    pad = jnp.zeros((SUBLANES, LRU_BLOCK), F32)
    for j in blocks:
        xp[j, pl.ds(0, SUBLANES), :] = pad
        xp[j, pl.ds(SUBLANES, t_len), :] = x_ref[:, lanes[j]]
        xp[j, pl.ds(SUBLANES + t_len, SUBLANES), :] = pad

    lam = lam_ref[...]
    sp = jnp.maximum(-lam, 0.0) + jnp.log1p(jnp.exp(-jnp.abs(lam)))
    half_scale = (-0.5 * LRU_C) * sp
    cw = cw_ref[...]
    cb = cb_ref[...]
    lane = lax.broadcasted_iota(jnp.int32, (rows, LRU_BLOCK), 1)
    bias_taps = jnp.where(lane < BIAS_SPLIT, 1.0, 0.0).astype(BF16)

    def gate_tile(r, carry):
        t0 = pl.multiple_of(r * rows, SUBLANES)
        for j in blocks:
            u = cb[:, lanes[j]]
            for tap in range(4):
                u = u + xp[j, pl.ds(t0 + SUBLANES - 1 + tap, rows), :] * cw[tap:tap + 1, lanes[j]]
            zh = _dot(jnp.concatenate([u.astype(BF16), bias_taps], axis=1), w_ref[j])
            u_half = 0.5 * u
            for d, (a_s, b_s) in enumerate(((a_f, b_f), (a_b, b_b))):
                scale = half_scale[d:d + 1, lanes[j]]
                tr = jnp.tanh(zh[:, (2 * d) * LRU_BLOCK:(2 * d + 1) * LRU_BLOCK])
                ti = jnp.tanh(zh[:, (2 * d + 1) * LRU_BLOCK:(2 * d + 2) * LRU_BLOCK])
                log_a = scale + scale * tr
                a = jnp.exp(log_a)
                gain_sq = jnp.tanh(log_a) * (-1.0 - a * a)
                gain = jnp.where(gain_sq > 0.0, gain_sq * lax.rsqrt(gain_sq), 0.0)
                bb = gain * (u_half + u_half * ti)
                for cc in range(chunks_per_tile):
                    dst = pl.multiple_of(r * (chunks_per_tile * pitch), SUBLANES) + cc * pitch
                    a_s[j, pl.ds(dst, chunk), :] = a[cc * chunk:(cc + 1) * chunk]
                    b_s[j, pl.ds(dst, chunk), :] = bb[cc * chunk:(cc + 1) * chunk]
        return carry

    lax.fori_loop(0, t_len // rows, gate_tile, 0)

    def local_step(l, carry):
        out = []
        for j in blocks:
            hf, pf, hb, pb = carry[4 * j:4 * j + 4]
            rf = pl.ds(l, n_chunks, stride=pitch)
            a = a_f[j, rf, :]
            hf = a * hf + b_f[j, rf, :]
            pf = a * pf
            g_f[j, rf, :] = hf
            p_f[j, rf, :] = pf
            rb = pl.ds(chunk - 1 - l, n_chunks, stride=pitch)
            a = a_b[j, rb, :]
            hb = a * hb + b_b[j, rb, :]
            pb = a * pb
            g_b[j, rb, :] = hb
            p_b[j, rb, :] = pb
            out += [hf, pf, hb, pb]
        return tuple(out)

    z0 = jnp.zeros((n_chunks, LRU_BLOCK), F32)
    o0 = jnp.ones((n_chunks, LRU_BLOCK), F32)
    chunk_maps = lax.fori_loop(0, chunk, local_step, (z0, o0, z0, o0) * blocks_per_step, unroll=4)

    h0 = h0_ref[...]
    chunk_id = lax.broadcasted_iota(jnp.int32, (n_chunks, LRU_BLOCK), 0)

    def compose(end, decay, towards_higher):
        shift = 1
        while shift < n_chunks:
            if towards_higher:
                valid = chunk_id >= shift
                amount = shift
            else:
                valid = chunk_id < n_chunks - shift
                amount = n_chunks - shift
            prev_end = jnp.where(valid, pltpu.roll(end, amount, 0), 0.0)
            prev_decay = jnp.where(valid, pltpu.roll(decay, amount, 0), 1.0)
            end = end + decay * prev_end
            decay = decay * prev_decay
            shift *= 2
        return end, decay

    for j in blocks:
        end_f, decay_f, end_b, decay_b = chunk_maps[4 * j:4 * j + 4]
        h0_f = h0[0:1, lanes[j]]
        h0_b = h0[1:2, lanes[j]]
        end_f, decay_f = compose(end_f, decay_f, True)
        after_f = end_f + decay_f * h0_f
        hin_f[j] = jnp.where(chunk_id == 0, h0_f, pltpu.roll(after_f, 1, 0))
        st_ref[0:1, lanes[j]] = after_f[n_chunks - 1:n_chunks]
        end_b, decay_b = compose(end_b, decay_b, False)
        after_b = end_b + decay_b * h0_b
        hin_b[j] = jnp.where(chunk_id == n_chunks - 1, h0_b, pltpu.roll(after_b, n_chunks - 1, 0))
        st_ref[1:2, lanes[j]] = after_b[0:1]

    def out_pair(cp, carry):
        for par in range(2):
            c = 2 * cp + par
            src = pl.ds(pl.multiple_of(cp * (2 * pitch), SUBLANES) + par * pitch, chunk)
            dst = pl.ds(pl.multiple_of(c * chunk, SUBLANES), chunk)
            for j in blocks:
                hf = g_f[j, src, :] + p_f[j, src, :] * hin_f[j, pl.ds(c, 1), :]
                hb = g_b[j, src, :] + p_b[j, src, :] * hin_b[j, pl.ds(c, 1), :]
                o_ref[dst, lanes[j]] = ((hf + hb) * _silu(g_ref[dst, lanes[j]])).astype(BF16)
        return carry

    lax.fori_loop(0, n_chunks // 2, out_pair, 0, unroll=4)


def _lru_gate_operand(w_r, w_i, b_r, b_i):
    weights = 0.5 * jnp.concatenate([w_r[0], w_i[0], w_r[1], w_i[1]], axis=-1)
    bias = 0.5 * jnp.concatenate(
        [bb.reshape(N_LRU_BLOCKS, 1, LRU_BLOCK) for bb in (b_r[0], b_i[0], b_r[1], b_i[1])], axis=-1)
    parts, rest = [], bias
    for _ in range(BIAS_SPLIT):
        gamma = rest * (2.0 ** 16 + 1.0)
        part = gamma - (gamma - rest)
        parts.append(part)
        rest = rest - part
    bias_rows = jnp.pad(jnp.concatenate(parts, axis=1), ((0, 0), (0, LRU_BLOCK - BIAS_SPLIT), (0, 0)))
    return jnp.concatenate([weights, bias_rows], axis=1).astype(BF16)


def _lru(x_lru, g_lru, conv_w, conv_b, w_ext, lru_lam, h0):
    b, t_len, _ = x_lru.shape
    chunk = 64 if t_len >= 2048 else 32
    pitch = chunk + SUBLANES // 2
    n_chunks = t_len // chunk
    rows = min(4096, t_len)
    blocks_per_step = N_LRU_BLOCKS if t_len <= 512 else 1
    width = blocks_per_step * LRU_BLOCK
    seq_spec = pl.BlockSpec((None, t_len, width), lambda bi, n: (bi, 0, n))
    st_spec = pl.BlockSpec((None, 2, width), lambda bi, n: (bi, 0, n))
    scan_buf = pltpu.VMEM((blocks_per_step, n_chunks * pitch, LRU_BLOCK), F32)
    return pl.pallas_call(
        functools.partial(_lru_kernel, blocks_per_step=blocks_per_step,
                          t_len=t_len, chunk=chunk, pitch=pitch, rows=rows),
        grid=(b, N_LRU_BLOCKS // blocks_per_step),
        in_specs=[
            seq_spec, seq_spec,
            pl.BlockSpec((4, width), lambda bi, n: (0, n)),
            pl.BlockSpec((1, width), lambda bi, n: (0, n)),
            pl.BlockSpec((blocks_per_step, 2 * LRU_BLOCK, 4 * LRU_BLOCK), lambda bi, n: (n, 0, 0)),
            pl.BlockSpec((2, width), lambda bi, n: (0, n)),
            st_spec,
        ],
        out_specs=[seq_spec, st_spec],
        out_shape=[jax.ShapeDtypeStruct((b, t_len, D_LRU), BF16),
                   jax.ShapeDtypeStruct((b, 2, D_LRU), F32)],
        scratch_shapes=[
            pltpu.VMEM((blocks_per_step, t_len + 2 * SUBLANES, LRU_BLOCK), F32),
            scan_buf, scan_buf, scan_buf, scan_buf,
            pltpu.VMEM((blocks_per_step, n_chunks, LRU_BLOCK), F32),
            pltpu.VMEM((blocks_per_step, n_chunks, LRU_BLOCK), F32),
        ],
        compiler_params=pltpu.CompilerParams(
            dimension_semantics=("arbitrary",) * 2, vmem_limit_bytes=V7X_VMEM_LIMIT_BYTES),
        name="rglru",
    )(x_lru, g_lru, conv_w, conv_b, w_ext, lru_lam, h0)


def _outproj_kernel(att_ref, lru_ref, x_ref, gate_ref, wa_ref, wl_ref, gpost_ref, y_ref):
    o = _dot(att_ref[...], wa_ref[...]) + _dot(lru_ref[...], wl_ref[...])
    ms = jnp.mean(o * o, axis=-1, keepdims=True)
    n = o * lax.rsqrt(ms + EPS) * gpost_ref[...]
    y_ref[...] = x_ref[...] + gate_ref[...] * n


def _outproj(att, lru, x, gate, w_att, w_lru, g_post, *, t_len, tm=1024):
    n_tok = x.shape[0]
    nb = gate.shape[0]
    if nb == 1:
        tm = min(tm, n_tok)
        mod_map = lambda i: (0, 0, 0)
    else:
        tm = min(tm, t_len)
        tiles_per_batch = t_len // tm
        mod_map = lambda i: (i // tiles_per_batch, 0, 0)
    tok_spec = pl.BlockSpec((tm, D_MODEL), lambda i: (i, 0))
    w_spec = pl.BlockSpec((D_MODEL, D_MODEL), lambda i: (0, 0))
    return pl.pallas_call(
        _outproj_kernel,
        grid=(n_tok // tm,),
        in_specs=[tok_spec, tok_spec, tok_spec,
                  pl.BlockSpec((None, 1, D_MODEL), mod_map),
                  w_spec, w_spec,
                  pl.BlockSpec((1, D_MODEL), lambda i: (0, 0))],
        out_specs=tok_spec,
        out_shape=jax.ShapeDtypeStruct((n_tok, D_MODEL), F32),
        compiler_params=pltpu.CompilerParams(
            dimension_semantics=("arbitrary",), vmem_limit_bytes=V7X_VMEM_LIMIT_BYTES),
        name="outproj",
    )(att, lru, x, gate, w_att, w_lru, g_post)


def _sublayer(x, scale, shift, gate, cache_k, cache_v, h0, use_rope, w):
    b, t_len, _ = x.shape
    xf = x.reshape(b * t_len, D_MODEL)
    emit_kv = cache_k is None
    outs = _inproj(xf, scale, shift, w["g_pre"], w["w_in"], t_len=t_len,
                   use_rope=use_rope, emit_kv_f32=emit_kv)
    qt, vt = outs[0], outs[2]
    k, g_att, x_lru, g_lru = [outs[i].reshape(b, t_len, D_MODEL) for i in (1, 3, 4, 5)]
    att = _attention(w["lam"], w["g_subln"], qt, k, vt, g_att, cache_k, cache_v)
    lru, state = _lru(x_lru, g_lru, w["conv_w"], w["conv_b"], w["w_ext"], w["lru_lam"], h0)
    y = _outproj(att.reshape(b * t_len, D_ATT), lru.reshape(b * t_len, D_LRU), xf, gate,
                 w["w_out_att"], w["w_out_lru"], w["g_post"], t_len=t_len)
    y = y.reshape(b, t_len, D_MODEL)
    if emit_kv:
        return y, outs[6], outs[7], state
    return y, None, None, state


def kernel(x_prompt, x_sample, cache_k, cache_v, state_lru, c, c_ctx, w_ada, b_ada, g_pre, w_in, lambda_q1, lambda_k1, lambda_q2, lambda_k2, g_subln, conv_w, conv_b, w_rgate, b_rgate, w_igate, b_igate, lru_lambda, w_out, g_post):
    bp, seq, _ = x_prompt.shape
    bd, kc = cache_k.shape[0], cache_k.shape[2]
    assert cache_k.shape[1] == 1 and w_in.shape[0] == 1, "single-layer step only"
    l = 0

    cond = jnp.concatenate([c, c_ctx[None, :], jnp.zeros((SUBLANES - bd - 1, D_MODEL), F32)], axis=0)
    mod = _modulation(cond, w_ada[l], b_ada[l][None, :])
    shift, scale, gate = [mod[:, i * D_MODEL:(i + 1) * D_MODEL] for i in range(3)]

    def rows(a, lo, hi):
        return a[lo:hi][:, None, :]

    w = {
        "g_pre": g_pre[l][None, :],
        "w_in": w_in[l].astype(BF16),
        "lam": jnp.stack([lambda_q1[l], lambda_k1[l], lambda_q2[l], lambda_k2[l]], axis=0),
        "g_subln": g_subln[l][None, :],
        "conv_w": conv_w[l],
        "conv_b": conv_b[l][None, :],
        "w_ext": _lru_gate_operand(w_rgate[l], w_igate[l], b_rgate[l], b_igate[l]),
        "lru_lam": lru_lambda[l],
        "w_out_att": w_out[l, :D_ATT].astype(BF16),
        "w_out_lru": w_out[l, D_ATT:].astype(BF16),
        "g_post": g_post[l][None, :],
    }

    y_p, new_k, new_v, st_p = _sublayer(
        x_prompt, rows(scale, bd, bd + 1), rows(shift, bd, bd + 1), rows(gate, bd, bd + 1),
        None, None, jnp.zeros((bp, 2, D_LRU), F32), False, w)

    y_s, _, _, _ = _sublayer(
        x_sample, rows(scale, 0, bd), rows(shift, 0, bd), rows(gate, 0, bd),
        cache_k.reshape(bd, kc * N_HEADS, HEAD_W), cache_v.reshape(bd, kc * N_HEADS, HEAD_W),
        state_lru.reshape(bd, 2, D_LRU), True, w)

    new_k = new_k.reshape(bp, 1, seq, N_HEADS, HEAD_W)
    new_v = new_v.reshape(bp, 1, seq, N_HEADS, HEAD_W)
    return (y_p, y_s, new_k, new_v, st_p[:, None])
```

```python
import functools
import math

import jax
import jax.numpy as jnp
import numpy as np
from jax import lax
from jax.experimental import pallas as pl
from jax.experimental.pallas import tpu as pltpu

F32 = jnp.float32
BF16 = jnp.bfloat16

D_MODEL = 1024
GRID_W = 64
N_HEADS = 8
DIFF_HEAD_DIM = 64
HEAD_W = 2 * DIFF_HEAD_DIM
D_ATT = N_HEADS * HEAD_W
D_LRU = 1024
N_LRU_BLOCKS = 8
LRU_BLOCK = D_LRU // N_LRU_BLOCKS
LRU_C = 8.0
N_GROUPS = 6
ROPE_BASE = 10000.0
EPS = 1e-6
LAM_INIT = 0.8 - 0.6 * math.exp(-0.3 * 0)
LOG2_E = math.log2(math.e)

V7X_VMEM_LIMIT_BYTES = 56 * 1024 * 1024
SUBLANES = 8
ROPE_SWAP = DIFF_HEAD_DIM // 4
BF16_ROWS_PER_VREG = 16
ACC_ROWS = HEAD_W + BF16_ROWS_PER_VREG
CACHE_CHUNK = 256
BIAS_SPLIT = 3


def _silu(x):
    return x * jax.nn.sigmoid(x)


def _dot(a, b):
    return jnp.dot(a, b, preferred_element_type=F32)


def _mod_kernel(cond_ref, w_ref, b_ref, o_ref):
    s = _silu(cond_ref[...])
    o_ref[...] = _dot(s.astype(BF16), w_ref[...].astype(BF16)) + b_ref[...]


def _modulation(cond, w_ada, b_ada):
    n = cond.shape[0]
    return pl.pallas_call(
        _mod_kernel,
        grid=(3,),
        in_specs=[
            pl.BlockSpec((n, D_MODEL), lambda j: (0, 0)),
            pl.BlockSpec((D_MODEL, D_MODEL), lambda j: (0, j)),
            pl.BlockSpec((1, D_MODEL), lambda j: (0, j)),
        ],
        out_specs=pl.BlockSpec((n, D_MODEL), lambda j: (0, j)),
        out_shape=jax.ShapeDtypeStruct((n, 3 * D_MODEL), F32),
        name="modulation",
    )(cond, w_ada, b_ada)


def _rope_tables(t_len):
    half = DIFF_HEAD_DIM // 2
    nf = half // 2
    t = np.arange(t_len)
    row = (t // GRID_W).astype(np.float32)
    col = (t % GRID_W).astype(np.float32)
    inv = (ROPE_BASE ** (-np.arange(nf, dtype=np.float32) * 2.0 / half)).astype(np.float32)
    lane = np.arange(HEAD_W) % DIFF_HEAD_DIM
    use_row = lane < half
    freq = (lane % half) % nf
    first = (lane % half) < nf
    pos = np.where(use_row[None, :], row[:, None], col[:, None]).astype(np.float32)
    ang = (pos * inv[freq][None, :]).astype(np.float32).astype(np.float64)
    cos = np.cos(ang).astype(np.float32)
    sin = (np.sin(ang) * np.where(first, -1.0, 1.0)[None, :]).astype(np.float32)
    return jnp.asarray(cos), jnp.asarray(sin)


def _inproj_kernel(*refs, use_rope, emit_kv_f32):
    x_ref, scale_ref, shift_ref, gpre_ref, w_ref = refs[:5]
    refs = refs[5:]
    if use_rope:
        cos_ref, sin_ref = refs[:2]
        refs = refs[2:]
    q_ref, k_ref, v_ref, gatt_ref, xlru_ref, glru_ref = refs[:6]
    refs = refs[6:]
    if emit_kv_f32:
        kf_ref, vf_ref = refs

    x = x_ref[...]
    ms = jnp.mean(x * x, axis=-1, keepdims=True)
    y = x * lax.rsqrt(ms + EPS) * gpre_ref[...]
    h = (y * (1.0 + scale_ref[...]) + shift_ref[...]).astype(BF16)

    def proj(g):
        return _dot(h, w_ref[:, g * D_MODEL:(g + 1) * D_MODEL])

    if use_rope:
        cos = cos_ref[...]
        sin = sin_ref[...]
        lane = lax.broadcasted_iota(jnp.int32, cos.shape, 1)
        take_next = (lane % (2 * ROPE_SWAP)) < ROPE_SWAP

        def rope(p):
            outs = []
            for hd in range(N_HEADS):
                xh = p[:, hd * HEAD_W:(hd + 1) * HEAD_W]
                partner = jnp.where(take_next,
                                    pltpu.roll(xh, HEAD_W - ROPE_SWAP, 1),
                                    pltpu.roll(xh, ROPE_SWAP, 1))
                outs.append(xh * cos + partner * sin)
            return jnp.concatenate(outs, axis=-1)
    else:
        def rope(p):
            return p

    q = rope(proj(0)) * (LOG2_E / math.sqrt(DIFF_HEAD_DIM))
    q_ref[...] = q.T.astype(BF16)
    k = rope(proj(1))
    k_ref[...] = k.astype(BF16)
    v = proj(2)
    vt = v.T
    for hd in range(N_HEADS):
        v_ref[hd] = vt[hd * HEAD_W:(hd + 1) * HEAD_W].astype(BF16)
    if emit_kv_f32:
        kf_ref[...] = k
        vf_ref[...] = v
    gatt_ref[...] = proj(3).T
    xlru_ref[...] = proj(4)
    glru_ref[...] = proj(5)


def _inproj(x, scale, shift, g_pre, w_in_bf16, *, t_len, use_rope, emit_kv_f32, tm=512):
    n_tok = x.shape[0]
    tm = min(tm, t_len)
    tiles_per_batch = t_len // tm
    nb = scale.shape[0]
    if nb == 1:
        mod_map = lambda i: (0, 0, 0)
    else:
        mod_map = lambda i: (i // tiles_per_batch, 0, 0)
    tok_spec = pl.BlockSpec((tm, D_MODEL), lambda i: (i, 0))
    in_specs = [
        tok_spec,
        pl.BlockSpec((None, 1, D_MODEL), mod_map),
        pl.BlockSpec((None, 1, D_MODEL), mod_map),
        pl.BlockSpec((1, D_MODEL), lambda i: (0, 0)),
        pl.BlockSpec((D_MODEL, N_GROUPS * D_MODEL), lambda i: (0, 0), pipeline_mode=pl.Buffered(1)),
    ]
    args = [x, scale, shift, g_pre, w_in_bf16]
    if use_rope:
        cos, sin = _rope_tables(t_len)
        rope_spec = pl.BlockSpec((tm, HEAD_W), lambda i: (i % tiles_per_batch, 0))
        in_specs += [rope_spec, rope_spec]
        args += [cos, sin]
    n_batch = n_tok // t_len
    kc = _key_chunk(t_len)
    tiles_per_chunk = kc // tm
    out_specs = [
        pl.BlockSpec((None, D_MODEL, tm), lambda i: (i // tiles_per_batch, 0, i % tiles_per_batch)),
        tok_spec,
        pl.BlockSpec((None, N_HEADS, None, HEAD_W, tm),
                     lambda i: (i // tiles_per_batch, 0, (i % tiles_per_batch) // tiles_per_chunk, 0,
                                i % tiles_per_chunk)),
        pl.BlockSpec((None, D_MODEL, tm), lambda i: (i // tiles_per_batch, 0, i % tiles_per_batch)),
        tok_spec, tok_spec,
    ]
    out_shape = [
        jax.ShapeDtypeStruct((n_batch, D_MODEL, t_len), BF16),
        jax.ShapeDtypeStruct((n_tok, D_MODEL), BF16),
        jax.ShapeDtypeStruct((n_batch, N_HEADS, t_len // kc, HEAD_W, kc), BF16),
        jax.ShapeDtypeStruct((n_batch, D_MODEL, t_len), F32),
    ] + [jax.ShapeDtypeStruct((n_tok, D_MODEL), F32)] * 2
    if emit_kv_f32:
        out_specs += [tok_spec] * 2
        out_shape += [jax.ShapeDtypeStruct((n_tok, D_MODEL), F32)] * 2
    return pl.pallas_call(
        functools.partial(_inproj_kernel, use_rope=use_rope, emit_kv_f32=emit_kv_f32),
        grid=(n_tok // tm,),
        in_specs=in_specs,
        out_specs=out_specs,
        out_shape=out_shape,
        compiler_params=pltpu.CompilerParams(
            dimension_semantics=("arbitrary",), vmem_limit_bytes=V7X_VMEM_LIMIT_BYTES),
        name="inproj_rope" if use_rope else "inproj",
    )(*args)


def _key_chunk(t_len):
    return min(512, t_len)


def _attn_kernel(*refs, has_cache, n_chunks, kc, heads_per_step):
    lam_ref, gsub_ref, qt_ref, k_ref, vt_ref, gatt_ref = refs[:6]
    refs = refs[6:]
    cache_refs = None
    if has_cache:
        cache_refs = refs[:2]
        refs = refs[2:]
    o_ref, acc_ref, s_ref, e_ref = refs

    lp = lam_ref[...]
    lam = (jnp.exp(jnp.sum(lp[0:1] * lp[1:2], axis=-1, keepdims=True))
           - jnp.exp(jnp.sum(lp[2:3] * lp[3:4], axis=-1, keepdims=True)) + LAM_INIT)
    g_subln = gsub_ref[...]

    if n_chunks == 1 and not has_cache:
        _attn_single_chunk(lam, g_subln, qt_ref, k_ref, vt_ref, gatt_ref, o_ref, s_ref, e_ref,
                           n_heads=heads_per_step)
        return
    for hh in range(heads_per_step):
        lanes = slice(hh * HEAD_W, (hh + 1) * HEAD_W)
        _attn_accumulate(qt_ref.at[lanes, :], k_ref.at[:, lanes], vt_ref.at[hh], cache_refs,
                         pl.program_id(1) * heads_per_step + hh, acc_ref, s_ref, e_ref,
                         n_chunks=n_chunks, kc=kc)
        _attn_finalize(lam, g_subln, acc_ref[0], acc_ref[1], gatt_ref.at[lanes, :], o_ref.at[lanes, :])


def _query_maps(qt):
    row = lax.broadcasted_iota(jnp.int32, qt.shape, 0)
    zero = jnp.zeros_like(qt)
    return jnp.where(row < DIFF_HEAD_DIM, qt, zero), jnp.where(row >= DIFF_HEAD_DIM, qt, zero)


def _with_ones(vt_chunk):
    ones = jnp.ones((ACC_ROWS - HEAD_W, vt_chunk.shape[1]), BF16)
    return jnp.concatenate([vt_chunk, ones], axis=0)


def _attn_single_chunk(lam, g_subln, qt_ref, k_ref, vt_ref, gatt_ref, o_ref, s_ref, e_ref, *, n_heads):
    maxes = {}
    for t in range(n_heads + 2):
        if t < n_heads:
            lanes = slice(t * HEAD_W, (t + 1) * HEAD_W)
            keys = k_ref[:, lanes]
            q_maps = _query_maps(qt_ref[lanes, :])
            head_max = []
            for idx in range(2):
                st = _dot(keys, q_maps[idx])
                s_ref[t % 2, idx] = st
                head_max.append(jnp.max(st, axis=0, keepdims=True))
            maxes[t] = head_max
        if 1 <= t <= n_heads:
            h = t - 1
            for idx in range(2):
                e_ref[h % 2, idx] = jnp.exp2(s_ref[h % 2, idx] - maxes[h][idx]).astype(BF16)
        if t >= 2:
            h = t - 2
            lanes = slice(h * HEAD_W, (h + 1) * HEAD_W)
            vt_ext = _with_ones(vt_ref[h, 0])
            acc1, acc2 = [_dot(vt_ext, e_ref[h % 2, idx]) for idx in range(2)]
            _attn_finalize(lam, g_subln, acc1, acc2, gatt_ref.at[lanes, :], o_ref.at[lanes, :])


def _attn_finalize(lam, g_subln, acc1, acc2, gatt_ref, o_ref):
    o1 = acc1[:HEAD_W] * (1.0 / acc1[HEAD_W:HEAD_W + 1])
    o2 = acc2[:HEAD_W] * (1.0 / acc2[HEAD_W:HEAD_W + 1])
    ot = o1 - lam * o2
    ms = jnp.mean(ot * ot, axis=0, keepdims=True)
    on = ot * lax.rsqrt(ms + EPS) * (g_subln * (1.0 - LAM_INIT))
    o_ref[...] = (on * _silu(gatt_ref[...])).astype(BF16)


def _attn_accumulate(qt_ref, k_ref, vt_ref, cache_refs, head, acc_ref, s_ref, e_ref, *, n_chunks, kc):
    has_cache = cache_refs is not None
    tq = qt_ref.shape[1]
    q_maps = _query_maps(qt_ref[...])

    acc_ref[...] = jnp.zeros(acc_ref.shape, F32)

    n_cache = 0
    if has_cache:
        kc_ref, vc_ref = cache_refs
        cache_len = kc_ref.shape[0] // N_HEADS
        cache_chunk = min(CACHE_CHUNK, cache_len)
        n_cache = cache_len // cache_chunk
    n_total = n_chunks + n_cache

    def is_cache(c):
        return isinstance(c, int) and c < n_cache

    def cache_rows(c):
        return pl.ds(head + c * cache_chunk * N_HEADS, cache_chunk, stride=N_HEADS)

    def keys_of(c):
        if is_cache(c):
            return kc_ref[cache_rows(c), :].astype(BF16)
        start = (c - n_cache) * kc
        return k_ref[pl.ds(start if isinstance(c, int) else pl.multiple_of(start, kc), kc), :]

    def values_of(c):
        if is_cache(c):
            return vc_ref[cache_rows(c), :].T.astype(BF16)
        return vt_ref[c - n_cache]

    def scores(c, slot):
        k_chunk = keys_of(c)
        maxes = []
        for idx in range(2):
            st = _dot(k_chunk, q_maps[idx])
            s_ref[slot, idx, :k_chunk.shape[0], :] = st
            maxes.append(jnp.max(st, axis=0, keepdims=True))
        return tuple(maxes)

    def probs(n_keys, slot, ms, maxes):
        new_ms, alphas = [], []
        for idx in range(2):
            m_new = jnp.maximum(ms[idx], maxes[idx])
            alphas.append(jnp.exp2(ms[idx] - m_new))
            e_ref[slot, idx, :n_keys, :] = jnp.exp2(s_ref[slot, idx, :n_keys, :] - m_new).astype(BF16)
            new_ms.append(m_new)
        return tuple(new_ms), tuple(alphas)

    def values(c, slot, alphas):
        vt_ext = _with_ones(values_of(c))
        n_keys = vt_ext.shape[1]
        for idx in range(2):
            acc_ref[idx] = acc_ref[idx] * alphas[idx] + _dot(vt_ext, e_ref[slot, idx, :n_keys, :])

    def n_keys_of(c):
        return cache_chunk if is_cache(c) else kc

    def step(i, slot, ms, maxes_next, alphas_cur):
        maxes_after = scores(i + 2, slot)
        ms, alphas_next = probs(n_keys_of(i + 1), 1 - slot, ms, maxes_next)
        values(i, slot, alphas_cur)
        return ms, maxes_after, alphas_next

    m0 = jnp.full((1, tq), -jnp.inf, F32)
    ms = (m0, m0)
    maxes_cur = scores(0, 0)
    if n_total == 1:
        ms, alphas_cur = probs(n_keys_of(0), 0, ms, maxes_cur)
        values(0, 0, alphas_cur)
    else:
        maxes_next = scores(1, 1)
        ms, alphas_cur = probs(n_keys_of(0), 0, ms, maxes_cur)
        n_steps = n_total - 2
        first_loop = n_cache
        for i in range(min(first_loop, n_steps)):
            ms, maxes_next, alphas_cur = step(i, i % 2, ms, maxes_next, alphas_cur)
        n_pairs = max(n_steps - first_loop, 0) // 2

        def pair(t, carry):
            ms, maxes_next, alphas_cur = carry
            i = first_loop + 2 * t
            ms, maxes_next, alphas_cur = step(i, first_loop % 2, ms, maxes_next, alphas_cur)
            return step(i + 1, (first_loop + 1) % 2, ms, maxes_next, alphas_cur)

        ms, maxes_next, alphas_cur = lax.fori_loop(0, n_pairs, pair, (ms, maxes_next, alphas_cur))
        for i in range(first_loop + 2 * n_pairs, n_steps):
            ms, maxes_next, alphas_cur = step(i, i % 2, ms, maxes_next, alphas_cur)
        last = n_total - 1
        ms, alphas_last = probs(n_keys_of(last), last % 2, ms, maxes_next)
        values(last - 1, (last - 1) % 2, alphas_cur)
        values(last, last % 2, alphas_last)


def _attention(lam_params, g_subln, qt, k, vt, g_att_t, cache_k=None, cache_v=None, *, tq=2048):
    b, t_len, _ = k.shape
    has_cache = cache_k is not None
    n_chunks, kc = vt.shape[2], vt.shape[4]
    tq = min(tq, t_len)
    heads_per_step = N_HEADS if n_chunks == 1 and not has_cache else 1
    width = heads_per_step * HEAD_W
    s_rows = max(kc, cache_k.shape[1] // N_HEADS) if has_cache else kc
    tok_spec = pl.BlockSpec((None, width, tq), lambda bi, h, qi: (bi, h, qi))
    in_specs = [
        pl.BlockSpec(lam_params.shape, lambda bi, h, qi: (0, 0)),
        pl.BlockSpec((HEAD_W, 1), lambda bi, h, qi: (0, 0)),
        tok_spec,
        pl.BlockSpec((None, t_len, width), lambda bi, h, qi: (bi, 0, h)),
        pl.BlockSpec((None, heads_per_step, n_chunks, HEAD_W, kc), lambda bi, h, qi: (bi, h, 0, 0, 0)),
        tok_spec,
    ]
    args = [lam_params, g_subln, qt, k, vt, g_att_t]
    if has_cache:
        c_spec = pl.BlockSpec((None,) + cache_k.shape[1:], lambda bi, h, qi: (bi, 0, 0))
        in_specs += [c_spec, c_spec]
        args += [cache_k, cache_v]
    return pl.pallas_call(
        functools.partial(_attn_kernel, has_cache=has_cache, n_chunks=n_chunks, kc=kc,
                          heads_per_step=heads_per_step),
        grid=(b, N_HEADS // heads_per_step, t_len // tq),
        in_specs=in_specs,
        out_specs=tok_spec,
        out_shape=jax.ShapeDtypeStruct((b, D_ATT, t_len), BF16),
        scratch_shapes=[pltpu.VMEM((2, ACC_ROWS, tq), F32),
                        pltpu.VMEM((2, 2, s_rows, tq), F32),
                        pltpu.VMEM((2, 2, s_rows, tq), BF16)],
        compiler_params=pltpu.CompilerParams(
            dimension_semantics=("arbitrary",) * 3, vmem_limit_bytes=V7X_VMEM_LIMIT_BYTES),
        name="diff_attn_cache" if has_cache else "diff_attn",
    )(*args)


def _lru_kernel(x_ref, g_ref, cw_ref, cb_ref, w_ref, lam_ref, h0_ref,
                o_ref, st_ref,
                xp, a_f, b_f, a_b, b_b, hin_f, hin_b,
                *, blocks_per_step, t_len, chunk, pitch, rows):
    blocks = range(blocks_per_step)
    lanes = [slice(j * LRU_BLOCK, (j + 1) * LRU_BLOCK) for j in blocks]
    g_f, p_f, g_b, p_b = b_f, a_f, b_b, a_b
    n_chunks = t_len // chunk
    chunks_per_tile = rows // chunk

    pad = jnp.zeros((SUBLANES, LRU_BLOCK), F32)
    for j in blocks:
        xp[j, pl.ds(0, SUBLANES), :] = pad
        xp[j, pl.ds(SUBLANES, t_len), :] = x_ref[:, lanes[j]]
        xp[j, pl.ds(SUBLANES + t_len, SUBLANES), :] = pad

    lam = lam_ref[...]
    sp = jnp.maximum(-lam, 0.0) + jnp.log1p(jnp.exp(-jnp.abs(lam)))
    half_scale = (-0.5 * LRU_C) * sp
    cw = cw_ref[...]
    cb = cb_ref[...]
    lane = lax.broadcasted_iota(jnp.int32, (rows, LRU_BLOCK), 1)
    bias_taps = jnp.where(lane < BIAS_SPLIT, 1.0, 0.0).astype(BF16)

    def gate_tile(r, carry):
        t0 = pl.multiple_of(r * rows, SUBLANES)
        for j in blocks:
            u = cb[:, lanes[j]]
            for tap in range(4):
                u = u + xp[j, pl.ds(t0 + SUBLANES - 1 + tap, rows), :] * cw[tap:tap + 1, lanes[j]]
            zh = _dot(jnp.concatenate([u.astype(BF16), bias_taps], axis=1), w_ref[j])
            u_half = 0.5 * u
            for d, (a_s, b_s) in enumerate(((a_f, b_f), (a_b, b_b))):
                scale = half_scale[d:d + 1, lanes[j]]
                tr = jnp.tanh(zh[:, (2 * d) * LRU_BLOCK:(2 * d + 1) * LRU_BLOCK])
                ti = jnp.tanh(zh[:, (2 * d + 1) * LRU_BLOCK:(2 * d + 2) * LRU_BLOCK])
                log_a = scale + scale * tr
                a = jnp.exp(log_a)
                gain_sq = jnp.tanh(log_a) * (-1.0 - a * a)
                gain = jnp.where(gain_sq > 0.0, gain_sq * lax.rsqrt(gain_sq), 0.0)
                bb = gain * (u_half + u_half * ti)
                for cc in range(chunks_per_tile):
                    dst = pl.multiple_of(r * (chunks_per_tile * pitch), SUBLANES) + cc * pitch
                    a_s[j, pl.ds(dst, chunk), :] = a[cc * chunk:(cc + 1) * chunk]
                    b_s[j, pl.ds(dst, chunk), :] = bb[cc * chunk:(cc + 1) * chunk]
        return carry

    lax.fori_loop(0, t_len // rows, gate_tile, 0)

    def local_step(l, carry):
        out = []
        for j in blocks:
            hf, pf, hb, pb = carry[4 * j:4 * j + 4]
            rf = pl.ds(l, n_chunks, stride=pitch)
            a = a_f[j, rf, :]
            hf = a * hf + b_f[j, rf, :]
            pf = a * pf
            g_f[j, rf, :] = hf
            p_f[j, rf, :] = pf
            rb = pl.ds(chunk - 1 - l, n_chunks, stride=pitch)
            a = a_b[j, rb, :]
            hb = a * hb + b_b[j, rb, :]
            pb = a * pb
            g_b[j, rb, :] = hb
            p_b[j, rb, :] = pb
            out += [hf, pf, hb, pb]
        return tuple(out)

    z0 = jnp.zeros((n_chunks, LRU_BLOCK), F32)
    o0 = jnp.ones((n_chunks, LRU_BLOCK), F32)
    chunk_maps = lax.fori_loop(0, chunk, local_step, (z0, o0, z0, o0) * blocks_per_step, unroll=4)

    h0 = h0_ref[...]
    chunk_id = lax.broadcasted_iota(jnp.int32, (n_chunks, LRU_BLOCK), 0)

    def compose(end, decay, towards_higher):
        shift = 1
        while shift < n_chunks:
            if towards_higher:
                valid = chunk_id >= shift
                amount = shift
            else:
                valid = chunk_id < n_chunks - shift
                amount = n_chunks - shift
            prev_end = jnp.where(valid, pltpu.roll(end, amount, 0), 0.0)
            prev_decay = jnp.where(valid, pltpu.roll(decay, amount, 0), 1.0)
            end = end + decay * prev_end
            decay = decay * prev_decay
            shift *= 2
        return end, decay

    for j in blocks:
        end_f, decay_f, end_b, decay_b = chunk_maps[4 * j:4 * j + 4]
        h0_f = h0[0:1, lanes[j]]
        h0_b = h0[1:2, lanes[j]]
        end_f, decay_f = compose(end_f, decay_f, True)
        after_f = end_f + decay_f * h0_f
        hin_f[j] = jnp.where(chunk_id == 0, h0_f, pltpu.roll(after_f, 1, 0))
        st_ref[0:1, lanes[j]] = after_f[n_chunks - 1:n_chunks]
        end_b, decay_b = compose(end_b, decay_b, False)
        after_b = end_b + decay_b * h0_b
        hin_b[j] = jnp.where(chunk_id == n_chunks - 1, h0_b, pltpu.roll(after_b, n_chunks - 1, 0))
        st_ref[1:2, lanes[j]] = after_b[0:1]

    def out_pair(cp, carry):
        for par in range(2):
            c = 2 * cp + par
            src = pl.ds(pl.multiple_of(cp * (2 * pitch), SUBLANES) + par * pitch, chunk)
            dst = pl.ds(pl.multiple_of(c * chunk, SUBLANES), chunk)
            for j in blocks:
                hf = g_f[j, src, :] + p_f[j, src, :] * hin_f[j, pl.ds(c, 1), :]
                hb = g_b[j, src, :] + p_b[j, src, :] * hin_b[j, pl.ds(c, 1), :]
                o_ref[dst, lanes[j]] = ((hf + hb) * _silu(g_ref[dst, lanes[j]])).astype(BF16)
        return carry

    lax.fori_loop(0, n_chunks // 2, out_pair, 0, unroll=4)


def _lru_gate_operand(w_r, w_i, b_r, b_i):
    weights = 0.5 * jnp.concatenate([w_r[0], w_i[0], w_r[1], w_i[1]], axis=-1)
    bias = 0.5 * jnp.concatenate(
        [bb.reshape(N_LRU_BLOCKS, 1, LRU_BLOCK) for bb in (b_r[0], b_i[0], b_r[1], b_i[1])], axis=-1)
    parts, rest = [], bias
    for _ in range(BIAS_SPLIT):
        gamma = rest * (2.0 ** 16 + 1.0)
        part = gamma - (gamma - rest)
        parts.append(part)
        rest = rest - part
    bias_rows = jnp.pad(jnp.concatenate(parts, axis=1), ((0, 0), (0, LRU_BLOCK - BIAS_SPLIT), (0, 0)))
    return jnp.concatenate([weights, bias_rows], axis=1).astype(BF16)


def _lru(x_lru, g_lru, conv_w, conv_b, w_ext, lru_lam, h0):
    b, t_len, _ = x_lru.shape
    chunk = 64 if t_len >= 2048 else 32
    pitch = chunk + SUBLANES // 2
    n_chunks = t_len // chunk
    rows = min(4096, t_len)
    blocks_per_step = N_LRU_BLOCKS if t_len <= 512 else 1
    width = blocks_per_step * LRU_BLOCK
    seq_spec = pl.BlockSpec((None, t_len, width), lambda bi, n: (bi, 0, n))
    st_spec = pl.BlockSpec((None, 2, width), lambda bi, n: (bi, 0, n))
    scan_buf = pltpu.VMEM((blocks_per_step, n_chunks * pitch, LRU_BLOCK), F32)
    return pl.pallas_call(
        functools.partial(_lru_kernel, blocks_per_step=blocks_per_step,
                          t_len=t_len, chunk=chunk, pitch=pitch, rows=rows),
        grid=(b, N_LRU_BLOCKS // blocks_per_step),
        in_specs=[
            seq_spec, seq_spec,
            pl.BlockSpec((4, width), lambda bi, n: (0, n)),
            pl.BlockSpec((1, width), lambda bi, n: (0, n)),
            pl.BlockSpec((blocks_per_step, 2 * LRU_BLOCK, 4 * LRU_BLOCK), lambda bi, n: (n, 0, 0)),
            pl.BlockSpec((2, width), lambda bi, n: (0, n)),
            st_spec,
        ],
        out_specs=[seq_spec, st_spec],
        out_shape=[jax.ShapeDtypeStruct((b, t_len, D_LRU), BF16),
                   jax.ShapeDtypeStruct((b, 2, D_LRU), F32)],
        scratch_shapes=[
            pltpu.VMEM((blocks_per_step, t_len + 2 * SUBLANES, LRU_BLOCK), F32),
            scan_buf, scan_buf, scan_buf, scan_buf,
            pltpu.VMEM((blocks_per_step, n_chunks, LRU_BLOCK), F32),
            pltpu.VMEM((blocks_per_step, n_chunks, LRU_BLOCK), F32),
        ],
        compiler_params=pltpu.CompilerParams(
            dimension_semantics=("arbitrary",) * 2, vmem_limit_bytes=V7X_VMEM_LIMIT_BYTES),
        name="rglru",
    )(x_lru, g_lru, conv_w, conv_b, w_ext, lru_lam, h0)


def _outproj_kernel(att_ref, lru_ref, x_ref, gate_ref, wa_ref, wl_ref, gpost_ref, y_ref):
    o = lax.dot_general(att_ref[...], wa_ref[...], (((0,), (0,)), ((), ())), preferred_element_type=F32)
    o = o + _dot(lru_ref[...], wl_ref[...])
    ms = jnp.mean(o * o, axis=-1, keepdims=True)
    n = o * lax.rsqrt(ms + EPS) * gpost_ref[...]
    y_ref[...] = x_ref[...] + gate_ref[...] * n


def _outproj(att_t, lru, x, gate, w_att, w_lru, g_post, *, t_len, tm=1024):
    n_tok = x.shape[0]
    nb = gate.shape[0]
    tm = min(tm, t_len)
    tiles_per_batch = t_len // tm
    if nb == 1:
        mod_map = lambda i: (0, 0, 0)
    else:
        mod_map = lambda i: (i // tiles_per_batch, 0, 0)
    tok_spec = pl.BlockSpec((tm, D_MODEL), lambda i: (i, 0))
    att_spec = pl.BlockSpec((None, D_ATT, tm), lambda i: (i // tiles_per_batch, 0, i % tiles_per_batch))
    w_spec = pl.BlockSpec((D_MODEL, D_MODEL), lambda i: (0, 0))
    return pl.pallas_call(
        _outproj_kernel,
        grid=(n_tok // tm,),
        in_specs=[att_spec, tok_spec, tok_spec,
                  pl.BlockSpec((None, 1, D_MODEL), mod_map),
                  w_spec, w_spec,
                  pl.BlockSpec((1, D_MODEL), lambda i: (0, 0))],
        out_specs=tok_spec,
        out_shape=jax.ShapeDtypeStruct((n_tok, D_MODEL), F32),
        compiler_params=pltpu.CompilerParams(
            dimension_semantics=("arbitrary",), vmem_limit_bytes=V7X_VMEM_LIMIT_BYTES),
        name="outproj",
    )(att_t, lru, x, gate, w_att, w_lru, g_post)


def _sublayer(x, scale, shift, gate, cache_k, cache_v, h0, use_rope, w):
    b, t_len, _ = x.shape
    xf = x.reshape(b * t_len, D_MODEL)
    emit_kv = cache_k is None
    outs = _inproj(xf, scale, shift, w["g_pre"], w["w_in"], t_len=t_len,
                   use_rope=use_rope, emit_kv_f32=emit_kv)
    qt, vt = outs[0], outs[2]
    g_att_t = outs[3]
    k, x_lru, g_lru = [outs[i].reshape(b, t_len, D_MODEL) for i in (1, 4, 5)]
    att_t = _attention(w["lam"], w["g_subln"], qt, k, vt, g_att_t, cache_k, cache_v)
    lru, state = _lru(x_lru, g_lru, w["conv_w"], w["conv_b"], w["w_ext"], w["lru_lam"], h0)
    y = _outproj(att_t, lru.reshape(b * t_len, D_LRU), xf, gate,
                 w["w_out_att"], w["w_out_lru"], w["g_post"], t_len=t_len)
    y = y.reshape(b, t_len, D_MODEL)
    if emit_kv:
        return y, outs[6], outs[7], state
    return y, None, None, state


def kernel(x_prompt, x_sample, cache_k, cache_v, state_lru, c, c_ctx, w_ada, b_ada, g_pre, w_in, lambda_q1, lambda_k1, lambda_q2, lambda_k2, g_subln, conv_w, conv_b, w_rgate, b_rgate, w_igate, b_igate, lru_lambda, w_out, g_post):
    bp, seq, _ = x_prompt.shape
    bd, kc = cache_k.shape[0], cache_k.shape[2]
    assert cache_k.shape[1] == 1 and w_in.shape[0] == 1, "single-layer step only"
    l = 0

    cond = jnp.concatenate([c, c_ctx[None, :], jnp.zeros((SUBLANES - bd - 1, D_MODEL), F32)], axis=0)
    mod = _modulation(cond, w_ada[l], b_ada[l][None, :])
    shift, scale, gate = [mod[:, i * D_MODEL:(i + 1) * D_MODEL] for i in range(3)]

    def rows(a, lo, hi):
        return a[lo:hi][:, None, :]

    w = {
        "g_pre": g_pre[l][None, :],
        "w_in": w_in[l].astype(BF16),
        "lam": jnp.stack([lambda_q1[l], lambda_k1[l], lambda_q2[l], lambda_k2[l]], axis=0),
        "g_subln": g_subln[l][:, None],
        "conv_w": conv_w[l],
        "conv_b": conv_b[l][None, :],
        "w_ext": _lru_gate_operand(w_rgate[l], w_igate[l], b_rgate[l], b_igate[l]),
        "lru_lam": lru_lambda[l],
        "w_out_att": w_out[l, :D_ATT].astype(BF16),
        "w_out_lru": w_out[l, D_ATT:].astype(BF16),
        "g_post": g_post[l][None, :],
    }

    y_p, new_k, new_v, st_p = _sublayer(
        x_prompt, rows(scale, bd, bd + 1), rows(shift, bd, bd + 1), rows(gate, bd, bd + 1),
        None, None, jnp.zeros((bp, 2, D_LRU), F32), False, w)

    y_s, _, _, _ = _sublayer(
        x_sample, rows(scale, 0, bd), rows(shift, 0, bd), rows(gate, 0, bd),
        cache_k.reshape(bd, kc * N_HEADS, HEAD_W), cache_v.reshape(bd, kc * N_HEADS, HEAD_W),
        state_lru.reshape(bd, 2, D_LRU), True, w)

    new_k = new_k.reshape(bp, 1, seq, N_HEADS, HEAD_W)
    new_v = new_v.reshape(bp, 1, seq, N_HEADS, HEAD_W)
    return (y_p, y_s, new_k, new_v, st_p[:, None])
```

```python
import functools
import math

import jax
import jax.numpy as jnp
import numpy as np
from jax import lax
from jax.experimental import pallas as pl
from jax.experimental.pallas import tpu as pltpu

F32 = jnp.float32
BF16 = jnp.bfloat16

D_MODEL = 1024
GRID_W = 64
N_HEADS = 8
DIFF_HEAD_DIM = 64
HEAD_W = 2 * DIFF_HEAD_DIM
D_ATT = N_HEADS * HEAD_W
D_LRU = 1024
N_LRU_BLOCKS = 8
LRU_BLOCK = D_LRU // N_LRU_BLOCKS
LRU_C = 8.0
N_GROUPS = 6
ROPE_BASE = 10000.0
EPS = 1e-6
LAM_INIT = 0.8 - 0.6 * math.exp(-0.3 * 0)
LOG2_E = math.log2(math.e)

V7X_VMEM_LIMIT_BYTES = 56 * 1024 * 1024
SUBLANES = 8
ROPE_SWAP = DIFF_HEAD_DIM // 4
BF16_ROWS_PER_VREG = 16
ACC_ROWS = HEAD_W + BF16_ROWS_PER_VREG
CACHE_CHUNK = 256
BIAS_SPLIT = 3


def _silu(x):
    return x * jax.nn.sigmoid(x)


def _dot(a, b):
    return jnp.dot(a, b, preferred_element_type=F32)


def _mod_kernel(cond_ref, w_ref, b_ref, win_ref, wout_ref, o_ref, win_bf_ref, wout_bf_ref):
    s = _silu(cond_ref[...])
    o_ref[...] = _dot(s.astype(BF16), w_ref[...].astype(BF16)) + b_ref[...]
    win_bf_ref[...] = win_ref[...].astype(BF16)
    wout_bf_ref[...] = wout_ref[...].astype(BF16)


def _modulation(cond, w_ada, b_ada, w_in, w_out, *, steps=8):
    n = cond.shape[0]
    d_in, d_mix = w_in.shape[1], w_out.shape[0]
    mod_w, in_w, out_r = 3 * D_MODEL // steps, d_in // steps, d_mix // steps
    return pl.pallas_call(
        _mod_kernel,
        grid=(steps,),
        in_specs=[
            pl.BlockSpec((n, D_MODEL), lambda j: (0, 0)),
            pl.BlockSpec((D_MODEL, mod_w), lambda j: (0, j)),
            pl.BlockSpec((1, mod_w), lambda j: (0, j)),
            pl.BlockSpec((D_MODEL, in_w), lambda j: (0, j)),
            pl.BlockSpec((out_r, D_MODEL), lambda j: (j, 0)),
        ],
        out_specs=[
            pl.BlockSpec((n, mod_w), lambda j: (0, j)),
            pl.BlockSpec((D_MODEL, in_w), lambda j: (0, j)),
            pl.BlockSpec((out_r, D_MODEL), lambda j: (j, 0)),
        ],
        out_shape=[
            jax.ShapeDtypeStruct((n, 3 * D_MODEL), F32),
            jax.ShapeDtypeStruct(w_in.shape, BF16),
            jax.ShapeDtypeStruct(w_out.shape, BF16),
        ],
        name="modulation",
    )(cond, w_ada, b_ada, w_in, w_out)


def _rope_tables(t_len):
    half = DIFF_HEAD_DIM // 2
    nf = half // 2
    t = np.arange(t_len)
    row = (t // GRID_W).astype(np.float32)
    col = (t % GRID_W).astype(np.float32)
    inv = (ROPE_BASE ** (-np.arange(nf, dtype=np.float32) * 2.0 / half)).astype(np.float32)
    lane = np.arange(HEAD_W) % DIFF_HEAD_DIM
    use_row = lane < half
    freq = (lane % half) % nf
    first = (lane % half) < nf
    pos = np.where(use_row[None, :], row[:, None], col[:, None]).astype(np.float32)
    ang = (pos * inv[freq][None, :]).astype(np.float32).astype(np.float64)
    cos = np.cos(ang).astype(np.float32)
    sin = (np.sin(ang) * np.where(first, -1.0, 1.0)[None, :]).astype(np.float32)
    return jnp.asarray(cos), jnp.asarray(sin)


def _inproj_kernel(*refs, use_rope, emit_kv_f32):
    x_ref, scale_ref, shift_ref, gpre_ref, w_ref = refs[:5]
    refs = refs[5:]
    if use_rope:
        cos_ref, sin_ref = refs[:2]
        refs = refs[2:]
    q_ref, k_ref, v_ref, gatt_ref, xlru_ref, glru_ref = refs[:6]
    refs = refs[6:]
    if emit_kv_f32:
        kf_ref, vf_ref = refs

    x = x_ref[...]
    ms = jnp.mean(x * x, axis=-1, keepdims=True)
    y = x * lax.rsqrt(ms + EPS) * gpre_ref[...]
    h = (y * (1.0 + scale_ref[...]) + shift_ref[...]).astype(BF16)

    def proj(g):
        return _dot(h, w_ref[:, g * D_MODEL:(g + 1) * D_MODEL])

    if use_rope:
        cos = cos_ref[...]
        sin = sin_ref[...]
        lane = lax.broadcasted_iota(jnp.int32, cos.shape, 1)
        take_next = (lane % (2 * ROPE_SWAP)) < ROPE_SWAP

        def rope(p):
            outs = []
            for hd in range(N_HEADS):
                xh = p[:, hd * HEAD_W:(hd + 1) * HEAD_W]
                partner = jnp.where(take_next,
                                    pltpu.roll(xh, HEAD_W - ROPE_SWAP, 1),
                                    pltpu.roll(xh, ROPE_SWAP, 1))
                outs.append(xh * cos + partner * sin)
            return jnp.concatenate(outs, axis=-1)
    else:
        def rope(p):
            return p

    q = rope(proj(0)) * (LOG2_E / math.sqrt(DIFF_HEAD_DIM))
    q_ref[...] = q.T.astype(BF16)
    k = rope(proj(1))
    k_ref[...] = k.astype(BF16)
    v = proj(2)
    vt = v.T
    for hd in range(N_HEADS):
        v_ref[hd] = vt[hd * HEAD_W:(hd + 1) * HEAD_W].astype(BF16)
    if emit_kv_f32:
        kf_ref[...] = k
        vf_ref[...] = v
    gatt_ref[...] = proj(3)
    xlru_ref[...] = proj(4)
    glru_ref[...] = proj(5)


def _inproj(x, scale, shift, g_pre, w_in_bf16, *, t_len, use_rope, emit_kv_f32, tm=512):
    n_tok = x.shape[0]
    tm = min(tm, t_len)
    tiles_per_batch = t_len // tm
    nb = scale.shape[0]
    if nb == 1:
        mod_map = lambda i: (0, 0, 0)
    else:
        mod_map = lambda i: (i // tiles_per_batch, 0, 0)
    tok_spec = pl.BlockSpec((tm, D_MODEL), lambda i: (i, 0))
    in_specs = [
        tok_spec,
        pl.BlockSpec((None, 1, D_MODEL), mod_map),
        pl.BlockSpec((None, 1, D_MODEL), mod_map),
        pl.BlockSpec((1, D_MODEL), lambda i: (0, 0)),
        pl.BlockSpec((D_MODEL, N_GROUPS * D_MODEL), lambda i: (0, 0), pipeline_mode=pl.Buffered(1)),
    ]
    args = [x, scale, shift, g_pre, w_in_bf16]
    if use_rope:
        cos, sin = _rope_tables(t_len)
        rope_spec = pl.BlockSpec((tm, HEAD_W), lambda i: (i % tiles_per_batch, 0))
        in_specs += [rope_spec, rope_spec]
        args += [cos, sin]
    n_batch = n_tok // t_len
    kc = _key_chunk(t_len)
    tiles_per_chunk = kc // tm
    out_specs = [
        pl.BlockSpec((None, D_MODEL, tm), lambda i: (i // tiles_per_batch, 0, i % tiles_per_batch)),
        tok_spec,
        pl.BlockSpec((None, N_HEADS, None, HEAD_W, tm),
                     lambda i: (i // tiles_per_batch, 0, (i % tiles_per_batch) // tiles_per_chunk, 0,
                                i % tiles_per_chunk)),
        tok_spec, tok_spec, tok_spec,
    ]
    out_shape = [
        jax.ShapeDtypeStruct((n_batch, D_MODEL, t_len), BF16),
        jax.ShapeDtypeStruct((n_tok, D_MODEL), BF16),
        jax.ShapeDtypeStruct((n_batch, N_HEADS, t_len // kc, HEAD_W, kc), BF16),
    ] + [jax.ShapeDtypeStruct((n_tok, D_MODEL), F32)] * 3
    if emit_kv_f32:
        out_specs += [tok_spec] * 2
        out_shape += [jax.ShapeDtypeStruct((n_tok, D_MODEL), F32)] * 2
    return pl.pallas_call(
        functools.partial(_inproj_kernel, use_rope=use_rope, emit_kv_f32=emit_kv_f32),
        grid=(n_tok // tm,),
        in_specs=in_specs,
        out_specs=out_specs,
        out_shape=out_shape,
        compiler_params=pltpu.CompilerParams(
            dimension_semantics=("arbitrary",), vmem_limit_bytes=V7X_VMEM_LIMIT_BYTES),
        name="inproj_rope" if use_rope else "inproj",
    )(*args)


def _key_chunk(t_len):
    return min(512, t_len)


def _attn_kernel(*refs, has_cache, n_chunks, kc, heads_per_step):
    lam_ref, gsub_ref, qt_ref, k_ref, vt_ref, gatt_ref = refs[:6]
    refs = refs[6:]
    cache_refs = None
    if has_cache:
        cache_refs = refs[:2]
        refs = refs[2:]
    o_ref, acc_ref, s_ref, e_ref = refs

    lp = lam_ref[...]
    lam = (jnp.exp(jnp.sum(lp[0:1] * lp[1:2], axis=-1, keepdims=True))
           - jnp.exp(jnp.sum(lp[2:3] * lp[3:4], axis=-1, keepdims=True)) + LAM_INIT)
    g_subln = gsub_ref[...]

    if n_chunks == 1 and not has_cache:
        _attn_single_chunk(lam, g_subln, qt_ref, k_ref, vt_ref, gatt_ref, o_ref, s_ref, e_ref,
                           n_heads=heads_per_step)
        return
    for hh in range(heads_per_step):
        lanes = slice(hh * HEAD_W, (hh + 1) * HEAD_W)
        _attn_accumulate(qt_ref.at[lanes, :], k_ref.at[:, lanes], vt_ref.at[hh], cache_refs,
                         pl.program_id(1) * heads_per_step + hh, acc_ref, s_ref, e_ref,
                         n_chunks=n_chunks, kc=kc)
        _attn_finalize(lam, g_subln, acc_ref[0], acc_ref[1], gatt_ref.at[:, lanes], o_ref.at[:, lanes])


def _query_maps(qt):
    row = lax.broadcasted_iota(jnp.int32, qt.shape, 0)
    zero = jnp.zeros_like(qt)
    return jnp.where(row < DIFF_HEAD_DIM, qt, zero), jnp.where(row >= DIFF_HEAD_DIM, qt, zero)


def _with_ones(vt_chunk):
    ones = jnp.ones((ACC_ROWS - HEAD_W, vt_chunk.shape[1]), BF16)
    return jnp.concatenate([vt_chunk, ones], axis=0)


def _attn_single_chunk(lam, g_subln, qt_ref, k_ref, vt_ref, gatt_ref, o_ref, s_ref, e_ref, *, n_heads):
    maxes = {}
    for t in range(n_heads + 2):
        if t < n_heads:
            lanes = slice(t * HEAD_W, (t + 1) * HEAD_W)
            keys = k_ref[:, lanes]
            q_maps = _query_maps(qt_ref[lanes, :])
            head_max = []
            for idx in range(2):
                st = _dot(keys, q_maps[idx])
                s_ref[t % 2, idx] = st
                head_max.append(jnp.max(st, axis=0, keepdims=True))
            maxes[t] = head_max
        if 1 <= t <= n_heads:
            h = t - 1
            for idx in range(2):
                e_ref[h % 2, idx] = jnp.exp2(s_ref[h % 2, idx] - maxes[h][idx]).astype(BF16)
        if t >= 2:
            h = t - 2
            lanes = slice(h * HEAD_W, (h + 1) * HEAD_W)
            vt_ext = _with_ones(vt_ref[h, 0])
            acc1, acc2 = [_dot(vt_ext, e_ref[h % 2, idx]) for idx in range(2)]
            _attn_finalize(lam, g_subln, acc1, acc2, gatt_ref.at[:, lanes], o_ref.at[:, lanes])


def _attn_finalize(lam, g_subln, acc1, acc2, gatt_ref, o_ref):
    o1 = acc1[:HEAD_W] * (1.0 / acc1[HEAD_W:HEAD_W + 1])
    o2 = acc2[:HEAD_W] * (1.0 / acc2[HEAD_W:HEAD_W + 1])
    ot = o1 - lam * o2
    ms = jnp.mean(ot * ot, axis=0, keepdims=True)
    on = (ot * lax.rsqrt(ms + EPS)).T * (g_subln * (1.0 - LAM_INIT))
    o_ref[...] = (on * _silu(gatt_ref[...])).astype(BF16)


def _attn_accumulate(qt_ref, k_ref, vt_ref, cache_refs, head, acc_ref, s_ref, e_ref, *, n_chunks, kc):
    has_cache = cache_refs is not None
    tq = qt_ref.shape[1]
    q_maps = _query_maps(qt_ref[...])

    acc_ref[...] = jnp.zeros(acc_ref.shape, F32)

    n_cache = 0
    if has_cache:
        kc_ref, vc_ref = cache_refs
        cache_len = kc_ref.shape[0] // N_HEADS
        cache_chunk = min(CACHE_CHUNK, cache_len)
        n_cache = cache_len // cache_chunk
    n_total = n_chunks + n_cache

    def is_cache(c):
        return isinstance(c, int) and c < n_cache

    def cache_rows(c):
        return pl.ds(head + c * cache_chunk * N_HEADS, cache_chunk, stride=N_HEADS)

    def keys_of(c):
        if is_cache(c):
            return kc_ref[cache_rows(c), :].astype(BF16)
        start = (c - n_cache) * kc
        return k_ref[pl.ds(start if isinstance(c, int) else pl.multiple_of(start, kc), kc), :]

    def values_of(c):
        if is_cache(c):
            return vc_ref[cache_rows(c), :].T.astype(BF16)
        return vt_ref[c - n_cache]

    def scores(c, slot):
        k_chunk = keys_of(c)
        maxes = []
        for idx in range(2):
            st = _dot(k_chunk, q_maps[idx])
            s_ref[slot, idx, :k_chunk.shape[0], :] = st
            maxes.append(jnp.max(st, axis=0, keepdims=True))
        return tuple(maxes)

    def probs(n_keys, slot, ms, maxes):
        new_ms, alphas = [], []
        for idx in range(2):
            m_new = jnp.maximum(ms[idx], maxes[idx])
            alphas.append(jnp.exp2(ms[idx] - m_new))
            e_ref[slot, idx, :n_keys, :] = jnp.exp2(s_ref[slot, idx, :n_keys, :] - m_new).astype(BF16)
            new_ms.append(m_new)
        return tuple(new_ms), tuple(alphas)

    def values(c, slot, alphas):
        vt_ext = _with_ones(values_of(c))
        n_keys = vt_ext.shape[1]
        for idx in range(2):
            acc_ref[idx] = acc_ref[idx] * alphas[idx] + _dot(vt_ext, e_ref[slot, idx, :n_keys, :])

    def n_keys_of(c):
        return cache_chunk if is_cache(c) else kc

    def step(i, slot, ms, maxes_next, alphas_cur):
        maxes_after = scores(i + 2, slot)
        ms, alphas_next = probs(n_keys_of(i + 1), 1 - slot, ms, maxes_next)
        values(i, slot, alphas_cur)
        return ms, maxes_after, alphas_next

    m0 = jnp.full((1, tq), -jnp.inf, F32)
    ms = (m0, m0)
    maxes_cur = scores(0, 0)
    if n_total == 1:
        ms, alphas_cur = probs(n_keys_of(0), 0, ms, maxes_cur)
        values(0, 0, alphas_cur)
    else:
        maxes_next = scores(1, 1)
        ms, alphas_cur = probs(n_keys_of(0), 0, ms, maxes_cur)
        n_steps = n_total - 2
        first_loop = n_cache
        for i in range(min(first_loop, n_steps)):
            ms, maxes_next, alphas_cur = step(i, i % 2, ms, maxes_next, alphas_cur)
        n_pairs = max(n_steps - first_loop, 0) // 2

        def pair(t, carry):
            ms, maxes_next, alphas_cur = carry
            i = first_loop + 2 * t
            ms, maxes_next, alphas_cur = step(i, first_loop % 2, ms, maxes_next, alphas_cur)
            return step(i + 1, (first_loop + 1) % 2, ms, maxes_next, alphas_cur)

        ms, maxes_next, alphas_cur = lax.fori_loop(0, n_pairs, pair, (ms, maxes_next, alphas_cur))
        for i in range(first_loop + 2 * n_pairs, n_steps):
            ms, maxes_next, alphas_cur = step(i, i % 2, ms, maxes_next, alphas_cur)
        last = n_total - 1
        ms, alphas_last = probs(n_keys_of(last), last % 2, ms, maxes_next)
        values(last - 1, (last - 1) % 2, alphas_cur)
        values(last, last % 2, alphas_last)


def _attention(lam_params, g_subln, qt, k, vt, g_att, cache_k=None, cache_v=None, *, tq=2048):
    b, t_len, _ = k.shape
    has_cache = cache_k is not None
    n_chunks, kc = vt.shape[2], vt.shape[4]
    tq = min(tq, t_len)
    heads_per_step = N_HEADS if n_chunks == 1 and not has_cache else 1
    width = heads_per_step * HEAD_W
    s_rows = max(kc, cache_k.shape[1] // N_HEADS) if has_cache else kc
    tok_spec = pl.BlockSpec((None, tq, width), lambda bi, h, qi: (bi, qi, h))
    in_specs = [
        pl.BlockSpec(lam_params.shape, lambda bi, h, qi: (0, 0)),
        pl.BlockSpec((1, HEAD_W), lambda bi, h, qi: (0, 0)),
        pl.BlockSpec((None, width, tq), lambda bi, h, qi: (bi, h, qi)),
        pl.BlockSpec((None, t_len, width), lambda bi, h, qi: (bi, 0, h)),
        pl.BlockSpec((None, heads_per_step, n_chunks, HEAD_W, kc), lambda bi, h, qi: (bi, h, 0, 0, 0)),
        tok_spec,
    ]
    args = [lam_params, g_subln, qt, k, vt, g_att]
    if has_cache:
        c_spec = pl.BlockSpec((None,) + cache_k.shape[1:], lambda bi, h, qi: (bi, 0, 0))
        in_specs += [c_spec, c_spec]
        args += [cache_k, cache_v]
    return pl.pallas_call(
        functools.partial(_attn_kernel, has_cache=has_cache, n_chunks=n_chunks, kc=kc,
                          heads_per_step=heads_per_step),
        grid=(b, N_HEADS // heads_per_step, t_len // tq),
        in_specs=in_specs,
        out_specs=tok_spec,
        out_shape=jax.ShapeDtypeStruct((b, t_len, D_ATT), BF16),
        scratch_shapes=[pltpu.VMEM((2, ACC_ROWS, tq), F32),
                        pltpu.VMEM((2, 2, s_rows, tq), F32),
                        pltpu.VMEM((2, 2, s_rows, tq), BF16)],
        compiler_params=pltpu.CompilerParams(
            dimension_semantics=("arbitrary",) * 3, vmem_limit_bytes=V7X_VMEM_LIMIT_BYTES),
        name="diff_attn_cache" if has_cache else "diff_attn",
    )(*args)


def _lru_kernel(x_ref, g_ref, cw_ref, cb_ref, w_ref, lam_ref, h0_ref,
                o_ref, st_ref,
                xp, a_f, b_f, a_b, b_b, hin_f, hin_b,
                *, blocks_per_step, t_len, chunk, pitch, rows):
    blocks = range(blocks_per_step)
    lanes = [slice(j * LRU_BLOCK, (j + 1) * LRU_BLOCK) for j in blocks]
    g_f, p_f, g_b, p_b = b_f, a_f, b_b, a_b
    n_chunks = t_len // chunk
    chunks_per_tile = rows // chunk

    pad = jnp.zeros((SUBLANES, LRU_BLOCK), F32)
    for j in blocks:
        xp[j, pl.ds(0, SUBLANES), :] = pad
        xp[j, pl.ds(SUBLANES, t_len), :] = x_ref[:, lanes[j]]
        xp[j, pl.ds(SUBLANES + t_len, SUBLANES), :] = pad

    lam = lam_ref[...]
    sp = jnp.maximum(-lam, 0.0) + jnp.log1p(jnp.exp(-jnp.abs(lam)))
    half_scale = (-0.5 * LRU_C) * sp
    cw = cw_ref[...]
    cb = cb_ref[...]
    lane = lax.broadcasted_iota(jnp.int32, (rows, LRU_BLOCK), 1)
    bias_taps = jnp.where(lane < BIAS_SPLIT, 1.0, 0.0).astype(BF16)

    def gate_tile(r, carry):
        t0 = pl.multiple_of(r * rows, SUBLANES)
        for j in blocks:
            u = cb[:, lanes[j]]
            for tap in range(4):
                u = u + xp[j, pl.ds(t0 + SUBLANES - 1 + tap, rows), :] * cw[tap:tap + 1, lanes[j]]
            zh = _dot(jnp.concatenate([u.astype(BF16), bias_taps], axis=1), w_ref[j])
            u_half = 0.5 * u
            for d, (a_s, b_s) in enumerate(((a_f, b_f), (a_b, b_b))):
                scale = half_scale[d:d + 1, lanes[j]]
                tr = jnp.tanh(zh[:, (2 * d) * LRU_BLOCK:(2 * d + 1) * LRU_BLOCK])
                ti = jnp.tanh(zh[:, (2 * d + 1) * LRU_BLOCK:(2 * d + 2) * LRU_BLOCK])
                log_a = scale + scale * tr
                a = jnp.exp(log_a)
                gain_sq = jnp.tanh(log_a) * (-1.0 - a * a)
                gain = jnp.where(gain_sq > 0.0, gain_sq * lax.rsqrt(gain_sq), 0.0)
                bb = gain * (u_half + u_half * ti)
                for cc in range(chunks_per_tile):
                    dst = pl.multiple_of(r * (chunks_per_tile * pitch), SUBLANES) + cc * pitch
                    a_s[j, pl.ds(dst, chunk), :] = a[cc * chunk:(cc + 1) * chunk]
                    b_s[j, pl.ds(dst, chunk), :] = bb[cc * chunk:(cc + 1) * chunk]
        return carry

    lax.fori_loop(0, t_len // rows, gate_tile, 0)

    def local_step(l, carry):
        out = []
        for j in blocks:
            hf, pf, hb, pb = carry[4 * j:4 * j + 4]
            rf = pl.ds(l, n_chunks, stride=pitch)
            a = a_f[j, rf, :]
            hf = a * hf + b_f[j, rf, :]
            pf = a * pf
            g_f[j, rf, :] = hf
            p_f[j, rf, :] = pf
            rb = pl.ds(chunk - 1 - l, n_chunks, stride=pitch)
            a = a_b[j, rb, :]
            hb = a * hb + b_b[j, rb, :]
            pb = a * pb
            g_b[j, rb, :] = hb
            p_b[j, rb, :] = pb
            out += [hf, pf, hb, pb]
        return tuple(out)

    z0 = jnp.zeros((n_chunks, LRU_BLOCK), F32)
    o0 = jnp.ones((n_chunks, LRU_BLOCK), F32)
    chunk_maps = lax.fori_loop(0, chunk, local_step, (z0, o0, z0, o0) * blocks_per_step, unroll=4)

    h0 = h0_ref[...]
    chunk_id = lax.broadcasted_iota(jnp.int32, (n_chunks, LRU_BLOCK), 0)

    def compose(end, decay, towards_higher):
        shift = 1
        while shift < n_chunks:
            if towards_higher:
                valid = chunk_id >= shift
                amount = shift
            else:
                valid = chunk_id < n_chunks - shift
                amount = n_chunks - shift
            prev_end = jnp.where(valid, pltpu.roll(end, amount, 0), 0.0)
            prev_decay = jnp.where(valid, pltpu.roll(decay, amount, 0), 1.0)
            end = end + decay * prev_end
            decay = decay * prev_decay
            shift *= 2
        return end, decay

    for j in blocks:
        end_f, decay_f, end_b, decay_b = chunk_maps[4 * j:4 * j + 4]
        h0_f = h0[0:1, lanes[j]]
        h0_b = h0[1:2, lanes[j]]
        end_f, decay_f = compose(end_f, decay_f, True)
        after_f = end_f + decay_f * h0_f
        hin_f[j] = jnp.where(chunk_id == 0, h0_f, pltpu.roll(after_f, 1, 0))
        st_ref[0:1, lanes[j]] = after_f[n_chunks - 1:n_chunks]
        end_b, decay_b = compose(end_b, decay_b, False)
        after_b = end_b + decay_b * h0_b
        hin_b[j] = jnp.where(chunk_id == n_chunks - 1, h0_b, pltpu.roll(after_b, n_chunks - 1, 0))
        st_ref[1:2, lanes[j]] = after_b[0:1]

    def out_pair(cp, carry):
        for par in range(2):
            c = 2 * cp + par
            src = pl.ds(pl.multiple_of(cp * (2 * pitch), SUBLANES) + par * pitch, chunk)
            dst = pl.ds(pl.multiple_of(c * chunk, SUBLANES), chunk)
            for j in blocks:
                hf = g_f[j, src, :] + p_f[j, src, :] * hin_f[j, pl.ds(c, 1), :]
                hb = g_b[j, src, :] + p_b[j, src, :] * hin_b[j, pl.ds(c, 1), :]
                o_ref[dst, lanes[j]] = ((hf + hb) * _silu(g_ref[dst, lanes[j]])).astype(BF16)
        return carry

    lax.fori_loop(0, n_chunks // 2, out_pair, 0, unroll=4)


def _lru_gate_operand(w_r, w_i, b_r, b_i):
    weights = 0.5 * jnp.concatenate([w_r[0], w_i[0], w_r[1], w_i[1]], axis=-1)
    bias = 0.5 * jnp.concatenate(
        [bb.reshape(N_LRU_BLOCKS, 1, LRU_BLOCK) for bb in (b_r[0], b_i[0], b_r[1], b_i[1])], axis=-1)
    parts, rest = [], bias
    for _ in range(BIAS_SPLIT):
        gamma = rest * (2.0 ** 16 + 1.0)
        part = gamma - (gamma - rest)
        parts.append(part)
        rest = rest - part
    bias_rows = jnp.pad(jnp.concatenate(parts, axis=1), ((0, 0), (0, LRU_BLOCK - BIAS_SPLIT), (0, 0)))
    return jnp.concatenate([weights, bias_rows], axis=1).astype(BF16)


def _lru(x_lru, g_lru, conv_w, conv_b, w_ext, lru_lam, h0):
    b, t_len, _ = x_lru.shape
    chunk = 64 if t_len >= 2048 else 32
    pitch = chunk + SUBLANES // 2
    n_chunks = t_len // chunk
    rows = min(4096, t_len)
    blocks_per_step = N_LRU_BLOCKS if t_len <= 512 else 1
    width = blocks_per_step * LRU_BLOCK
    seq_spec = pl.BlockSpec((None, t_len, width), lambda bi, n: (bi, 0, n))
    st_spec = pl.BlockSpec((None, 2, width), lambda bi, n: (bi, 0, n))
    scan_buf = pltpu.VMEM((blocks_per_step, n_chunks * pitch, LRU_BLOCK), F32)
    return pl.pallas_call(
        functools.partial(_lru_kernel, blocks_per_step=blocks_per_step,
                          t_len=t_len, chunk=chunk, pitch=pitch, rows=rows),
        grid=(b, N_LRU_BLOCKS // blocks_per_step),
        in_specs=[
            seq_spec, seq_spec,
            pl.BlockSpec((4, width), lambda bi, n: (0, n)),
            pl.BlockSpec((1, width), lambda bi, n: (0, n)),
            pl.BlockSpec((blocks_per_step, 2 * LRU_BLOCK, 4 * LRU_BLOCK), lambda bi, n: (n, 0, 0)),
            pl.BlockSpec((2, width), lambda bi, n: (0, n)),
            st_spec,
        ],
        out_specs=[seq_spec, st_spec],
        out_shape=[jax.ShapeDtypeStruct((b, t_len, D_LRU), BF16),
                   jax.ShapeDtypeStruct((b, 2, D_LRU), F32)],
        scratch_shapes=[
            pltpu.VMEM((blocks_per_step, t_len + 2 * SUBLANES, LRU_BLOCK), F32),
            scan_buf, scan_buf, scan_buf, scan_buf,
            pltpu.VMEM((blocks_per_step, n_chunks, LRU_BLOCK), F32),
            pltpu.VMEM((blocks_per_step, n_chunks, LRU_BLOCK), F32),
        ],
        compiler_params=pltpu.CompilerParams(
            dimension_semantics=("arbitrary",) * 2, vmem_limit_bytes=V7X_VMEM_LIMIT_BYTES),
        name="rglru",
    )(x_lru, g_lru, conv_w, conv_b, w_ext, lru_lam, h0)


def _outproj_kernel(att_ref, lru_ref, x_ref, gate_ref, wa_ref, wl_ref, gpost_ref, y_ref):
    o = _dot(att_ref[...], wa_ref[...]) + _dot(lru_ref[...], wl_ref[...])
    ms = jnp.mean(o * o, axis=-1, keepdims=True)
    n = o * lax.rsqrt(ms + EPS) * gpost_ref[...]
    y_ref[...] = x_ref[...] + gate_ref[...] * n


def _outproj(att, lru, x, gate, w_out_bf16, g_post, *, t_len, tm=1024):
    n_tok = x.shape[0]
    nb = gate.shape[0]
    if nb == 1:
        tm = min(tm, n_tok)
        mod_map = lambda i: (0, 0, 0)
    else:
        tm = min(tm, t_len)
        tiles_per_batch = t_len // tm
        mod_map = lambda i: (i // tiles_per_batch, 0, 0)
    tok_spec = pl.BlockSpec((tm, D_MODEL), lambda i: (i, 0))
    return pl.pallas_call(
        _outproj_kernel,
        grid=(n_tok // tm,),
        in_specs=[tok_spec, tok_spec, tok_spec,
                  pl.BlockSpec((None, 1, D_MODEL), mod_map),
                  pl.BlockSpec((D_ATT, D_MODEL), lambda i: (0, 0)),
                  pl.BlockSpec((D_LRU, D_MODEL), lambda i: (D_ATT // D_LRU, 0)),
                  pl.BlockSpec((1, D_MODEL), lambda i: (0, 0))],
        out_specs=tok_spec,
        out_shape=jax.ShapeDtypeStruct((n_tok, D_MODEL), F32),
        compiler_params=pltpu.CompilerParams(
            dimension_semantics=("arbitrary",), vmem_limit_bytes=V7X_VMEM_LIMIT_BYTES),
        name="outproj",
    )(att, lru, x, gate, w_out_bf16, w_out_bf16, g_post)


def _sublayer(x, scale, shift, gate, cache_k, cache_v, h0, use_rope, w):
    b, t_len, _ = x.shape
    xf = x.reshape(b * t_len, D_MODEL)
    emit_kv = cache_k is None
    outs = _inproj(xf, scale, shift, w["g_pre"], w["w_in"], t_len=t_len,
                   use_rope=use_rope, emit_kv_f32=emit_kv)
    qt, vt = outs[0], outs[2]
    k, g_att, x_lru, g_lru = [outs[i].reshape(b, t_len, D_MODEL) for i in (1, 3, 4, 5)]
    att = _attention(w["lam"], w["g_subln"], qt, k, vt, g_att, cache_k, cache_v)
    lru, state = _lru(x_lru, g_lru, w["conv_w"], w["conv_b"], w["w_ext"], w["lru_lam"], h0)
    y = _outproj(att.reshape(b * t_len, D_ATT), lru.reshape(b * t_len, D_LRU), xf, gate,
                 w["w_out"], w["g_post"], t_len=t_len)
    y = y.reshape(b, t_len, D_MODEL)
    if emit_kv:
        return y, outs[6], outs[7], state
    return y, None, None, state


def kernel(x_prompt, x_sample, cache_k, cache_v, state_lru, c, c_ctx, w_ada, b_ada, g_pre, w_in, lambda_q1, lambda_k1, lambda_q2, lambda_k2, g_subln, conv_w, conv_b, w_rgate, b_rgate, w_igate, b_igate, lru_lambda, w_out, g_post):
    bp, seq, _ = x_prompt.shape
    bd, kc = cache_k.shape[0], cache_k.shape[2]
    assert cache_k.shape[1] == 1 and w_in.shape[0] == 1, "single-layer step only"
    l = 0

    cond = jnp.concatenate([c, c_ctx[None, :], jnp.zeros((SUBLANES - bd - 1, D_MODEL), F32)], axis=0)
    mod, w_in_bf16, w_out_bf16 = _modulation(cond, w_ada[l], b_ada[l][None, :], w_in[l], w_out[l])
    shift, scale, gate = [mod[:, i * D_MODEL:(i + 1) * D_MODEL] for i in range(3)]

    def rows(a, lo, hi):
        return a[lo:hi][:, None, :]

    w = {
        "g_pre": g_pre[l][None, :],
        "w_in": w_in_bf16,
        "lam": jnp.stack([lambda_q1[l], lambda_k1[l], lambda_q2[l], lambda_k2[l]], axis=0),
        "g_subln": g_subln[l][None, :],
        "conv_w": conv_w[l],
        "conv_b": conv_b[l][None, :],
        "w_ext": _lru_gate_operand(w_rgate[l], w_igate[l], b_rgate[l], b_igate[l]),
        "lru_lam": lru_lambda[l],
        "w_out": w_out_bf16,
        "g_post": g_post[l][None, :],
    }

    y_p, new_k, new_v, st_p = _sublayer(
        x_prompt, rows(scale, bd, bd + 1), rows(shift, bd, bd + 1), rows(gate, bd, bd + 1),
        None, None, jnp.zeros((bp, 2, D_LRU), F32), False, w)

    y_s, _, _, _ = _sublayer(
        x_sample, rows(scale, 0, bd), rows(shift, 0, bd), rows(gate, 0, bd),
        cache_k.reshape(bd, kc * N_HEADS, HEAD_W), cache_v.reshape(bd, kc * N_HEADS, HEAD_W),
        state_lru.reshape(bd, 2, D_LRU), True, w)

    new_k = new_k.reshape(bp, 1, seq, N_HEADS, HEAD_W)
    new_v = new_v.reshape(bp, 1, seq, N_HEADS, HEAD_W)
    return (y_p, y_s, new_k, new_v, st_p[:, None])
```

```python
import functools
import math

import jax
import jax.numpy as jnp
import numpy as np
from jax import lax
from jax.experimental import pallas as pl
from jax.experimental.pallas import tpu as pltpu

F32 = jnp.float32
BF16 = jnp.bfloat16

D_MODEL = 1024
GRID_W = 64
N_HEADS = 8
DIFF_HEAD_DIM = 64
HEAD_W = 2 * DIFF_HEAD_DIM
D_ATT = N_HEADS * HEAD_W
D_LRU = 1024
N_LRU_BLOCKS = 8
LRU_BLOCK = D_LRU // N_LRU_BLOCKS
LRU_C = 8.0
N_GROUPS = 6
ROPE_BASE = 10000.0
EPS = 1e-6
LAM_INIT = 0.8 - 0.6 * math.exp(-0.3 * 0)
LOG2_E = math.log2(math.e)

V7X_VMEM_LIMIT_BYTES = 56 * 1024 * 1024
SUBLANES = 8
ROPE_SWAP = DIFF_HEAD_DIM // 4
BF16_ROWS_PER_VREG = 16
ACC_ROWS = HEAD_W + BF16_ROWS_PER_VREG
CACHE_CHUNK = 256
BIAS_SPLIT = 3


def _silu(x):
    return x * jax.nn.sigmoid(x)


def _dot(a, b):
    return jnp.dot(a, b, preferred_element_type=F32)


def _mod_kernel(cond_ref, w_ref, b_ref, win_ref, wout_ref, o_ref, win_bf_ref, wout_bf_ref):
    s = _silu(cond_ref[...])
    o_ref[...] = _dot(s.astype(BF16), w_ref[...].astype(BF16)) + b_ref[...]
    win_bf_ref[...] = win_ref[...].astype(BF16)
    wout_bf_ref[...] = wout_ref[...].astype(BF16)


def _modulation(cond, w_ada, b_ada, w_in, w_out, *, steps=8):
    n = cond.shape[0]
    d_in, d_mix = w_in.shape[1], w_out.shape[0]
    mod_w, in_w, out_r = 3 * D_MODEL // steps, d_in // steps, d_mix // steps
    return pl.pallas_call(
        _mod_kernel,
        grid=(steps,),
        in_specs=[
            pl.BlockSpec((n, D_MODEL), lambda j: (0, 0)),
            pl.BlockSpec((D_MODEL, mod_w), lambda j: (0, j)),
            pl.BlockSpec((1, mod_w), lambda j: (0, j)),
            pl.BlockSpec((D_MODEL, in_w), lambda j: (0, j)),
            pl.BlockSpec((out_r, D_MODEL), lambda j: (j, 0)),
        ],
        out_specs=[
            pl.BlockSpec((n, mod_w), lambda j: (0, j)),
            pl.BlockSpec((D_MODEL, in_w), lambda j: (0, j)),
            pl.BlockSpec((out_r, D_MODEL), lambda j: (j, 0)),
        ],
        out_shape=[
            jax.ShapeDtypeStruct((n, 3 * D_MODEL), F32),
            jax.ShapeDtypeStruct(w_in.shape, BF16),
            jax.ShapeDtypeStruct(w_out.shape, BF16),
        ],
        name="modulation",
    )(cond, w_ada, b_ada, w_in, w_out)


def _rope_tables(t_len):
    half = DIFF_HEAD_DIM // 2
    nf = half // 2
    t = np.arange(t_len)
    row = (t // GRID_W).astype(np.float32)
    col = (t % GRID_W).astype(np.float32)
    inv = (ROPE_BASE ** (-np.arange(nf, dtype=np.float32) * 2.0 / half)).astype(np.float32)
    lane = np.arange(HEAD_W) % DIFF_HEAD_DIM
    use_row = lane < half
    freq = (lane % half) % nf
    first = (lane % half) < nf
    pos = np.where(use_row[None, :], row[:, None], col[:, None]).astype(np.float32)
    ang = (pos * inv[freq][None, :]).astype(np.float32).astype(np.float64)
    cos = np.cos(ang).astype(np.float32)
    sin = (np.sin(ang) * np.where(first, -1.0, 1.0)[None, :]).astype(np.float32)
    return jnp.asarray(cos), jnp.asarray(sin)


def _inproj_kernel(*refs, use_rope, emit_kv_f32):
    x_ref, scale_ref, shift_ref, gpre_ref, w_ref = refs[:5]
    refs = refs[5:]
    if use_rope:
        cos_ref, sin_ref = refs[:2]
        refs = refs[2:]
    q_ref, k_ref, v_ref, gatt_ref, xlru_ref, glru_ref = refs[:6]
    refs = refs[6:]
    if emit_kv_f32:
        kf_ref, vf_ref = refs

    x = x_ref[...]
    ms = jnp.mean(x * x, axis=-1, keepdims=True)
    y = x * lax.rsqrt(ms + EPS) * gpre_ref[...]
    h = (y * (1.0 + scale_ref[...]) + shift_ref[...]).astype(BF16)

    def proj(g):
        return _dot(h, w_ref[:, g * D_MODEL:(g + 1) * D_MODEL])

    if use_rope:
        cos = cos_ref[...]
        sin = sin_ref[...]
        lane = lax.broadcasted_iota(jnp.int32, cos.shape, 1)
        take_next = (lane % (2 * ROPE_SWAP)) < ROPE_SWAP

        def rope(p):
            outs = []
            for hd in range(N_HEADS):
                xh = p[:, hd * HEAD_W:(hd + 1) * HEAD_W]
                partner = jnp.where(take_next,
                                    pltpu.roll(xh, HEAD_W - ROPE_SWAP, 1),
                                    pltpu.roll(xh, ROPE_SWAP, 1))
                outs.append(xh * cos + partner * sin)
            return jnp.concatenate(outs, axis=-1)
    else:
        def rope(p):
            return p

    q = rope(proj(0)) * (LOG2_E / math.sqrt(DIFF_HEAD_DIM))
    q_ref[...] = q.T.astype(BF16)
    k = rope(proj(1))
    k_ref[...] = k.astype(BF16)
    v = proj(2)
    vt = v.T
    for hd in range(N_HEADS):
        v_ref[hd] = vt[hd * HEAD_W:(hd + 1) * HEAD_W].astype(BF16)
    if emit_kv_f32:
        kf_ref[...] = k
        vf_ref[...] = v
    gatt_ref[...] = proj(3)
    xlru_ref[...] = proj(4)
    glru_ref[...] = proj(5)


def _inproj(x, scale, shift, g_pre, w_in_bf16, *, t_len, use_rope, emit_kv_f32, tm=512):
    n_tok = x.shape[0]
    tm = min(tm, t_len)
    tiles_per_batch = t_len // tm
    nb = scale.shape[0]
    if nb == 1:
        mod_map = lambda i: (0, 0, 0)
    else:
        mod_map = lambda i: (i // tiles_per_batch, 0, 0)
    tok_spec = pl.BlockSpec((tm, D_MODEL), lambda i: (i, 0))
    in_specs = [
        tok_spec,
        pl.BlockSpec((None, 1, D_MODEL), mod_map),
        pl.BlockSpec((None, 1, D_MODEL), mod_map),
        pl.BlockSpec((1, D_MODEL), lambda i: (0, 0)),
        pl.BlockSpec((D_MODEL, N_GROUPS * D_MODEL), lambda i: (0, 0), pipeline_mode=pl.Buffered(1)),
    ]
    args = [x, scale, shift, g_pre, w_in_bf16]
    if use_rope:
        cos, sin = _rope_tables(t_len)
        rope_spec = pl.BlockSpec((tm, HEAD_W), lambda i: (i % tiles_per_batch, 0))
        in_specs += [rope_spec, rope_spec]
        args += [cos, sin]
    n_batch = n_tok // t_len
    kc = _key_chunk(t_len)
    tiles_per_chunk = kc // tm
    out_specs = [
        pl.BlockSpec((None, D_MODEL, tm), lambda i: (i // tiles_per_batch, 0, i % tiles_per_batch)),
        tok_spec,
        pl.BlockSpec((None, N_HEADS, None, HEAD_W, tm),
                     lambda i: (i // tiles_per_batch, 0, (i % tiles_per_batch) // tiles_per_chunk, 0,
                                i % tiles_per_chunk)),
        tok_spec, tok_spec, tok_spec,
    ]
    out_shape = [
        jax.ShapeDtypeStruct((n_batch, D_MODEL, t_len), BF16),
        jax.ShapeDtypeStruct((n_tok, D_MODEL), BF16),
        jax.ShapeDtypeStruct((n_batch, N_HEADS, t_len // kc, HEAD_W, kc), BF16),
    ] + [jax.ShapeDtypeStruct((n_tok, D_MODEL), F32)] * 3
    if emit_kv_f32:
        out_specs += [tok_spec] * 2
        out_shape += [jax.ShapeDtypeStruct((n_tok, D_MODEL), F32)] * 2
    return pl.pallas_call(
        functools.partial(_inproj_kernel, use_rope=use_rope, emit_kv_f32=emit_kv_f32),
        grid=(n_tok // tm,),
        in_specs=in_specs,
        out_specs=out_specs,
        out_shape=out_shape,
        compiler_params=pltpu.CompilerParams(
            dimension_semantics=("arbitrary",), vmem_limit_bytes=V7X_VMEM_LIMIT_BYTES),
        name="inproj_rope" if use_rope else "inproj",
    )(*args)


def _key_chunk(t_len):
    return min(512, t_len)


def _attn_kernel(*refs, has_cache, n_chunks, kc, heads_per_step):
    lam_ref, gsub_ref, qt_ref, k_ref, vt_ref, gatt_ref = refs[:6]
    refs = refs[6:]
    cache_refs = None
    if has_cache:
        cache_refs = refs[:2]
        refs = refs[2:]
    o_ref, acc_ref, s_ref, e_ref = refs

    lp = lam_ref[...]
    lam = (jnp.exp(jnp.sum(lp[0:1] * lp[1:2], axis=-1, keepdims=True))
           - jnp.exp(jnp.sum(lp[2:3] * lp[3:4], axis=-1, keepdims=True)) + LAM_INIT)
    g_subln = gsub_ref[...]

    if n_chunks == 1 and not has_cache:
        _attn_single_chunk(lam, g_subln, qt_ref, k_ref, vt_ref, gatt_ref, o_ref, s_ref, e_ref,
                           n_heads=heads_per_step)
        return
    for hh in range(heads_per_step):
        lanes = slice(hh * HEAD_W, (hh + 1) * HEAD_W)
        _attn_accumulate(qt_ref.at[lanes, :], k_ref.at[:, lanes], vt_ref.at[hh], cache_refs,
                         pl.program_id(1) * heads_per_step + hh, acc_ref, s_ref, e_ref,
                         n_chunks=n_chunks, kc=kc)
        _attn_finalize(lam, g_subln, acc_ref[0], acc_ref[1], gatt_ref.at[:, lanes], o_ref.at[:, lanes])


def _query_maps(qt):
    row = lax.broadcasted_iota(jnp.int32, qt.shape, 0)
    zero = jnp.zeros_like(qt)
    return jnp.where(row < DIFF_HEAD_DIM, qt, zero), jnp.where(row >= DIFF_HEAD_DIM, qt, zero)


def _with_ones(vt_chunk):
    ones = jnp.ones((ACC_ROWS - HEAD_W, vt_chunk.shape[1]), BF16)
    return jnp.concatenate([vt_chunk, ones], axis=0)


def _attn_single_chunk(lam, g_subln, qt_ref, k_ref, vt_ref, gatt_ref, o_ref, s_ref, e_ref, *, n_heads):
    maxes = {}
    for t in range(n_heads + 2):
        if t < n_heads:
            lanes = slice(t * HEAD_W, (t + 1) * HEAD_W)
            keys = k_ref[:, lanes]
            q_maps = _query_maps(qt_ref[lanes, :])
            head_max = []
            for idx in range(2):
                st = _dot(keys, q_maps[idx])
                s_ref[t % 2, idx] = st
                head_max.append(jnp.max(st, axis=0, keepdims=True))
            maxes[t] = head_max
        if 1 <= t <= n_heads:
            h = t - 1
            for idx in range(2):
                e_ref[h % 2, idx] = jnp.exp2(s_ref[h % 2, idx] - maxes[h][idx]).astype(BF16)
        if t >= 2:
            h = t - 2
            lanes = slice(h * HEAD_W, (h + 1) * HEAD_W)
            vt_ext = _with_ones(vt_ref[h, 0])
            acc1, acc2 = [_dot(vt_ext, e_ref[h % 2, idx]) for idx in range(2)]
            _attn_finalize(lam, g_subln, acc1, acc2, gatt_ref.at[:, lanes], o_ref.at[:, lanes])


def _attn_finalize(lam, g_subln, acc1, acc2, gatt_ref, o_ref):
    o1 = acc1[:HEAD_W] * (1.0 / acc1[HEAD_W:HEAD_W + 1])
    o2 = acc2[:HEAD_W] * (1.0 / acc2[HEAD_W:HEAD_W + 1])
    ot = o1 - lam * o2
    ms = jnp.mean(ot * ot, axis=0, keepdims=True)
    on = (ot * lax.rsqrt(ms + EPS)).T * (g_subln * (1.0 - LAM_INIT))
    o_ref[...] = (on * _silu(gatt_ref[...])).astype(BF16)


def _attn_accumulate(qt_ref, k_ref, vt_ref, cache_refs, head, acc_ref, s_ref, e_ref, *, n_chunks, kc):
    has_cache = cache_refs is not None
    tq = qt_ref.shape[1]
    q_maps = _query_maps(qt_ref[...])

    acc_ref[...] = jnp.zeros(acc_ref.shape, F32)

    n_cache = 0
    if has_cache:
        kc_ref, vc_ref = cache_refs
        cache_len = kc_ref.shape[0] // N_HEADS
        cache_chunk = min(CACHE_CHUNK, cache_len)
        n_cache = cache_len // cache_chunk
    n_total = n_chunks + n_cache

    def is_cache(c):
        return isinstance(c, int) and c < n_cache

    def cache_rows(c):
        return pl.ds(head + c * cache_chunk * N_HEADS, cache_chunk, stride=N_HEADS)

    def keys_of(c):
        if is_cache(c):
            return kc_ref[cache_rows(c), :].astype(BF16)
        start = (c - n_cache) * kc
        return k_ref[pl.ds(start if isinstance(c, int) else pl.multiple_of(start, kc), kc), :]

    def values_of(c):
        if is_cache(c):
            return vc_ref[cache_rows(c), :].T.astype(BF16)
        return vt_ref[c - n_cache]

    def scores(c, slot):
        k_chunk = keys_of(c)
        maxes = []
        for idx in range(2):
            st = _dot(k_chunk, q_maps[idx])
            s_ref[slot, idx, :k_chunk.shape[0], :] = st
            maxes.append(jnp.max(st, axis=0, keepdims=True))
        return tuple(maxes)

    def probs(n_keys, slot, ms, maxes):
        new_ms, alphas = [], []
        for idx in range(2):
            m_new = jnp.maximum(ms[idx], maxes[idx])
            alphas.append(jnp.exp2(ms[idx] - m_new))
            e_ref[slot, idx, :n_keys, :] = jnp.exp2(s_ref[slot, idx, :n_keys, :] - m_new).astype(BF16)
            new_ms.append(m_new)
        return tuple(new_ms), tuple(alphas)

    def values(c, slot, alphas):
        vt_ext = _with_ones(values_of(c))
        n_keys = vt_ext.shape[1]
        for idx in range(2):
            acc_ref[idx] = acc_ref[idx] * alphas[idx] + _dot(vt_ext, e_ref[slot, idx, :n_keys, :])

    def n_keys_of(c):
        return cache_chunk if is_cache(c) else kc

    def step(i, slot, ms, maxes_next, alphas_cur):
        maxes_after = scores(i + 2, slot)
        ms, alphas_next = probs(n_keys_of(i + 1), 1 - slot, ms, maxes_next)
        values(i, slot, alphas_cur)
        return ms, maxes_after, alphas_next

    m0 = jnp.full((1, tq), -jnp.inf, F32)
    ms = (m0, m0)
    maxes_cur = scores(0, 0)
    if n_total == 1:
        ms, alphas_cur = probs(n_keys_of(0), 0, ms, maxes_cur)
        values(0, 0, alphas_cur)
    else:
        maxes_next = scores(1, 1)
        ms, alphas_cur = probs(n_keys_of(0), 0, ms, maxes_cur)
        n_steps = n_total - 2
        first_loop = n_cache
        for i in range(min(first_loop, n_steps)):
            ms, maxes_next, alphas_cur = step(i, i % 2, ms, maxes_next, alphas_cur)
        n_pairs = max(n_steps - first_loop, 0) // 2

        def pair(t, carry):
            ms, maxes_next, alphas_cur = carry
            i = first_loop + 2 * t
            ms, maxes_next, alphas_cur = step(i, first_loop % 2, ms, maxes_next, alphas_cur)
            return step(i + 1, (first_loop + 1) % 2, ms, maxes_next, alphas_cur)

        ms, maxes_next, alphas_cur = lax.fori_loop(0, n_pairs, pair, (ms, maxes_next, alphas_cur))
        for i in range(first_loop + 2 * n_pairs, n_steps):
            ms, maxes_next, alphas_cur = step(i, i % 2, ms, maxes_next, alphas_cur)
        last = n_total - 1
        ms, alphas_last = probs(n_keys_of(last), last % 2, ms, maxes_next)
        values(last - 1, (last - 1) % 2, alphas_cur)
        values(last, last % 2, alphas_last)


def _attention(lam_params, g_subln, qt, k, vt, g_att, cache_k=None, cache_v=None, *, tq=2048):
    b, t_len, _ = k.shape
    has_cache = cache_k is not None
    n_chunks, kc = vt.shape[2], vt.shape[4]
    tq = min(tq, t_len)
    heads_per_step = N_HEADS if n_chunks == 1 and not has_cache else 1
    width = heads_per_step * HEAD_W
    s_rows = max(kc, cache_k.shape[1] // N_HEADS) if has_cache else kc
    tok_spec = pl.BlockSpec((None, tq, width), lambda bi, h, qi: (bi, qi, h))
    in_specs = [
        pl.BlockSpec(lam_params.shape, lambda bi, h, qi: (0, 0)),
        pl.BlockSpec((1, HEAD_W), lambda bi, h, qi: (0, 0)),
        pl.BlockSpec((None, width, tq), lambda bi, h, qi: (bi, h, qi)),
        pl.BlockSpec((None, t_len, width), lambda bi, h, qi: (bi, 0, h)),
        pl.BlockSpec((None, heads_per_step, n_chunks, HEAD_W, kc), lambda bi, h, qi: (bi, h, 0, 0, 0)),
        tok_spec,
    ]
    args = [lam_params, g_subln, qt, k, vt, g_att]
    if has_cache:
        c_spec = pl.BlockSpec((None,) + cache_k.shape[1:], lambda bi, h, qi: (bi, 0, 0))
        in_specs += [c_spec, c_spec]
        args += [cache_k, cache_v]
    return pl.pallas_call(
        functools.partial(_attn_kernel, has_cache=has_cache, n_chunks=n_chunks, kc=kc,
                          heads_per_step=heads_per_step),
        grid=(b, N_HEADS // heads_per_step, t_len // tq),
        in_specs=in_specs,
        out_specs=tok_spec,
        out_shape=jax.ShapeDtypeStruct((b, t_len, D_ATT), BF16),
        scratch_shapes=[pltpu.VMEM((2, ACC_ROWS, tq), F32),
                        pltpu.VMEM((2, 2, s_rows, tq), F32),
                        pltpu.VMEM((2, 2, s_rows, tq), BF16)],
        compiler_params=pltpu.CompilerParams(
            dimension_semantics=("arbitrary",) * 3, vmem_limit_bytes=V7X_VMEM_LIMIT_BYTES),
        name="diff_attn_cache" if has_cache else "diff_attn",
    )(*args)


def _lru_kernel(x_ref, g_ref, cw_ref, cb_ref, w_ref, lam_ref, h0_ref,
                o_ref, st_ref,
                xp, a_f, b_f, a_b, b_b, hin_f, hin_b,
                *, blocks_per_step, t_len, chunk, pitch, rows):
    blocks = range(blocks_per_step)
    lanes = [slice(j * LRU_BLOCK, (j + 1) * LRU_BLOCK) for j in blocks]
    g_f, p_f, g_b, p_b = b_f, a_f, b_b, a_b
    n_chunks = t_len // chunk
    chunks_per_tile = rows // chunk

    pad = jnp.zeros((SUBLANES, LRU_BLOCK), F32)
    for j in blocks:
        xp[j, pl.ds(0, SUBLANES), :] = pad
        xp[j, pl.ds(SUBLANES, t_len), :] = x_ref[:, lanes[j]]
        xp[j, pl.ds(SUBLANES + t_len, SUBLANES), :] = pad

    lam = lam_ref[...]
    sp = jnp.maximum(-lam, 0.0) + jnp.log1p(jnp.exp(-jnp.abs(lam)))
    half_scale = (-0.5 * LRU_C) * sp
    cw = cw_ref[...]
    cb = cb_ref[...]
    lane = lax.broadcasted_iota(jnp.int32, (rows, LRU_BLOCK), 1)
    bias_taps = jnp.where(lane < BIAS_SPLIT, 1.0, 0.0).astype(BF16)

    def gate_tile(r, carry):
        t0 = pl.multiple_of(r * rows, SUBLANES)
        for j in blocks:
            u = cb[:, lanes[j]]
            for tap in range(4):
                u = u + xp[j, pl.ds(t0 + SUBLANES - 1 + tap, rows), :] * cw[tap:tap + 1, lanes[j]]
            zh = _dot(jnp.concatenate([u.astype(BF16), bias_taps], axis=1), w_ref[j])
            u_half = 0.5 * u
            for d, (a_s, b_s) in enumerate(((a_f, b_f), (a_b, b_b))):
                scale = half_scale[d:d + 1, lanes[j]]
                tr = jnp.tanh(zh[:, (2 * d) * LRU_BLOCK:(2 * d + 1) * LRU_BLOCK])
                ti = jnp.tanh(zh[:, (2 * d + 1) * LRU_BLOCK:(2 * d + 2) * LRU_BLOCK])
                log_a = scale + scale * tr
                a = jnp.exp(log_a)
                gain_sq = jnp.tanh(log_a) * (-1.0 - a * a)
                gain = jnp.where(gain_sq > 0.0, gain_sq * lax.rsqrt(gain_sq), 0.0)
                bb = gain * (u_half + u_half * ti)
                for cc in range(chunks_per_tile):
                    dst = pl.multiple_of(r * (chunks_per_tile * pitch), SUBLANES) + cc * pitch
                    a_s[j, pl.ds(dst, chunk), :] = a[cc * chunk:(cc + 1) * chunk]
                    b_s[j, pl.ds(dst, chunk), :] = bb[cc * chunk:(cc + 1) * chunk]
        return carry

    lax.fori_loop(0, t_len // rows, gate_tile, 0)

    def local_step(l, carry):
        out = []
        for j in blocks:
            hf, pf, hb, pb = carry[4 * j:4 * j + 4]
            rf = pl.ds(l, n_chunks, stride=pitch)
            a = a_f[j, rf, :]
            hf = a * hf + b_f[j, rf, :]
            pf = a * pf
            g_f[j, rf, :] = hf
            p_f[j, rf, :] = pf
            rb = pl.ds(chunk - 1 - l, n_chunks, stride=pitch)
            a = a_b[j, rb, :]
            hb = a * hb + b_b[j, rb, :]
            pb = a * pb
            g_b[j, rb, :] = hb
            p_b[j, rb, :] = pb
            out += [hf, pf, hb, pb]
        return tuple(out)

    z0 = jnp.zeros((n_chunks, LRU_BLOCK), F32)
    o0 = jnp.ones((n_chunks, LRU_BLOCK), F32)
    chunk_maps = lax.fori_loop(0, chunk, local_step, (z0, o0, z0, o0) * blocks_per_step, unroll=4)

    h0 = h0_ref[...]
    chunk_id = lax.broadcasted_iota(jnp.int32, (n_chunks, LRU_BLOCK), 0)

    def compose(end, decay, towards_higher):
        shift = 1
        while shift < n_chunks:
            if towards_higher:
                valid = chunk_id >= shift
                amount = shift
            else:
                valid = chunk_id < n_chunks - shift
                amount = n_chunks - shift
            prev_end = jnp.where(valid, pltpu.roll(end, amount, 0), 0.0)
            prev_decay = jnp.where(valid, pltpu.roll(decay, amount, 0), 1.0)
            end = end + decay * prev_end
            decay = decay * prev_decay
            shift *= 2
        return end, decay

    for j in blocks:
        end_f, decay_f, end_b, decay_b = chunk_maps[4 * j:4 * j + 4]
        h0_f = h0[0:1, lanes[j]]
        h0_b = h0[1:2, lanes[j]]
        end_f, decay_f = compose(end_f, decay_f, True)
        after_f = end_f + decay_f * h0_f
        hin_f[j] = jnp.where(chunk_id == 0, h0_f, pltpu.roll(after_f, 1, 0))
        st_ref[0:1, lanes[j]] = after_f[n_chunks - 1:n_chunks]
        end_b, decay_b = compose(end_b, decay_b, False)
        after_b = end_b + decay_b * h0_b
        hin_b[j] = jnp.where(chunk_id == n_chunks - 1, h0_b, pltpu.roll(after_b, n_chunks - 1, 0))
        st_ref[1:2, lanes[j]] = after_b[0:1]

    def out_pair(cp, carry):
        for par in range(2):
            c = 2 * cp + par
            src = pl.ds(pl.multiple_of(cp * (2 * pitch), SUBLANES) + par * pitch, chunk)
            dst = pl.ds(pl.multiple_of(c * chunk, SUBLANES), chunk)
            for j in blocks:
                hf = g_f[j, src, :] + p_f[j, src, :] * hin_f[j, pl.ds(c, 1), :]
                hb = g_b[j, src, :] + p_b[j, src, :] * hin_b[j, pl.ds(c, 1), :]
                o_ref[dst, lanes[j]] = ((hf + hb) * _silu(g_ref[dst, lanes[j]])).astype(BF16)
        return carry

    lax.fori_loop(0, n_chunks // 2, out_pair, 0, unroll=4)


def _lru_gate_operand(w_r, w_i, b_r, b_i):
    weights = 0.5 * jnp.concatenate([w_r[0], w_i[0], w_r[1], w_i[1]], axis=-1)
    bias = 0.5 * jnp.concatenate(
        [bb.reshape(N_LRU_BLOCKS, 1, LRU_BLOCK) for bb in (b_r[0], b_i[0], b_r[1], b_i[1])], axis=-1)
    parts, rest = [], bias
    for _ in range(BIAS_SPLIT):
        gamma = rest * (2.0 ** 16 + 1.0)
        part = gamma - (gamma - rest)
        parts.append(part)
        rest = rest - part
    bias_rows = jnp.pad(jnp.concatenate(parts, axis=1), ((0, 0), (0, LRU_BLOCK - BIAS_SPLIT), (0, 0)))
    return jnp.concatenate([weights, bias_rows], axis=1).astype(BF16)


def _lru(x_lru, g_lru, conv_w, conv_b, w_ext, lru_lam, h0):
    b, t_len, _ = x_lru.shape
    chunk = 64 if t_len >= 2048 else 32
    pitch = chunk + SUBLANES // 2
    n_chunks = t_len // chunk
    rows = min(4096, t_len)
    blocks_per_step = N_LRU_BLOCKS if t_len <= 512 else 1
    width = blocks_per_step * LRU_BLOCK
    seq_spec = pl.BlockSpec((None, t_len, width), lambda bi, n: (bi, 0, n))
    st_spec = pl.BlockSpec((None, 2, width), lambda bi, n: (bi, 0, n))
    scan_buf = pltpu.VMEM((blocks_per_step, n_chunks * pitch, LRU_BLOCK), F32)
    return pl.pallas_call(
        functools.partial(_lru_kernel, blocks_per_step=blocks_per_step,
                          t_len=t_len, chunk=chunk, pitch=pitch, rows=rows),
        grid=(b, N_LRU_BLOCKS // blocks_per_step),
        in_specs=[
            seq_spec, seq_spec,
            pl.BlockSpec((4, width), lambda bi, n: (0, n)),
            pl.BlockSpec((1, width), lambda bi, n: (0, n)),
            pl.BlockSpec((blocks_per_step, 2 * LRU_BLOCK, 4 * LRU_BLOCK), lambda bi, n: (n, 0, 0)),
            pl.BlockSpec((2, width), lambda bi, n: (0, n)),
            st_spec,
        ],
        out_specs=[seq_spec, st_spec],
        out_shape=[jax.ShapeDtypeStruct((b, t_len, D_LRU), BF16),
                   jax.ShapeDtypeStruct((b, 2, D_LRU), F32)],
        scratch_shapes=[
            pltpu.VMEM((blocks_per_step, t_len + 2 * SUBLANES, LRU_BLOCK), F32),
            scan_buf, scan_buf, scan_buf, scan_buf,
            pltpu.VMEM((blocks_per_step, n_chunks, LRU_BLOCK), F32),
            pltpu.VMEM((blocks_per_step, n_chunks, LRU_BLOCK), F32),
        ],
        compiler_params=pltpu.CompilerParams(
            dimension_semantics=("arbitrary",) * 2, vmem_limit_bytes=V7X_VMEM_LIMIT_BYTES),
        name="rglru",
    )(x_lru, g_lru, conv_w, conv_b, w_ext, lru_lam, h0)


def _outproj_kernel(att_ref, lru_ref, x_ref, gate_ref, w_ref, gpost_ref, y_ref):
    o = _dot(jnp.concatenate([att_ref[...], lru_ref[...]], axis=1), w_ref[...])
    ms = jnp.mean(o * o, axis=-1, keepdims=True)
    n = o * lax.rsqrt(ms + EPS) * gpost_ref[...]
    y_ref[...] = x_ref[...] + gate_ref[...] * n


def _outproj(att, lru, x, gate, w_out_bf16, g_post, *, t_len, tm=1024):
    n_tok = x.shape[0]
    nb = gate.shape[0]
    if nb == 1:
        tm = min(tm, n_tok)
        mod_map = lambda i: (0, 0, 0)
    else:
        tm = min(tm, t_len)
        tiles_per_batch = t_len // tm
        mod_map = lambda i: (i // tiles_per_batch, 0, 0)
    tok_spec = pl.BlockSpec((tm, D_MODEL), lambda i: (i, 0))
    return pl.pallas_call(
        _outproj_kernel,
        grid=(n_tok // tm,),
        in_specs=[tok_spec, tok_spec, tok_spec,
                  pl.BlockSpec((None, 1, D_MODEL), mod_map),
                  pl.BlockSpec((D_ATT + D_LRU, D_MODEL), lambda i: (0, 0)),
                  pl.BlockSpec((1, D_MODEL), lambda i: (0, 0))],
        out_specs=tok_spec,
        out_shape=jax.ShapeDtypeStruct((n_tok, D_MODEL), F32),
        compiler_params=pltpu.CompilerParams(
            dimension_semantics=("arbitrary",), vmem_limit_bytes=V7X_VMEM_LIMIT_BYTES),
        name="outproj",
    )(att, lru, x, gate, w_out_bf16, g_post)


def _sublayer(x, scale, shift, gate, cache_k, cache_v, h0, use_rope, w):
    b, t_len, _ = x.shape
    xf = x.reshape(b * t_len, D_MODEL)
    emit_kv = cache_k is None
    outs = _inproj(xf, scale, shift, w["g_pre"], w["w_in"], t_len=t_len,
                   use_rope=use_rope, emit_kv_f32=emit_kv)
    qt, vt = outs[0], outs[2]
    k, g_att, x_lru, g_lru = [outs[i].reshape(b, t_len, D_MODEL) for i in (1, 3, 4, 5)]
    att = _attention(w["lam"], w["g_subln"], qt, k, vt, g_att, cache_k, cache_v)
    lru, state = _lru(x_lru, g_lru, w["conv_w"], w["conv_b"], w["w_ext"], w["lru_lam"], h0)
    y = _outproj(att.reshape(b * t_len, D_ATT), lru.reshape(b * t_len, D_LRU), xf, gate,
                 w["w_out"], w["g_post"], t_len=t_len)
    y = y.reshape(b, t_len, D_MODEL)
    if emit_kv:
        return y, outs[6], outs[7], state
    return y, None, None, state


def kernel(x_prompt, x_sample, cache_k, cache_v, state_lru, c, c_ctx, w_ada, b_ada, g_pre, w_in, lambda_q1, lambda_k1, lambda_q2, lambda_k2, g_subln, conv_w, conv_b, w_rgate, b_rgate, w_igate, b_igate, lru_lambda, w_out, g_post):
    bp, seq, _ = x_prompt.shape
    bd, kc = cache_k.shape[0], cache_k.shape[2]
    assert cache_k.shape[1] == 1 and w_in.shape[0] == 1, "single-layer step only"
    l = 0

    cond = jnp.concatenate([c, c_ctx[None, :], jnp.zeros((SUBLANES - bd - 1, D_MODEL), F32)], axis=0)
    mod, w_in_bf16, w_out_bf16 = _modulation(cond, w_ada[l], b_ada[l][None, :], w_in[l], w_out[l])
    shift, scale, gate = [mod[:, i * D_MODEL:(i + 1) * D_MODEL] for i in range(3)]

    def rows(a, lo, hi):
        return a[lo:hi][:, None, :]

    w = {
        "g_pre": g_pre[l][None, :],
        "w_in": w_in_bf16,
        "lam": jnp.stack([lambda_q1[l], lambda_k1[l], lambda_q2[l], lambda_k2[l]], axis=0),
        "g_subln": g_subln[l][None, :],
        "conv_w": conv_w[l],
        "conv_b": conv_b[l][None, :],
        "w_ext": _lru_gate_operand(w_rgate[l], w_igate[l], b_rgate[l], b_igate[l]),
        "lru_lam": lru_lambda[l],
        "w_out": w_out_bf16,
        "g_post": g_post[l][None, :],
    }

    y_p, new_k, new_v, st_p = _sublayer(
        x_prompt, rows(scale, bd, bd + 1), rows(shift, bd, bd + 1), rows(gate, bd, bd + 1),
        None, None, jnp.zeros((bp, 2, D_LRU), F32), False, w)

    y_s, _, _, _ = _sublayer(
        x_sample, rows(scale, 0, bd), rows(shift, 0, bd), rows(gate, 0, bd),
        cache_k.reshape(bd, kc * N_HEADS, HEAD_W), cache_v.reshape(bd, kc * N_HEADS, HEAD_W),
        state_lru.reshape(bd, 2, D_LRU), True, w)

    new_k = new_k.reshape(bp, 1, seq, N_HEADS, HEAD_W)
    new_v = new_v.reshape(bp, 1, seq, N_HEADS, HEAD_W)
    return (y_p, y_s, new_k, new_v, st_p[:, None])
```
